```python
import functools
import jax, jax.numpy as jnp
from jax import lax
import numpy as np

D_MODEL = 1024
BATCH = 8
SEQ = 2048
DEPTH = 2

GRID_W = 64
CTX_LEN = 256
N_FGROUPS = 4
FGROUP_DIM = 128
F_WIDTH = N_FGROUPS * FGROUP_DIM
HEAD_DIM = 64
N_RHEADS = 8
R_WIDTH = N_RHEADS * HEAD_DIM
D_DECAY_LORA = 64
D_AAA_LORA = 64
D_GATE_LORA = 128
N_DIRS = 2
D_FF = 4 * D_MODEL
RWKV_IN = 3 * R_WIDTH + N_DIRS * (D_DECAY_LORA + D_AAA_LORA) + D_GATE_LORA
IN_WIDTH = F_WIDTH + RWKV_IN + 2 * D_MODEL
N_MOD = 6
NORM_EPS = 1e-6
GN_EPS = 64e-5
L2_EPS = 1e-12
RWKV_SPLITS = [R_WIDTH, 2 * R_WIDTH, 3 * R_WIDTH,
               3 * R_WIDTH + D_DECAY_LORA,
               3 * R_WIDTH + 2 * D_DECAY_LORA,
               3 * R_WIDTH + 2 * D_DECAY_LORA + D_AAA_LORA,
               3 * R_WIDTH + 2 * D_DECAY_LORA + 2 * D_AAA_LORA]

kernel_name = "hybrid_fourier_rwkv7_dit_prefix"


def rms_norm(x, g):
    x32 = x.astype(jnp.float32)
    y = x32 * lax.rsqrt(jnp.mean(x32 * x32, axis=-1, keepdims=True) + NORM_EPS)
    return (y * g.astype(jnp.float32)).astype(x.dtype)


def grid_shift(u):
    b, l, ch = u.shape
    rows = l // GRID_W
    q = u.reshape(b, rows, GRID_W, 4, ch // 4)
    left = jnp.pad(q[:, :, :-1, 0], ((0, 0), (0, 0), (1, 0), (0, 0)))
    right = jnp.pad(q[:, :, 1:, 1], ((0, 0), (0, 0), (0, 1), (0, 0)))
    up = jnp.pad(q[:, :-1, :, 2], ((0, 0), (1, 0), (0, 0), (0, 0)))
    down = jnp.pad(q[:, 1:, :, 3], ((0, 0), (0, 1), (0, 0), (0, 0)))
    return jnp.stack([left, right, up, down], axis=3).reshape(b, l, ch)


def seq_shift(u):
    b, l, ch = u.shape
    q = u.reshape(b, l, 2, ch // 2)
    prev = jnp.pad(q[:, :-1, 0], ((0, 0), (1, 0), (0, 0)))
    nxt = jnp.pad(q[:, 1:, 1], ((0, 0), (0, 1), (0, 0)))
    return jnp.stack([prev, nxt], axis=2).reshape(b, l, ch)


def fourier_mix(u):
    b, l, _ = u.shape
    ug = u.astype(jnp.float32).reshape(b, l, N_FGROUPS, FGROUP_DIM)
    f = jnp.fft.fft2(ug, axes=(1, 3), norm="ortho").real
    return f.reshape(b, l, F_WIDTH).astype(u.dtype)


def rwkv_scan(r, k, v, w, kk, a, s0):
    def step(s, inp):
        r_t, k_t, v_t, w_t, kk_t, a_t = inp
        sa = jnp.einsum('dbhvk,dbhk->dbhv', s, kk_t)
        s = (s * w_t[..., None, :] - sa[..., :, None] * (kk_t * a_t)[..., None, :]
             + v_t[..., :, None] * k_t[..., None, :])
        y = jnp.einsum('dbhvk,dbhk->dbhv', s, r_t)
        return s, y
    xs = tuple(jnp.moveaxis(t.astype(jnp.float32), 2, 0) for t in (r, k, v, w, kk, a))
    s_final, ys = lax.scan(step, s0, xs)
    return jnp.moveaxis(ys, 0, 2), s_final


def _dir_seq(t):
    return jnp.stack([t[0], jnp.flip(t[1], axis=1)], axis=0)


def _both(t):
    return jnp.stack([t, jnp.flip(t, axis=1)], axis=0)


def token_mixer(h, s0, shift_fn, need_out, w_in, mu_shift, w0, w_up, a0, a_up, g_up,
                k_k, k_a, r_k, ln_x_w, ln_x_b, w_fourier_up, w_rwkv_up, w_out):
    b, l, _ = h.shape
    proj = h @ w_in
    f_in = proj[..., :F_WIDTH]
    rw = proj[..., F_WIDTH:F_WIDTH + RWKV_IN]
    gates = proj[..., F_WIDTH + RWKV_IN:]
    rw = rw + mu_shift * (shift_fn(rw) - rw)
    r, k, v, wd_f, wd_b, ad_f, ad_b, gd = jnp.split(rw, RWKV_SPLITS, axis=-1)
    wd = jnp.stack([wd_f, wd_b], axis=0)
    ad = jnp.stack([ad_f, ad_b], axis=0)
    w_logit = w0[:, None, None, :] + jnp.einsum('dblr,drc->dblc', jnp.tanh(wd), w_up)
    decay = jnp.exp(-jnp.exp((-jax.nn.softplus(-w_logit) - 0.5).astype(jnp.float32)))
    a = jax.nn.sigmoid(a0[:, None, None, :] + jnp.einsum('dblr,drc->dblc', ad, a_up))
    k_dir = k[None] * (1 + (a - 1) * k_a)
    g = jax.nn.sigmoid(gd) @ g_up
    kk = (k * k_k).astype(jnp.float32).reshape(b, l, N_RHEADS, HEAD_DIM)
    kk = kk / jnp.maximum(jnp.sqrt(jnp.sum(kk * kk, axis=-1, keepdims=True)), L2_EPS)

    hd = lambda t: t.reshape(t.shape[:-1] + (N_RHEADS, HEAD_DIM))
    ys, s_final = rwkv_scan(_both(hd(r)), _dir_seq(hd(k_dir)), _both(hd(v)),
                            _dir_seq(hd(decay)), _both(kk), _dir_seq(hd(a)), s0)
    if not need_out:
        return None, s_final
    y = ys[0] + jnp.flip(ys[1], axis=1)
    mu = jnp.mean(y, axis=-1, keepdims=True)
    var = jnp.mean(jnp.square(y - mu), axis=-1, keepdims=True)
    yn = ((y - mu) * lax.rsqrt(var + GN_EPS)).reshape(b, l, R_WIDTH)
    yn = (yn * ln_x_w + ln_x_b).astype(h.dtype)
    bonus = jnp.sum(hd(r) * hd(k_dir[0] + k_dir[1]) * r_k, axis=-1, keepdims=True) * hd(v)
    rwkv_out = (yn + bonus.reshape(b, l, R_WIDTH)) * g
    f_out = fourier_mix(f_in)
    gate_f, gate_r = jnp.split(gates, 2, axis=-1)
    merged = (jax.nn.sigmoid(gate_f) * (f_out @ w_fourier_up)
              + jax.nn.sigmoid(gate_r) * (rwkv_out @ w_rwkv_up))
    return merged @ w_out, s_final


def sq_relu_mlp(h, w1, w2):
    return jnp.square(jax.nn.relu(h @ w1)) @ w2


def setup_inputs(seed: int = 0) -> dict:
    key = jax.random.key(seed)
    ks = jax.random.split(key, 32)
    nrm = lambda k, shape, s: jax.random.normal(k, shape, jnp.float32) * s
    return {
        "x": nrm(ks[0], (BATCH, SEQ, D_MODEL), 1.0),
        "c": nrm(ks[1], (BATCH, D_MODEL), 1.0),
        "ctx": nrm(ks[2], (BATCH, CTX_LEN, D_MODEL), 1.0),
        "c_ctx": nrm(ks[3], (D_MODEL,), 1.0),
        "w_mod": nrm(ks[4], (DEPTH, D_MODEL, N_MOD * D_MODEL), 0.5 * D_MODEL ** -0.5),
        "b_mod": nrm(ks[5], (DEPTH, N_MOD * D_MODEL), 0.02),
        "norm1": 1.0 + nrm(ks[6], (DEPTH, D_MODEL), 0.05),
        "norm2": 1.0 + nrm(ks[7], (DEPTH, D_MODEL), 0.05),
        "w_in": nrm(ks[8], (DEPTH, D_MODEL, IN_WIDTH), D_MODEL ** -0.5),
        "mu_shift": jax.random.uniform(ks[9], (DEPTH, RWKV_IN), jnp.float32),
        "w0": jax.random.uniform(ks[10], (DEPTH, N_DIRS, R_WIDTH), jnp.float32, -5.0, -0.5),
        "w_up": nrm(ks[11], (DEPTH, N_DIRS, D_DECAY_LORA, R_WIDTH), 0.5 * D_DECAY_LORA ** -0.5),
        "a0": nrm(ks[12], (DEPTH, N_DIRS, R_WIDTH), 0.3),
        "a_up": nrm(ks[13], (DEPTH, N_DIRS, D_AAA_LORA, R_WIDTH), 0.5 * D_AAA_LORA ** -0.5),
        "g_up": nrm(ks[14], (DEPTH, D_GATE_LORA, R_WIDTH), D_GATE_LORA ** -0.5),
        "k_k": 0.85 + nrm(ks[15], (DEPTH, R_WIDTH), 0.05),
        "k_a": 1.0 + nrm(ks[16], (DEPTH, R_WIDTH), 0.05),
        "r_k": nrm(ks[17], (DEPTH, N_RHEADS, HEAD_DIM), 0.1),
        "ln_x_w": 1.0 + nrm(ks[18], (DEPTH, R_WIDTH), 0.05),
        "ln_x_b": nrm(ks[19], (DEPTH, R_WIDTH), 0.02),
        "w_fourier_up": nrm(ks[20], (DEPTH, F_WIDTH, D_MODEL), F_WIDTH ** -0.5),
        "w_rwkv_up": nrm(ks[21], (DEPTH, R_WIDTH, D_MODEL), R_WIDTH ** -0.5),
        "w_out": nrm(ks[22], (DEPTH, D_MODEL, D_MODEL), D_MODEL ** -0.5),
        "mlp_w1": nrm(ks[23], (DEPTH, D_MODEL, D_FF), D_MODEL ** -0.5),
        "mlp_w2": nrm(ks[24], (DEPTH, D_FF, D_MODEL), D_FF ** -0.5),
        "norm_f": 1.0 + nrm(ks[25], (D_MODEL,), 0.05),
    }


def reference(x, c, ctx, c_ctx, w_mod, b_mod, norm1, norm2, w_in, mu_shift, w0, w_up, a0,
              a_up, g_up, k_k, k_a, r_k, ln_x_w, ln_x_b, w_fourier_up, w_rwkv_up, w_out,
              mlp_w1, mlp_w2, norm_f):
    x_lat, x_ctx = x, ctx
    s0 = jnp.zeros((N_DIRS, x.shape[0], N_RHEADS, HEAD_DIM, HEAD_DIM), jnp.float32)
    for l in range(DEPTH):
        last = l == DEPTH - 1
        mod = jax.nn.silu(c) @ w_mod[l] + b_mod[l]
        sh1, sc1, g1, sh2, sc2, g2 = jnp.split(mod[:, None, :], N_MOD, axis=-1)
        mod_c = jax.nn.silu(c_ctx) @ w_mod[l] + b_mod[l]
        ch1, cs1, cg1, ch2, cs2, cg2 = jnp.split(mod_c, N_MOD, axis=-1)
        mixer = functools.partial(
            token_mixer, w_in=w_in[l], mu_shift=mu_shift[l], w0=w0[l], w_up=w_up[l],
            a0=a0[l], a_up=a_up[l], g_up=g_up[l], k_k=k_k[l], k_a=k_a[l], r_k=r_k[l],
            ln_x_w=ln_x_w[l], ln_x_b=ln_x_b[l], w_fourier_up=w_fourier_up[l],
            w_rwkv_up=w_rwkv_up[l], w_out=w_out[l])
        h_c = rms_norm(x_ctx, norm1[l]) * (1 + cs1) + ch1
        out_c, s_ctx = mixer(h_c, s0, seq_shift, not last)
        h = rms_norm(x_lat, norm1[l]) * (1 + sc1) + sh1
        out, _ = mixer(h, s_ctx, grid_shift, True)
        x_lat = x_lat + g1 * out
        h = rms_norm(x_lat, norm2[l]) * (1 + sc2) + sh2
        x_lat = x_lat + g2 * sq_relu_mlp(h, mlp_w1[l], mlp_w2[l])
        if not last:
            x_ctx = x_ctx + cg1 * out_c
            h_c = rms_norm(x_ctx, norm2[l]) * (1 + cs2) + ch2
            x_ctx = x_ctx + cg2 * sq_relu_mlp(h_c, mlp_w1[l], mlp_w2[l])
    return rms_norm(x_lat, norm_f)
```

```python
import functools

import numpy as np
import jax
import jax.numpy as jnp
from jax import lax
from jax.experimental import pallas as pl
from jax.experimental.pallas import tpu as pltpu

F32 = jnp.float32
BF16 = jnp.bfloat16

D_MODEL = 1024
GRID_W = 64
F_WIDTH = 512
FGROUP_DIM = 128
HEAD_DIM = 64
N_RHEADS = 8
R_WIDTH = N_RHEADS * HEAD_DIM
D_LORA = 64
D_GATE_LORA = 128
RWKV_IN = 3 * R_WIDTH + 4 * D_LORA + D_GATE_LORA
GATE_W = 2 * D_MODEL
D_FF = 4 * D_MODEL
N_MOD = 6
NORM_EPS = 1e-6
GN_EPS = 64e-5
L2_EPS = 1e-12

CHUNK = 64
GROUP_LANES = 256
HEADS_PER_GROUP = GROUP_LANES // HEAD_DIM
HEAD_SHIFT = HEAD_DIM.bit_length() - 1
N_GROUPS = R_WIDTH // GROUP_LANES
MOD_ROWS = 16
VMEM_LIMIT = 56 * 1024 * 1024


def _bf(x):
    return x.astype(BF16)


def _dot(a, b):
    return jnp.dot(a, b, preferred_element_type=F32)


def _dot_nt(a, b):
    return lax.dot_general(a, b, (((1,), (1,)), ((), ())), preferred_element_type=F32)


def _dot_tn(a, b):
    return lax.dot_general(a, b, (((0,), (0,)), ((), ())), preferred_element_type=F32)


def _split2(x):
    hi = _bf(x)
    lo = _bf(x - hi.astype(F32))
    return hi, lo


def _split3(x):
    hi = _bf(x)
    r1 = x - hi.astype(F32)
    mid = _bf(r1)
    lo = _bf(r1 - mid.astype(F32))
    return hi, mid, lo


def _dot_exact_rhs(x, w):
    hi, lo = _split2(x)
    return _dot(hi, w) + _dot(lo, w)


def _params(*sem):
    return pltpu.CompilerParams(dimension_semantics=sem, vmem_limit_bytes=VMEM_LIMIT)


def _const_spec(shape):
    zeros = (0,) * len(shape)
    return pl.BlockSpec(shape, lambda *_: zeros)


def _linear_kernel(x_ref, w_ref, b_ref, o_ref, *, silu):
    x = x_ref[...]
    if silu:
        x = x * jax.nn.sigmoid(x)
    o_ref[...] = _dot(_bf(x), _bf(w_ref[...])) + b_ref[...]


def _modulation(cc, w, b):
    n = w.shape[1]
    tn = 512
    return pl.pallas_call(
        functools.partial(_linear_kernel, silu=True),
        grid=(n // tn,),
        in_specs=[_const_spec(cc.shape),
                  pl.BlockSpec((w.shape[0], tn), lambda j: (0, j)),
                  pl.BlockSpec((1, tn), lambda j: (0, j))],
        out_specs=pl.BlockSpec((cc.shape[0], tn), lambda j: (0, j)),
        out_shape=jax.ShapeDtypeStruct((cc.shape[0], n), F32),
        compiler_params=_params("parallel"),
        name="modulation",
    )(cc, w, b.reshape(1, n))


def _fold_kernel(w_ref, c_ref, o_ref):
    wh, wl = _split2(w_ref[...])
    ch, cl = _split2(c_ref[...])
    o_ref[...] = _bf(_dot(wh, ch) + _dot(wl, ch) + _dot(wh, cl))


def _fold_channel_dft(w_f, cdft):
    return pl.pallas_call(
        _fold_kernel,
        grid=(1,),
        in_specs=[_const_spec(w_f.shape), _const_spec(cdft.shape)],
        out_specs=_const_spec((w_f.shape[0], cdft.shape[1])),
        out_shape=jax.ShapeDtypeStruct((w_f.shape[0], cdft.shape[1]), BF16),
        compiler_params=_params("arbitrary"),
        name="fold_channel_dft",
    )(w_f, cdft)


def _rms(x):
    return x * lax.rsqrt(jnp.mean(x * x, axis=-1, keepdims=True) + NORM_EPS)


def _inproj_kernel(x_ref, mod_ref, n1_ref, wrw_ref, *rest, need_out):
    if need_out:
        wz_ref, wg_ref, rw_ref, z_ref, g_ref = rest
    else:
        (rw_ref,) = rest
    x = x_ref[0]
    shift = mod_ref[0, 0:1, :]
    scale = mod_ref[0, 1:2, :]
    h = _bf(_rms(x) * n1_ref[...] * (1.0 + scale) + shift)
    rw_ref[0] = _dot(h, wrw_ref[...])
    if need_out:
        z = _dot(h, wz_ref[...])
        z_ref[0, 0] = _bf(z[:, :F_WIDTH])
        z_ref[0, 1] = _bf(z[:, F_WIDTH:])
        g_ref[0] = _bf(_dot(h, wg_ref[...]))


def _inproj(x, mod, mod_row, n1, wrw, wz, wg, need_out):
    b, l, d = x.shape
    tm = min(256, l)
    row_spec = lambda w: pl.BlockSpec((1, tm, w), lambda bi, i: (bi, i, 0))
    in_specs = [row_spec(d),
                pl.BlockSpec((1, N_MOD, d), lambda bi, i: (mod_row(bi), 0, 0)),
                _const_spec((1, d)), _const_spec(wrw.shape)]
    args = [x, mod, n1, wrw]
    out_specs = [row_spec(RWKV_IN)]
    out_shape = [jax.ShapeDtypeStruct((b, l, RWKV_IN), F32)]
    if need_out:
        in_specs += [_const_spec(wz.shape), _const_spec(wg.shape)]
        args += [wz, wg]
        out_specs += [pl.BlockSpec((1, 2, tm, F_WIDTH), lambda bi, i: (bi, 0, i, 0)), row_spec(GATE_W)]
        out_shape += [jax.ShapeDtypeStruct((b, 2, l, F_WIDTH), BF16),
                      jax.ShapeDtypeStruct((b, l, GATE_W), BF16)]
    return pl.pallas_call(
        functools.partial(_inproj_kernel, need_out=need_out),
        grid=(b, l // tm),
        in_specs=in_specs, out_specs=out_specs, out_shape=out_shape,
        compiler_params=_params("parallel", "parallel"),
        name="inproj",
    )(*args)


def _log_sigmoid(x):
    return jnp.minimum(x, 0.0) - jnp.log1p(jnp.exp(-jnp.abs(x)))


def _prep_kernel(*refs, grid_mode, seq_len, tm):
    if grid_mode:
        (rw_ref, prev_ref, next_ref, mu_ref, w0_ref, a0_ref, wup_ref, aup_ref, gup_ref,
         kkw_ref, ka_ref, rk_ref, eh_ref,
         r_ref, v_ref, kk_ref, kd_ref, lw_ref, bd_ref, g_ref, bonus_ref) = refs
    else:
        (rw_ref, mu_ref, w0_ref, a0_ref, wup_ref, aup_ref, gup_ref,
         kkw_ref, ka_ref, rk_ref, eh_ref,
         r_ref, v_ref, kk_ref, kd_ref, lw_ref, bd_ref, g_ref, bonus_ref) = refs
    i = pl.program_id(1)
    x = rw_ref[0]
    t_loc = lax.broadcasted_iota(jnp.int32, (tm, 128), 0)
    t_glob = t_loc + i * tm
    lane = lax.broadcasted_iota(jnp.int32, (tm, 128), 1)
    if grid_mode:
        ext = jnp.concatenate([prev_ref[0], x, next_ref[0]], axis=0)
        n_ext = tm + 2 * GRID_W
        col = jnp.bitwise_and(t_loc, GRID_W - 1)
        masks = [col != 0, col != GRID_W - 1, t_glob >= GRID_W, t_glob < seq_len - GRID_W]
        n_parts = 4
    else:
        ext = x
        n_ext = tm
        masks = [t_glob != 0, t_glob != seq_len - 1]
        n_parts = 2
    part_w = RWKV_IN // n_parts

    def shifted(j, part):
        e = ext[:, 128 * j:128 * (j + 1)]
        if grid_mode:
            if part == 0:
                s = pltpu.roll(e, 1, 0)[GRID_W:GRID_W + tm]
            elif part == 1:
                s = pltpu.roll(e, n_ext - 1, 0)[GRID_W:GRID_W + tm]
            elif part == 2:
                s = e[0:tm]
            else:
                s = e[2 * GRID_W:2 * GRID_W + tm]
        else:
            s = pltpu.roll(e, 1, 0) if part == 0 else pltpu.roll(e, n_ext - 1, 0)
        return jnp.where(masks[part], s, 0.0)

    blocks = []
    for j in range(RWKV_IN // 128):
        p_lo = (128 * j) // part_w
        p_hi = (128 * j + 127) // part_w
        s = shifted(j, p_lo)
        if p_hi != p_lo:
            s = jnp.where(lane + 128 * j < part_w * p_hi, s, shifted(j, p_hi))
        xj = x[:, 128 * j:128 * (j + 1)]
        blocks.append(xj + mu_ref[:, 128 * j:128 * (j + 1)] * (s - xj))

    nb = R_WIDTH // 128
    r = jnp.concatenate(blocks[0:nb], axis=1)
    k = jnp.concatenate(blocks[nb:2 * nb], axis=1)
    v = jnp.concatenate(blocks[2 * nb:3 * nb], axis=1)
    wd, ad, gd = blocks[3 * nb], blocks[3 * nb + 1], blocks[3 * nb + 2]

    w_logit = w0_ref[...] + _dot(_bf(jnp.tanh(wd)), wup_ref[...])
    lw = -jnp.exp(_log_sigmoid(w_logit) - 0.5)
    a = jax.nn.sigmoid(a0_ref[...] + _dot(_bf(ad), aup_ref[...]))
    g = _dot(_bf(jax.nn.sigmoid(gd)), gup_ref[...])

    eh = eh_ref[...]
    kx = k * kkw_ref[...]
    nrm = jnp.maximum(jnp.sqrt(_dot_exact_rhs(kx * kx, eh)), L2_EPS)
    kk = kx / nrm
    ka = ka_ref[...]
    kd0 = k * (1.0 + (a[:, :R_WIDTH] - 1.0) * ka)
    kd1 = k * (1.0 + (a[:, R_WIDTH:] - 1.0) * ka)
    bonus = _dot_exact_rhs(r * (kd0 + kd1) * rk_ref[...], eh) * v

    r_ref[0] = r
    v_ref[0] = v
    kk_ref[0] = kk
    kd_ref[0, 0] = kd0
    kd_ref[1, 0] = kd1
    lw_ref[0, 0] = lw[:, :R_WIDTH]
    lw_ref[1, 0] = lw[:, R_WIDTH:]
    bd_ref[0, 0] = kk * a[:, :R_WIDTH]
    bd_ref[1, 0] = kk * a[:, R_WIDTH:]
    g_ref[0] = g
    bonus_ref[0] = bonus


def _prep(rw, grid_mode, mu, w0c, a0c, wup, aup, gup, kkw, ka, rk, eh):
    b, l, _ = rw.shape
    tm = min(256, l) if grid_mode else l
    nrow = l // GRID_W
    per = tm // GRID_W
    row_spec = pl.BlockSpec((1, tm, RWKV_IN), lambda bi, i: (bi, i, 0))
    in_specs = [row_spec]
    args = [rw]
    if grid_mode:
        in_specs += [
            pl.BlockSpec((1, GRID_W, RWKV_IN), lambda bi, i: (bi, jnp.maximum(i * per - 1, 0), 0)),
            pl.BlockSpec((1, GRID_W, RWKV_IN), lambda bi, i: (bi, jnp.minimum((i + 1) * per, nrow - 1), 0)),
        ]
        args += [rw, rw]
    consts = [mu, w0c, a0c, wup, aup, gup, kkw, ka, rk, eh]
    in_specs += [_const_spec(c.shape) for c in consts]
    args += consts
    one = pl.BlockSpec((1, tm, R_WIDTH), lambda bi, i: (bi, i, 0))
    two = pl.BlockSpec((2, 1, tm, R_WIDTH), lambda bi, i: (0, bi, i, 0))
    s1 = jax.ShapeDtypeStruct((b, l, R_WIDTH), F32)
    s2 = jax.ShapeDtypeStruct((2, b, l, R_WIDTH), F32)
    return pl.pallas_call(
        functools.partial(_prep_kernel, grid_mode=grid_mode, seq_len=l, tm=tm),
        grid=(b, l // tm),
        in_specs=in_specs,
        out_specs=[one, one, one, two, two, two, one, one],
        out_shape=[s1, s1, s1, s2, s2, s2, s1, s1],
        compiler_params=_params("parallel", "parallel"),
        name="rwkv_prep",
    )(*args)


def _block_diag_mask():
    rr = jnp.right_shift(lax.broadcasted_iota(jnp.int32, (GROUP_LANES, GROUP_LANES), 0), HEAD_SHIFT)
    cc = jnp.right_shift(lax.broadcasted_iota(jnp.int32, (GROUP_LANES, GROUP_LANES), 1), HEAD_SHIFT)
    return rr == cc


def _block_diag(x, mask):
    tiled = jnp.concatenate([x] * HEADS_PER_GROUP, axis=0)
    return jnp.where(mask, tiled, jnp.zeros_like(tiled))


def _diag_blocks(q, lane_head):
    out = jnp.where(lane_head == 0, q[0:HEAD_DIM], 0.0)
    for h in range(1, HEADS_PER_GROUP):
        out = out + jnp.where(lane_head == h, q[h * HEAD_DIM:(h + 1) * HEAD_DIM], 0.0)
    return out


def _chunk_kernel(r_ref, v_ref, kk_ref, kd_ref, lw_ref, bd_ref,
                  reff_ref, yloc_ref, m_ref, g_ref, *, tc):
    sign = 1 - 2 * pl.program_id(0)
    t = lax.broadcasted_iota(jnp.int32, (CHUNK, GROUP_LANES), 0)
    lane = lax.broadcasted_iota(jnp.int32, (CHUNK, GROUP_LANES), 1)
    s = jnp.bitwise_and(lane, CHUNK - 1)
    lane_head = jnp.right_shift(lane, HEAD_SHIFT)
    before = (s - t) * sign < 0
    upto = (s - t) * sign <= 0
    eye = jnp.where(s == t, 1.0, 0.0)
    bdmask = _block_diag_mask()
    t64 = lax.broadcasted_iota(jnp.int32, (CHUNK, CHUNK), 0)
    s64 = lax.broadcasted_iota(jnp.int32, (CHUNK, CHUNK), 1)
    tri = jnp.where((s64 - t64) * sign <= 0, 1.0, 0.0).astype(BF16)

    for ci in range(tc // CHUNK):
        rows = slice(ci * CHUNK, (ci + 1) * CHUNK)
        lw_all = lw_ref[0, 0, rows, :]
        h3 = _split3(lw_all)
        cum_all = _dot(tri, h3[0]) + _dot(tri, h3[1]) + _dot(tri, h3[2])
        for gi in range(N_GROUPS):
            lanes = slice(gi * GROUP_LANES, (gi + 1) * GROUP_LANES)
            r = r_ref[0, rows, lanes]
            v = v_ref[0, rows, lanes]
            kap = kk_ref[0, rows, lanes]
            k = kd_ref[0, 0, rows, lanes]
            b = bd_ref[0, 0, rows, lanes]
            lw = lw_all[:, lanes]
            cum = cum_all[:, lanes]
            tot = jnp.sum(lw, axis=0, keepdims=True)
            e_neg = jnp.exp(-cum)
            e_tot = jnp.exp(tot)
            kt = kap * jnp.exp(cum - lw)
            rt = r * jnp.exp(cum)
            kh = k * e_neg
            bh = b * e_neg
            kb = kh * e_tot
            bb = bh * e_tot

            lhs = _bf(jnp.concatenate([kt, rt], axis=0))
            sk = _dot_nt(lhs, _block_diag(_bf(kh), bdmask))
            sb = _dot_nt(lhs, _block_diag(_bf(bh), bdmask))
            a_k = jnp.where(before, sk[:CHUNK], 0.0)
            a_rk = jnp.where(upto, sk[CHUNK:], 0.0)
            a_b = jnp.where(before, sb[:CHUNK], 0.0)
            a_rb = jnp.where(upto, sb[CHUNK:], 0.0)

            n = -a_b
            tinv = eye + n
            p = _dot(_bf(n), _block_diag(_bf(n), bdmask))
            for _ in range(4):
                tp = _dot(_bf(jnp.concatenate([tinv, p], axis=0)), _block_diag(_bf(p), bdmask))
                tinv = tinv + tp[:CHUNK]
                p = tp[CHUNK:]
            tinv = tinv + _dot(_bf(tinv), _block_diag(_bf(p), bdmask))

            av = _dot(_bf(jnp.concatenate([a_k, a_rk], axis=0)), _block_diag(_bf(v), bdmask))
            akv = av[:CHUNK]
            arkv = av[CHUNK:]
            tb = _bf(tinv)
            uk = _dot(tb, _block_diag(_bf(kt), bdmask))
            uv = _dot(tb, _block_diag(_bf(akv), bdmask))
            arb = _bf(a_rb)
            reff = rt - _dot(arb, _block_diag(_bf(uk), bdmask))
            yloc = arkv - _dot(arb, _block_diag(_bf(uv), bdmask))

            q = _dot_tn(_bf(bb), _bf(uk))
            m = eye * e_tot - _diag_blocks(q, lane_head)
            q2 = _dot_tn(_bf(jnp.concatenate([kb, -bb], axis=0)), _bf(jnp.concatenate([v, uv], axis=0)))
            g = _diag_blocks(q2, lane_head)

            reff_ref[0, 0, rows, lanes] = reff
            yloc_ref[0, 0, rows, lanes] = yloc
            m_ref[0, 0, rows, lanes] = m
            g_ref[0, 0, rows, lanes] = g


def _chunk_pass(r, v, kk, kd, lw, bd):
    b, l, _ = r.shape
    tc = min(128, l)
    one = pl.BlockSpec((1, tc, R_WIDTH), lambda d, bi, i: (bi, i, 0))
    two = pl.BlockSpec((1, 1, tc, R_WIDTH), lambda d, bi, i: (d, bi, i, 0))
    s2 = jax.ShapeDtypeStruct((2, b, l, R_WIDTH), F32)
    return pl.pallas_call(
        functools.partial(_chunk_kernel, tc=tc),
        grid=(2, b, l // tc),
        in_specs=[one, one, one, two, two, two],
        out_specs=[two, two, two, two],
        out_shape=[s2, s2, s2, s2],
        compiler_params=_params("parallel", "parallel", "parallel"),
        name="rwkv_chunk",
    )(r, v, kk, kd, lw, bd)


def _state_kernel(m0_ref, g0_ref, re0_ref, yl0_ref, m1_ref, g1_ref, re1_ref, yl1_ref, s0_ref,
                  y0_ref, y1_ref, h_ref, *, batch):
    c = pl.program_id(0)

    @pl.when(c == 0)
    def _():
        h_ref[...] = s0_ref[...]

    bdmask = _block_diag_mask()
    per_dir = ((m0_ref, g0_ref, re0_ref, yl0_ref, y0_ref), (m1_ref, g1_ref, re1_ref, yl1_ref, y1_ref))
    for d, (m_ref, g_ref, re_ref, yl_ref, y_ref) in enumerate(per_dir):
        def body(bi, carry, d=d, m_ref=m_ref, g_ref=g_ref, re_ref=re_ref, yl_ref=yl_ref, y_ref=y_ref):
            for gi in range(N_GROUPS):
                lanes = slice(gi * GROUP_LANES, (gi + 1) * GROUP_LANES)
                hh, hl = _split2(h_ref[d, bi, :, lanes])
                mh, ml = _split2(m_ref[0, bi, :, lanes])
                rh, rl = _split2(re_ref[0, bi, :, lanes])
                o1 = _dot(jnp.concatenate([mh, ml, rh, rl], axis=0), _block_diag(hh, bdmask))
                o2 = _dot(jnp.concatenate([mh, rh], axis=0), _block_diag(hl, bdmask))
                mh_new = o1[0:CHUNK] + o1[CHUNK:2 * CHUNK] + o2[0:CHUNK]
                rh_new = o1[2 * CHUNK:3 * CHUNK] + o1[3 * CHUNK:] + o2[CHUNK:]
                y_ref[bi, :, lanes] = yl_ref[0, bi, :, lanes] + rh_new
                h_ref[d, bi, :, lanes] = mh_new + g_ref[0, bi, :, lanes]
            return carry
        lax.fori_loop(0, batch, body, 0)


def _state_pass(m, g, reff, yloc, s0):
    _, b, l, _ = m.shape
    nc = l // CHUNK
    blk = (1, b, CHUNK, R_WIDTH)
    fwd = pl.BlockSpec(blk, lambda c: (0, 0, c, 0))
    bwd = pl.BlockSpec(blk, lambda c: (1, 0, nc - 1 - c, 0))
    st = pl.BlockSpec((2, b, HEAD_DIM, R_WIDTH), lambda c: (0, 0, 0, 0))
    sy = jax.ShapeDtypeStruct((b, l, R_WIDTH), F32)
    return pl.pallas_call(
        functools.partial(_state_kernel, batch=b),
        grid=(nc,),
        in_specs=[fwd, fwd, fwd, fwd, bwd, bwd, bwd, bwd, st],
        out_specs=[pl.BlockSpec((b, CHUNK, R_WIDTH), lambda c: (0, c, 0)),
                   pl.BlockSpec((b, CHUNK, R_WIDTH), lambda c: (0, nc - 1 - c, 0)),
                   st],
        out_shape=[sy, sy, jax.ShapeDtypeStruct((2, b, HEAD_DIM, R_WIDTH), F32)],
        compiler_params=_params("arbitrary"),
        name="rwkv_state",
    )(m, g, reff, yloc, m, g, reff, yloc, s0)


def _posdft_kernel(cs_ref, z_ref, o_ref):
    l = z_ref.shape[2]
    z = z_ref[0].reshape(2 * l, F_WIDTH)
    o_ref[0] = _bf(_dot(cs_ref[...], z))


def _pos_dft(z, cs):
    b, _, l, _ = z.shape
    tm = min(256, l)
    return pl.pallas_call(
        _posdft_kernel,
        grid=(b, l // tm),
        in_specs=[pl.BlockSpec((tm, 2 * l), lambda bi, i: (i, 0)),
                  pl.BlockSpec((1, 2, l, F_WIDTH), lambda bi, i: (bi, 0, 0, 0))],
        out_specs=pl.BlockSpec((1, tm, F_WIDTH), lambda bi, i: (bi, i, 0)),
        out_shape=jax.ShapeDtypeStruct((b, l, F_WIDTH), BF16),
        compiler_params=_params("parallel", "parallel"),
        name="pos_dft",
    )(cs, z)


def _merge_kernel(f_ref, y0_ref, y1_ref, g_ref, bonus_ref, gates_ref, x_ref, mod_ref,
                  lnw_ref, lnb_ref, eh_ref, wf_ref, wr_ref, wo_ref, o_ref):
    y = y0_ref[0] + y1_ref[0]
    eh = eh_ref[...]
    inv_n = 1.0 / HEAD_DIM
    mu = _dot_exact_rhs(y, eh) * inv_n
    dlt = y - mu
    var = _dot_exact_rhs(dlt * dlt, eh) * inv_n
    yn = dlt * lax.rsqrt(var + GN_EPS) * lnw_ref[...] + lnb_ref[...]
    rwkv = _bf((yn + bonus_ref[0]) * g_ref[0])
    fo = _dot(f_ref[0], wf_ref[...])
    ro = _dot(rwkv, wr_ref[...])
    gates = gates_ref[0].astype(F32)
    merged = jax.nn.sigmoid(gates[:, :D_MODEL]) * fo + jax.nn.sigmoid(gates[:, D_MODEL:]) * ro
    out = _dot(_bf(merged), wo_ref[...])
    o_ref[0] = x_ref[0] + mod_ref[0, 2:3, :] * out


def _merge(f, y0, y1, g, bonus, gates, x, mod, mod_row, lnw, lnb, eh, wf, wr, wo):
    b, l, d = x.shape
    tm = min(256, l)
    row = lambda w: pl.BlockSpec((1, tm, w), lambda bi, i: (bi, i, 0))
    consts = [lnw, lnb, eh, wf, wr, wo]
    return pl.pallas_call(
        _merge_kernel,
        grid=(b, l // tm),
        in_specs=[row(F_WIDTH), row(R_WIDTH), row(R_WIDTH), row(R_WIDTH), row(R_WIDTH), row(GATE_W), row(d),
                  pl.BlockSpec((1, N_MOD, d), lambda bi, i: (mod_row(bi), 0, 0))]
                 + [_const_spec(c.shape) for c in consts],
        out_specs=row(d),
        out_shape=jax.ShapeDtypeStruct((b, l, d), F32),
        compiler_params=_params("parallel", "parallel"),
        name="merge",
    )(f, y0, y1, g, bonus, gates, x, mod, *consts)


def _mlp_kernel(x_ref, mod_ref, n2_ref, w1_ref, w2_ref, nf_ref, o_ref, *, final_norm):
    x = x_ref[0]
    h = _bf(_rms(x) * n2_ref[...] * (1.0 + mod_ref[0, 4:5, :]) + mod_ref[0, 3:4, :])
    acc = jnp.zeros(x.shape, F32)
    step = 1024
    for j in range(D_FF // step):
        u = jnp.maximum(_dot(h, w1_ref[:, j * step:(j + 1) * step]), 0.0)
        acc = acc + _dot(_bf(u * u), w2_ref[j * step:(j + 1) * step, :])
    x2 = x + mod_ref[0, 5:6, :] * acc
    if final_norm:
        x2 = _rms(x2) * nf_ref[...]
    o_ref[0] = x2


def _mlp(x, mod, mod_row, n2, w1, w2, nf, final_norm):
    b, l, d = x.shape
    tm = min(512, l)
    row = pl.BlockSpec((1, tm, d), lambda bi, i: (bi, i, 0))
    once = lambda shape: pl.BlockSpec(shape, lambda bi, i: (0, 0), pipeline_mode=pl.Buffered(1))
    return pl.pallas_call(
        functools.partial(_mlp_kernel, final_norm=final_norm),
        grid=(b, l // tm),
        in_specs=[row, pl.BlockSpec((1, N_MOD, d), lambda bi, i: (mod_row(bi), 0, 0)),
                  _const_spec((1, d)), once(w1.shape), once(w2.shape), _const_spec((1, d))],
        out_specs=row,
        out_shape=jax.ShapeDtypeStruct((b, l, d), F32),
        compiler_params=_params("parallel", "parallel"),
        name="mlp",
    )(x, mod, n2, w1, w2, nf)


def _channel_dft():
    n = FGROUP_DIM
    jk = np.outer(np.arange(n), np.arange(n)) % n
    ang = 2.0 * np.pi * jk / n
    c = np.cos(ang) / np.sqrt(n)
    s = np.sin(ang) / np.sqrt(n)
    g = F_WIDTH // n
    out = np.zeros((F_WIDTH, 2 * F_WIDTH), np.float32)
    for i in range(g):
        out[i * n:(i + 1) * n, i * n:(i + 1) * n] = c
        out[i * n:(i + 1) * n, F_WIDTH + i * n:F_WIDTH + (i + 1) * n] = s
    return jnp.asarray(out)


def _position_dft(l):
    jk = np.outer(np.arange(l), np.arange(l)) % l
    ang = 2.0 * np.pi * jk / l
    cs = np.concatenate([np.cos(ang), -np.sin(ang)], axis=1) / np.sqrt(l)
    return jnp.asarray(cs.astype(np.float32)).astype(BF16)


def _head_ones():
    h = np.arange(R_WIDTH) // HEAD_DIM
    return jnp.asarray((h[:, None] == h[None, :]).astype(np.float32)).astype(BF16)


def _two_dir_lora(w):
    z = jnp.zeros_like(w[0])
    return _bf(jnp.concatenate([jnp.concatenate([w[0], z], axis=1),
                                jnp.concatenate([z, w[1]], axis=1)], axis=0))


def _token_mixer(x, mod, mod_row, grid_mode, need_out, s0, lw_, consts):
    rw_out = _inproj(x, mod, mod_row, lw_["n1"], lw_["wrw"], lw_["wz"], lw_["wg"], need_out)
    rw = rw_out[0]
    r, v, kk, kd, lw, bd, g, bonus = _prep(rw, grid_mode, lw_["mu"], lw_["w0"], lw_["a0"], lw_["wup"],
                                           lw_["aup"], lw_["gup"], lw_["kkw"], lw_["ka"], lw_["rk"],
                                           consts["eh"])
    reff, yloc, m, gs = _chunk_pass(r, v, kk, kd, lw, bd)
    y0, y1, s_fin = _state_pass(m, gs, reff, yloc, s0)
    if not need_out:
        return None, s_fin
    z, gates = rw_out[1], rw_out[2]
    f = _pos_dft(z, consts["pos_dft"][x.shape[1]])
    x1 = _merge(f, y0, y1, g, bonus, gates, x, mod, mod_row, lw_["lnw"], lw_["lnb"], consts["eh"],
                lw_["wf"], lw_["wr"], lw_["wo"])
    return x1, s_fin


def kernel(x, c, ctx, c_ctx, w_mod, b_mod, norm1, norm2, w_in, mu_shift, w0, w_up, a0, a_up, g_up,
           k_k, k_a, r_k, ln_x_w, ln_x_b, w_fourier_up, w_rwkv_up, w_out, mlp_w1, mlp_w2, norm_f):
    depth = w_mod.shape[0]
    batch, seq, d = x.shape
    ctx_len = ctx.shape[1]
    assert d == D_MODEL and batch + 1 <= MOD_ROWS
    assert seq % GRID_W == 0 and ctx_len % CHUNK == 0 and seq % CHUNK == 0

    consts = {
        "eh": _head_ones(),
        "pos_dft": {n: _position_dft(n) for n in {seq, ctx_len}},
    }
    cdft = _channel_dft()
    cc = jnp.zeros((MOD_ROWS, d), F32).at[:batch].set(c).at[batch].set(c_ctx)
    row1 = lambda a: a.reshape(1, -1)
    lat_row = lambda bi: bi
    ctx_row = lambda bi: batch

    x_lat, x_ctx = x, ctx
    s_zero = jnp.zeros((2, batch, HEAD_DIM, R_WIDTH), F32)
    for l in range(depth):
        last = l == depth - 1
        mod = _modulation(cc, w_mod[l], b_mod[l]).reshape(MOD_ROWS, N_MOD, d)
        lw_ = {
            "n1": row1(norm1[l]),
            "wz": _fold_channel_dft(w_in[l][:, :F_WIDTH], cdft),
            "wrw": _bf(w_in[l][:, F_WIDTH:F_WIDTH + RWKV_IN]),
            "wg": _bf(w_in[l][:, F_WIDTH + RWKV_IN:]),
            "mu": row1(mu_shift[l]),
            "w0": row1(w0[l]), "a0": row1(a0[l]),
            "wup": _two_dir_lora(w_up[l]), "aup": _two_dir_lora(a_up[l]), "gup": _bf(g_up[l]),
            "kkw": row1(k_k[l]), "ka": row1(k_a[l]), "rk": row1(r_k[l]),
            "lnw": row1(ln_x_w[l]), "lnb": row1(ln_x_b[l]),
            "wf": _bf(w_fourier_up[l]), "wr": _bf(w_rwkv_up[l]), "wo": _bf(w_out[l]),
        }
        n2 = row1(norm2[l])
        w1 = _bf(mlp_w1[l])
        w2 = _bf(mlp_w2[l])
        nf = row1(norm_f)

        ctx1, s_ctx = _token_mixer(x_ctx, mod, ctx_row, False, not last, s_zero, lw_, consts)
        lat1, _ = _token_mixer(x_lat, mod, lat_row, True, True, s_ctx, lw_, consts)
        x_lat = _mlp(lat1, mod, lat_row, n2, w1, w2, nf, final_norm=last)
        if not last:
            x_ctx = _mlp(ctx1, mod, ctx_row, n2, w1, w2, nf, final_norm=False)
    return x_lat
```

```python
import functools

import numpy as np
import jax
import jax.numpy as jnp
from jax import lax
from jax.experimental import pallas as pl
from jax.experimental.pallas import tpu as pltpu

F32 = jnp.float32
BF16 = jnp.bfloat16

D_MODEL = 1024
GRID_W = 64
F_WIDTH = 512
FGROUP_DIM = 128
HEAD_DIM = 64
N_RHEADS = 8
R_WIDTH = N_RHEADS * HEAD_DIM
D_LORA = 64
D_GATE_LORA = 128
RWKV_IN = 3 * R_WIDTH + 4 * D_LORA + D_GATE_LORA
GATE_W = 2 * D_MODEL
D_FF = 4 * D_MODEL
N_MOD = 6
NORM_EPS = 1e-6
GN_EPS = 64e-5
L2_EPS = 1e-12

CHUNK = 64
GROUP_LANES = 256
HEADS_PER_GROUP = GROUP_LANES // HEAD_DIM
HEAD_SHIFT = HEAD_DIM.bit_length() - 1
N_GROUPS = R_WIDTH // GROUP_LANES
MOD_ROWS = 16
VMEM_LIMIT = 56 * 1024 * 1024


def _bf(x):
    return x.astype(BF16)


def _dot(a, b):
    return jnp.dot(a, b, preferred_element_type=F32)


def _dot_nt(a, b):
    return lax.dot_general(a, b, (((1,), (1,)), ((), ())), preferred_element_type=F32)


def _dot_tn(a, b):
    return lax.dot_general(a, b, (((0,), (0,)), ((), ())), preferred_element_type=F32)


def _split2(x):
    hi = _bf(x)
    lo = _bf(x - hi.astype(F32))
    return hi, lo


def _split3(x):
    hi = _bf(x)
    r1 = x - hi.astype(F32)
    mid = _bf(r1)
    lo = _bf(r1 - mid.astype(F32))
    return hi, mid, lo


def _dot_exact_rhs(x, w):
    hi, lo = _split2(x)
    return _dot(hi, w) + _dot(lo, w)


def _params(*sem):
    return pltpu.CompilerParams(dimension_semantics=sem, vmem_limit_bytes=VMEM_LIMIT)


def _const_spec(shape):
    zeros = (0,) * len(shape)
    return pl.BlockSpec(shape, lambda *_: zeros)


def _linear_kernel(x_ref, w_ref, b_ref, o_ref, *, silu):
    x = x_ref[...]
    if silu:
        x = x * jax.nn.sigmoid(x)
    o_ref[...] = _dot(_bf(x), _bf(w_ref[...])) + b_ref[...]


def _modulation(cc, w, b):
    n = w.shape[1]
    tn = 512
    return pl.pallas_call(
        functools.partial(_linear_kernel, silu=True),
        grid=(n // tn,),
        in_specs=[_const_spec(cc.shape),
                  pl.BlockSpec((w.shape[0], tn), lambda j: (0, j)),
                  pl.BlockSpec((1, tn), lambda j: (0, j))],
        out_specs=pl.BlockSpec((cc.shape[0], tn), lambda j: (0, j)),
        out_shape=jax.ShapeDtypeStruct((cc.shape[0], n), F32),
        compiler_params=_params("parallel"),
        name="modulation",
    )(cc, w, b.reshape(1, n))


def _fold_kernel(w_ref, c_ref, o_ref):
    wh, wl = _split2(w_ref[...])
    ch, cl = _split2(c_ref[...])
    o_ref[...] = _bf(_dot(wh, ch) + _dot(wl, ch) + _dot(wh, cl))


def _fold_channel_dft(w_f, cdft):
    return pl.pallas_call(
        _fold_kernel,
        grid=(1,),
        in_specs=[_const_spec(w_f.shape), _const_spec(cdft.shape)],
        out_specs=_const_spec((w_f.shape[0], cdft.shape[1])),
        out_shape=jax.ShapeDtypeStruct((w_f.shape[0], cdft.shape[1]), BF16),
        compiler_params=_params("arbitrary"),
        name="fold_channel_dft",
    )(w_f, cdft)


def _rms(x):
    return x * lax.rsqrt(jnp.mean(x * x, axis=-1, keepdims=True) + NORM_EPS)


def _inproj_kernel(x_ref, mod_ref, n1_ref, wrw_ref, *rest, need_out):
    if need_out:
        wz_ref, wg_ref, rw_ref, z_ref, g_ref = rest
    else:
        (rw_ref,) = rest
    x = x_ref[0]
    shift = mod_ref[0, 0:1, :]
    scale = mod_ref[0, 1:2, :]
    h = _bf(_rms(x) * n1_ref[...] * (1.0 + scale) + shift)
    rw_ref[0] = _dot(h, wrw_ref[...])
    if need_out:
        z = _dot(h, wz_ref[...])
        z_ref[0, 0] = _bf(z[:, :F_WIDTH])
        z_ref[0, 1] = _bf(z[:, F_WIDTH:])
        g_ref[0] = _bf(_dot(h, wg_ref[...]))


def _inproj(x, mod, mod_row, n1, wrw, wz, wg, need_out):
    b, l, d = x.shape
    tm = min(256, l)
    row_spec = lambda w: pl.BlockSpec((1, tm, w), lambda bi, i: (bi, i, 0))
    in_specs = [row_spec(d),
                pl.BlockSpec((1, N_MOD, d), lambda bi, i: (mod_row(bi), 0, 0)),
                _const_spec((1, d)), _const_spec(wrw.shape)]
    args = [x, mod, n1, wrw]
    out_specs = [row_spec(RWKV_IN)]
    out_shape = [jax.ShapeDtypeStruct((b, l, RWKV_IN), F32)]
    if need_out:
        in_specs += [_const_spec(wz.shape), _const_spec(wg.shape)]
        args += [wz, wg]
        out_specs += [pl.BlockSpec((1, 2, tm, F_WIDTH), lambda bi, i: (bi, 0, i, 0)), row_spec(GATE_W)]
        out_shape += [jax.ShapeDtypeStruct((b, 2, l, F_WIDTH), BF16),
                      jax.ShapeDtypeStruct((b, l, GATE_W), BF16)]
    return pl.pallas_call(
        functools.partial(_inproj_kernel, need_out=need_out),
        grid=(b, l // tm),
        in_specs=in_specs, out_specs=out_specs, out_shape=out_shape,
        compiler_params=_params("parallel", "parallel"),
        name="inproj",
    )(*args)


def _log_sigmoid(x):
    return jnp.minimum(x, 0.0) - jnp.log1p(jnp.exp(-jnp.abs(x)))


def _prep_kernel(*refs, grid_mode, seq_len, tm):
    if grid_mode:
        (rw_ref, prev_ref, next_ref, mu_ref, w0_ref, a0_ref, wup_ref, aup_ref, gup_ref,
         kkw_ref, ka_ref, rk_ref, eh_ref,
         r_ref, v_ref, kk_ref, kd_ref, lw_ref, bd_ref, g_ref, bonus_ref) = refs
    else:
        (rw_ref, mu_ref, w0_ref, a0_ref, wup_ref, aup_ref, gup_ref,
         kkw_ref, ka_ref, rk_ref, eh_ref,
         r_ref, v_ref, kk_ref, kd_ref, lw_ref, bd_ref, g_ref, bonus_ref) = refs
    i = pl.program_id(1)
    x = rw_ref[0]
    t_loc = lax.broadcasted_iota(jnp.int32, (tm, 128), 0)
    t_glob = t_loc + i * tm
    lane = lax.broadcasted_iota(jnp.int32, (tm, 128), 1)
    if grid_mode:
        ext = jnp.concatenate([prev_ref[0], x, next_ref[0]], axis=0)
        n_ext = tm + 2 * GRID_W
        col = jnp.bitwise_and(t_loc, GRID_W - 1)
        masks = [col != 0, col != GRID_W - 1, t_glob >= GRID_W, t_glob < seq_len - GRID_W]
        n_parts = 4
    else:
        ext = x
        n_ext = tm
        masks = [t_glob != 0, t_glob != seq_len - 1]
        n_parts = 2
    part_w = RWKV_IN // n_parts

    def shifted(j, part):
        e = ext[:, 128 * j:128 * (j + 1)]
        if grid_mode:
            if part == 0:
                s = pltpu.roll(e, 1, 0)[GRID_W:GRID_W + tm]
            elif part == 1:
                s = pltpu.roll(e, n_ext - 1, 0)[GRID_W:GRID_W + tm]
            elif part == 2:
                s = e[0:tm]
            else:
                s = e[2 * GRID_W:2 * GRID_W + tm]
        else:
            s = pltpu.roll(e, 1, 0) if part == 0 else pltpu.roll(e, n_ext - 1, 0)
        return jnp.where(masks[part], s, 0.0)

    blocks = []
    for j in range(RWKV_IN // 128):
        p_lo = (128 * j) // part_w
        p_hi = (128 * j + 127) // part_w
        s = shifted(j, p_lo)
        if p_hi != p_lo:
            s = jnp.where(lane + 128 * j < part_w * p_hi, s, shifted(j, p_hi))
        xj = x[:, 128 * j:128 * (j + 1)]
        blocks.append(xj + mu_ref[:, 128 * j:128 * (j + 1)] * (s - xj))

    nb = R_WIDTH // 128
    r = jnp.concatenate(blocks[0:nb], axis=1)
    k = jnp.concatenate(blocks[nb:2 * nb], axis=1)
    v = jnp.concatenate(blocks[2 * nb:3 * nb], axis=1)
    wd, ad, gd = blocks[3 * nb], blocks[3 * nb + 1], blocks[3 * nb + 2]

    w_logit = w0_ref[...] + _dot(_bf(jnp.tanh(wd)), wup_ref[...])
    lw = -jnp.exp(_log_sigmoid(w_logit) - 0.5)
    a = jax.nn.sigmoid(a0_ref[...] + _dot(_bf(ad), aup_ref[...]))
    g = _dot(_bf(jax.nn.sigmoid(gd)), gup_ref[...])

    eh = eh_ref[...]
    kx = k * kkw_ref[...]
    nrm = jnp.maximum(jnp.sqrt(_dot_exact_rhs(kx * kx, eh)), L2_EPS)
    kk = kx / nrm
    ka = ka_ref[...]
    kd0 = k * (1.0 + (a[:, :R_WIDTH] - 1.0) * ka)
    kd1 = k * (1.0 + (a[:, R_WIDTH:] - 1.0) * ka)
    bonus = _dot_exact_rhs(r * (kd0 + kd1) * rk_ref[...], eh) * v

    r_ref[0] = r
    v_ref[0] = v
    kk_ref[0] = kk
    kd_ref[0, 0] = kd0
    kd_ref[1, 0] = kd1
    lw_ref[0, 0] = lw[:, :R_WIDTH]
    lw_ref[1, 0] = lw[:, R_WIDTH:]
    bd_ref[0, 0] = kk * a[:, :R_WIDTH]
    bd_ref[1, 0] = kk * a[:, R_WIDTH:]
    g_ref[0] = g
    bonus_ref[0] = bonus


def _prep(rw, grid_mode, mu, w0c, a0c, wup, aup, gup, kkw, ka, rk, eh):
    b, l, _ = rw.shape
    tm = min(256, l) if grid_mode else l
    nrow = l // GRID_W
    per = tm // GRID_W
    row_spec = pl.BlockSpec((1, tm, RWKV_IN), lambda bi, i: (bi, i, 0))
    in_specs = [row_spec]
    args = [rw]
    if grid_mode:
        in_specs += [
            pl.BlockSpec((1, GRID_W, RWKV_IN), lambda bi, i: (bi, jnp.maximum(i * per - 1, 0), 0)),
            pl.BlockSpec((1, GRID_W, RWKV_IN), lambda bi, i: (bi, jnp.minimum((i + 1) * per, nrow - 1), 0)),
        ]
        args += [rw, rw]
    consts = [mu, w0c, a0c, wup, aup, gup, kkw, ka, rk, eh]
    in_specs += [_const_spec(c.shape) for c in consts]
    args += consts
    one = pl.BlockSpec((1, tm, R_WIDTH), lambda bi, i: (bi, i, 0))
    two = pl.BlockSpec((2, 1, tm, R_WIDTH), lambda bi, i: (0, bi, i, 0))
    s1 = jax.ShapeDtypeStruct((b, l, R_WIDTH), F32)
    s2 = jax.ShapeDtypeStruct((2, b, l, R_WIDTH), F32)
    return pl.pallas_call(
        functools.partial(_prep_kernel, grid_mode=grid_mode, seq_len=l, tm=tm),
        grid=(b, l // tm),
        in_specs=in_specs,
        out_specs=[one, one, one, two, two, two, one, one],
        out_shape=[s1, s1, s1, s2, s2, s2, s1, s1],
        compiler_params=_params("parallel", "parallel"),
        name="rwkv_prep",
    )(*args)


def _block_diag_mask():
    rr = jnp.right_shift(lax.broadcasted_iota(jnp.int32, (GROUP_LANES, GROUP_LANES), 0), HEAD_SHIFT)
    cc = jnp.right_shift(lax.broadcasted_iota(jnp.int32, (GROUP_LANES, GROUP_LANES), 1), HEAD_SHIFT)
    return rr == cc


def _block_diag(x, mask):
    tiled = jnp.concatenate([x] * HEADS_PER_GROUP, axis=0)
    return jnp.where(mask, tiled, jnp.zeros_like(tiled))


def _diag_blocks(q, lane_head):
    out = jnp.where(lane_head == 0, q[0:HEAD_DIM], 0.0)
    for h in range(1, HEADS_PER_GROUP):
        out = out + jnp.where(lane_head == h, q[h * HEAD_DIM:(h + 1) * HEAD_DIM], 0.0)
    return out


def _chunk_kernel(r_ref, v_ref, kk_ref, kd_ref, lw_ref, bd_ref,
                  reff_ref, yloc_ref, m_ref, g_ref, *, tc):
    sign = 1 - 2 * pl.program_id(0)
    t = lax.broadcasted_iota(jnp.int32, (CHUNK, GROUP_LANES), 0)
    lane = lax.broadcasted_iota(jnp.int32, (CHUNK, GROUP_LANES), 1)
    s = jnp.bitwise_and(lane, CHUNK - 1)
    lane_head = jnp.right_shift(lane, HEAD_SHIFT)
    before = (s - t) * sign < 0
    upto = (s - t) * sign <= 0
    eye = jnp.where(s == t, 1.0, 0.0)
    bdmask = _block_diag_mask()
    t64 = lax.broadcasted_iota(jnp.int32, (CHUNK, CHUNK), 0)
    s64 = lax.broadcasted_iota(jnp.int32, (CHUNK, CHUNK), 1)
    tri = jnp.where((s64 - t64) * sign <= 0, 1.0, 0.0).astype(BF16)

    bdiag = lambda x: _block_diag(_bf(x), bdmask)
    stack = lambda a, b: _bf(jnp.concatenate([a, b], axis=0))
    units = [(slice(ci * CHUNK, (ci + 1) * CHUNK), slice(gi * GROUP_LANES, (gi + 1) * GROUP_LANES))
             for ci in range(tc // CHUNK) for gi in range(N_GROUPS)]
    nu = range(len(units))

    cum_chunk = []
    for ci in range(tc // CHUNK):
        h3 = _split3(lw_ref[0, 0, ci * CHUNK:(ci + 1) * CHUNK, :])
        cum_chunk.append(_dot(tri, h3[0]) + _dot(tri, h3[1]) + _dot(tri, h3[2]))
    kt, rt, kh, bh, kb, bb, vv, e_tot = [], [], [], [], [], [], [], []
    for u, (rows, lanes) in enumerate(units):
        lw = lw_ref[0, 0, rows, lanes]
        cum = cum_chunk[u // N_GROUPS][:, lanes]
        e_neg = jnp.exp(-cum)
        et = jnp.exp(jnp.sum(lw, axis=0, keepdims=True))
        kt.append(kk_ref[0, rows, lanes] * jnp.exp(cum - lw))
        rt.append(r_ref[0, rows, lanes] * jnp.exp(cum))
        kh.append(kd_ref[0, 0, rows, lanes] * e_neg)
        bh.append(bd_ref[0, 0, rows, lanes] * e_neg)
        kb.append(kh[u] * et)
        bb.append(bh[u] * et)
        vv.append(v_ref[0, rows, lanes])
        e_tot.append(et)

    lhs = [stack(kt[u], rt[u]) for u in nu]
    sk = [_dot_nt(lhs[u], bdiag(kh[u])) for u in nu]
    sb = [_dot_nt(lhs[u], bdiag(bh[u])) for u in nu]
    a_k = [jnp.where(before, sk[u][:CHUNK], 0.0) for u in nu]
    a_rk = [jnp.where(upto, sk[u][CHUNK:], 0.0) for u in nu]
    n = [jnp.where(before, -sb[u][:CHUNK], 0.0) for u in nu]
    a_rb = [_bf(jnp.where(upto, sb[u][CHUNK:], 0.0)) for u in nu]

    tinv = [eye + n[u] for u in nu]
    p = [_dot(_bf(n[u]), bdiag(n[u])) for u in nu]
    av = [_dot(stack(a_k[u], a_rk[u]), bdiag(vv[u])) for u in nu]
    for _ in range(4):
        tp = [_dot(stack(tinv[u], p[u]), bdiag(p[u])) for u in nu]
        tinv = [tinv[u] + tp[u][:CHUNK] for u in nu]
        p = [tp[u][CHUNK:] for u in nu]
    tb = [_bf(tinv[u] + _dot(_bf(tinv[u]), bdiag(p[u]))) for u in nu]

    uk = [_dot(tb[u], bdiag(kt[u])) for u in nu]
    uv = [_dot(tb[u], bdiag(av[u][:CHUNK])) for u in nu]
    reff = [rt[u] - _dot(a_rb[u], bdiag(uk[u])) for u in nu]
    yloc = [av[u][CHUNK:] - _dot(a_rb[u], bdiag(uv[u])) for u in nu]
    q = [_dot_tn(_bf(bb[u]), _bf(uk[u])) for u in nu]
    q2 = [_dot_tn(stack(kb[u], -bb[u]), stack(vv[u], uv[u])) for u in nu]
    for u, (rows, lanes) in enumerate(units):
        reff_ref[0, 0, rows, lanes] = reff[u]
        yloc_ref[0, 0, rows, lanes] = yloc[u]
        m_ref[0, 0, rows, lanes] = eye * e_tot[u] - _diag_blocks(q[u], lane_head)
        g_ref[0, 0, rows, lanes] = _diag_blocks(q2[u], lane_head)


def _chunk_pass(r, v, kk, kd, lw, bd):
    b, l, _ = r.shape
    tc = min(256, l)
    one = pl.BlockSpec((1, tc, R_WIDTH), lambda d, bi, i: (bi, i, 0))
    two = pl.BlockSpec((1, 1, tc, R_WIDTH), lambda d, bi, i: (d, bi, i, 0))
    s2 = jax.ShapeDtypeStruct((2, b, l, R_WIDTH), F32)
    return pl.pallas_call(
        functools.partial(_chunk_kernel, tc=tc),
        grid=(2, b, l // tc),
        in_specs=[one, one, one, two, two, two],
        out_specs=[two, two, two, two],
        out_shape=[s2, s2, s2, s2],
        compiler_params=_params("parallel", "parallel", "parallel"),
        name="rwkv_chunk",
    )(r, v, kk, kd, lw, bd)


def _state_kernel(m0_ref, g0_ref, re0_ref, yl0_ref, m1_ref, g1_ref, re1_ref, yl1_ref, s0_ref,
                  y0_ref, y1_ref, h_ref, *, batch):
    c = pl.program_id(0)

    @pl.when(c == 0)
    def _():
        h_ref[...] = s0_ref[...]

    bdmask = _block_diag_mask()
    per_dir = ((m0_ref, g0_ref, re0_ref, yl0_ref, y0_ref), (m1_ref, g1_ref, re1_ref, yl1_ref, y1_ref))
    units = [(d, bi, slice(gi * GROUP_LANES, (gi + 1) * GROUP_LANES))
             for d in range(2) for bi in range(batch) for gi in range(N_GROUPS)]
    hs = [_split2(h_ref[d, bi, :, lanes]) for d, bi, lanes in units]
    ms = [_split2(per_dir[d][0][0, bi, :, lanes]) for d, bi, lanes in units]
    rs = [_split2(per_dir[d][2][0, bi, :, lanes]) for d, bi, lanes in units]
    o1 = [_dot(jnp.concatenate([ms[u][0], ms[u][1], rs[u][0], rs[u][1]], axis=0), _block_diag(hs[u][0], bdmask))
          for u in range(len(units))]
    o2 = [_dot(jnp.concatenate([ms[u][0], rs[u][0]], axis=0), _block_diag(hs[u][1], bdmask))
          for u in range(len(units))]
    for u, (d, bi, lanes) in enumerate(units):
        _, g_ref, _, yl_ref, y_ref = per_dir[d]
        mh_new = o1[u][0:CHUNK] + o1[u][CHUNK:2 * CHUNK] + o2[u][0:CHUNK]
        rh_new = o1[u][2 * CHUNK:3 * CHUNK] + o1[u][3 * CHUNK:] + o2[u][CHUNK:]
        y_ref[bi, :, lanes] = yl_ref[0, bi, :, lanes] + rh_new
        h_ref[d, bi, :, lanes] = mh_new + g_ref[0, bi, :, lanes]


def _state_pass(m, g, reff, yloc, s0):
    _, b, l, _ = m.shape
    nc = l // CHUNK
    blk = (1, b, CHUNK, R_WIDTH)
    fwd = pl.BlockSpec(blk, lambda c: (0, 0, c, 0))
    bwd = pl.BlockSpec(blk, lambda c: (1, 0, nc - 1 - c, 0))
    st = pl.BlockSpec((2, b, HEAD_DIM, R_WIDTH), lambda c: (0, 0, 0, 0))
    sy = jax.ShapeDtypeStruct((b, l, R_WIDTH), F32)
    return pl.pallas_call(
        functools.partial(_state_kernel, batch=b),
        grid=(nc,),
        in_specs=[fwd, fwd, fwd, fwd, bwd, bwd, bwd, bwd, st],
        out_specs=[pl.BlockSpec((b, CHUNK, R_WIDTH), lambda c: (0, c, 0)),
                   pl.BlockSpec((b, CHUNK, R_WIDTH), lambda c: (0, nc - 1 - c, 0)),
                   st],
        out_shape=[sy, sy, jax.ShapeDtypeStruct((2, b, HEAD_DIM, R_WIDTH), F32)],
        compiler_params=_params("arbitrary"),
        name="rwkv_state",
    )(m, g, reff, yloc, m, g, reff, yloc, s0)


def _posdft_kernel(cs_ref, z_ref, o_ref):
    l = z_ref.shape[2]
    z = z_ref[0].reshape(2 * l, F_WIDTH)
    o_ref[0] = _bf(_dot(cs_ref[...], z))


def _pos_dft(z, cs):
    b, _, l, _ = z.shape
    tm = min(256, l)
    return pl.pallas_call(
        _posdft_kernel,
        grid=(b, l // tm),
        in_specs=[pl.BlockSpec((tm, 2 * l), lambda bi, i: (i, 0)),
                  pl.BlockSpec((1, 2, l, F_WIDTH), lambda bi, i: (bi, 0, 0, 0))],
        out_specs=pl.BlockSpec((1, tm, F_WIDTH), lambda bi, i: (bi, i, 0)),
        out_shape=jax.ShapeDtypeStruct((b, l, F_WIDTH), BF16),
        compiler_params=_params("parallel", "parallel"),
        name="pos_dft",
    )(cs, z)


def _merge_kernel(f_ref, y0_ref, y1_ref, g_ref, bonus_ref, gates_ref, x_ref, mod_ref,
                  lnw_ref, lnb_ref, eh_ref, wf_ref, wr_ref, wo_ref, o_ref):
    y = y0_ref[0] + y1_ref[0]
    eh = eh_ref[...]
    inv_n = 1.0 / HEAD_DIM
    mu = _dot_exact_rhs(y, eh) * inv_n
    dlt = y - mu
    var = _dot_exact_rhs(dlt * dlt, eh) * inv_n
    yn = dlt * lax.rsqrt(var + GN_EPS) * lnw_ref[...] + lnb_ref[...]
    rwkv = _bf((yn + bonus_ref[0]) * g_ref[0])
    fo = _dot(f_ref[0], wf_ref[...])
    ro = _dot(rwkv, wr_ref[...])
    gates = gates_ref[0].astype(F32)
    merged = jax.nn.sigmoid(gates[:, :D_MODEL]) * fo + jax.nn.sigmoid(gates[:, D_MODEL:]) * ro
    out = _dot(_bf(merged), wo_ref[...])
    o_ref[0] = x_ref[0] + mod_ref[0, 2:3, :] * out


def _merge(f, y0, y1, g, bonus, gates, x, mod, mod_row, lnw, lnb, eh, wf, wr, wo):
    b, l, d = x.shape
    tm = min(256, l)
    row = lambda w: pl.BlockSpec((1, tm, w), lambda bi, i: (bi, i, 0))
    consts = [lnw, lnb, eh, wf, wr, wo]
    return pl.pallas_call(
        _merge_kernel,
        grid=(b, l // tm),
        in_specs=[row(F_WIDTH), row(R_WIDTH), row(R_WIDTH), row(R_WIDTH), row(R_WIDTH), row(GATE_W), row(d),
                  pl.BlockSpec((1, N_MOD, d), lambda bi, i: (mod_row(bi), 0, 0))]
                 + [_const_spec(c.shape) for c in consts],
        out_specs=row(d),
        out_shape=jax.ShapeDtypeStruct((b, l, d), F32),
        compiler_params=_params("parallel", "parallel"),
        name="merge",
    )(f, y0, y1, g, bonus, gates, x, mod, *consts)


def _mlp_kernel(x_ref, mod_ref, n2_ref, w1_ref, w2_ref, nf_ref, o_ref, *, final_norm):
    x = x_ref[0]
    h = _bf(_rms(x) * n2_ref[...] * (1.0 + mod_ref[0, 4:5, :]) + mod_ref[0, 3:4, :])
    acc = jnp.zeros(x.shape, F32)
    step = 1024
    for j in range(D_FF // step):
        u = jnp.maximum(_dot(h, w1_ref[:, j * step:(j + 1) * step]), 0.0)
        acc = acc + _dot(_bf(u * u), w2_ref[j * step:(j + 1) * step, :])
    x2 = x + mod_ref[0, 5:6, :] * acc
    if final_norm:
        x2 = _rms(x2) * nf_ref[...]
    o_ref[0] = x2


def _mlp(x, mod, mod_row, n2, w1, w2, nf, final_norm):
    b, l, d = x.shape
    tm = min(512, l)
    row = pl.BlockSpec((1, tm, d), lambda bi, i: (bi, i, 0))
    once = lambda shape: pl.BlockSpec(shape, lambda bi, i: (0, 0), pipeline_mode=pl.Buffered(1))
    return pl.pallas_call(
        functools.partial(_mlp_kernel, final_norm=final_norm),
        grid=(b, l // tm),
        in_specs=[row, pl.BlockSpec((1, N_MOD, d), lambda bi, i: (mod_row(bi), 0, 0)),
                  _const_spec((1, d)), once(w1.shape), once(w2.shape), _const_spec((1, d))],
        out_specs=row,
        out_shape=jax.ShapeDtypeStruct((b, l, d), F32),
        compiler_params=_params("parallel", "parallel"),
        name="mlp",
    )(x, mod, n2, w1, w2, nf)


def _channel_dft():
    n = FGROUP_DIM
    jk = np.outer(np.arange(n), np.arange(n)) % n
    ang = 2.0 * np.pi * jk / n
    c = np.cos(ang) / np.sqrt(n)
    s = np.sin(ang) / np.sqrt(n)
    g = F_WIDTH // n
    out = np.zeros((F_WIDTH, 2 * F_WIDTH), np.float32)
    for i in range(g):
        out[i * n:(i + 1) * n, i * n:(i + 1) * n] = c
        out[i * n:(i + 1) * n, F_WIDTH + i * n:F_WIDTH + (i + 1) * n] = s
    return jnp.asarray(out)


def _position_dft(l):
    jk = np.outer(np.arange(l), np.arange(l)) % l
    ang = 2.0 * np.pi * jk / l
    cs = np.concatenate([np.cos(ang), -np.sin(ang)], axis=1) / np.sqrt(l)
    return jnp.asarray(cs.astype(np.float32)).astype(BF16)


def _head_ones():
    h = np.arange(R_WIDTH) // HEAD_DIM
    return jnp.asarray((h[:, None] == h[None, :]).astype(np.float32)).astype(BF16)


def _two_dir_lora(w):
    z = jnp.zeros_like(w[0])
    return _bf(jnp.concatenate([jnp.concatenate([w[0], z], axis=1),
                                jnp.concatenate([z, w[1]], axis=1)], axis=0))


def _token_mixer(x, mod, mod_row, grid_mode, need_out, s0, lw_, consts):
    rw_out = _inproj(x, mod, mod_row, lw_["n1"], lw_["wrw"], lw_["wz"], lw_["wg"], need_out)
    rw = rw_out[0]
    r, v, kk, kd, lw, bd, g, bonus = _prep(rw, grid_mode, lw_["mu"], lw_["w0"], lw_["a0"], lw_["wup"],
                                           lw_["aup"], lw_["gup"], lw_["kkw"], lw_["ka"], lw_["rk"],
                                           consts["eh"])
    reff, yloc, m, gs = _chunk_pass(r, v, kk, kd, lw, bd)
    y0, y1, s_fin = _state_pass(m, gs, reff, yloc, s0)
    if not need_out:
        return None, s_fin
    z, gates = rw_out[1], rw_out[2]
    f = _pos_dft(z, consts["pos_dft"][x.shape[1]])
    x1 = _merge(f, y0, y1, g, bonus, gates, x, mod, mod_row, lw_["lnw"], lw_["lnb"], consts["eh"],
                lw_["wf"], lw_["wr"], lw_["wo"])
    return x1, s_fin


def kernel(x, c, ctx, c_ctx, w_mod, b_mod, norm1, norm2, w_in, mu_shift, w0, w_up, a0, a_up, g_up,
           k_k, k_a, r_k, ln_x_w, ln_x_b, w_fourier_up, w_rwkv_up, w_out, mlp_w1, mlp_w2, norm_f):
    depth = w_mod.shape[0]
    batch, seq, d = x.shape
    ctx_len = ctx.shape[1]
    assert d == D_MODEL and batch + 1 <= MOD_ROWS
    assert seq % GRID_W == 0 and ctx_len % CHUNK == 0 and seq % CHUNK == 0

    consts = {
        "eh": _head_ones(),
        "pos_dft": {n: _position_dft(n) for n in {seq, ctx_len}},
    }
    cdft = _channel_dft()
    cc = jnp.zeros((MOD_ROWS, d), F32).at[:batch].set(c).at[batch].set(c_ctx)
    row1 = lambda a: a.reshape(1, -1)
    lat_row = lambda bi: bi
    ctx_row = lambda bi: batch

    x_lat, x_ctx = x, ctx
    s_zero = jnp.zeros((2, batch, HEAD_DIM, R_WIDTH), F32)
    for l in range(depth):
        last = l == depth - 1
        mod = _modulation(cc, w_mod[l], b_mod[l]).reshape(MOD_ROWS, N_MOD, d)
        lw_ = {
            "n1": row1(norm1[l]),
            "wz": _fold_channel_dft(w_in[l][:, :F_WIDTH], cdft),
            "wrw": _bf(w_in[l][:, F_WIDTH:F_WIDTH + RWKV_IN]),
            "wg": _bf(w_in[l][:, F_WIDTH + RWKV_IN:]),
            "mu": row1(mu_shift[l]),
            "w0": row1(w0[l]), "a0": row1(a0[l]),
            "wup": _two_dir_lora(w_up[l]), "aup": _two_dir_lora(a_up[l]), "gup": _bf(g_up[l]),
            "kkw": row1(k_k[l]), "ka": row1(k_a[l]), "rk": row1(r_k[l]),
            "lnw": row1(ln_x_w[l]), "lnb": row1(ln_x_b[l]),
            "wf": _bf(w_fourier_up[l]), "wr": _bf(w_rwkv_up[l]), "wo": _bf(w_out[l]),
        }
        n2 = row1(norm2[l])
        w1 = _bf(mlp_w1[l])
        w2 = _bf(mlp_w2[l])
        nf = row1(norm_f)

        ctx1, s_ctx = _token_mixer(x_ctx, mod, ctx_row, False, not last, s_zero, lw_, consts)
        lat1, _ = _token_mixer(x_lat, mod, lat_row, True, True, s_ctx, lw_, consts)
        x_lat = _mlp(lat1, mod, lat_row, n2, w1, w2, nf, final_norm=last)
        if not last:
            x_ctx = _mlp(ctx1, mod, ctx_row, n2, w1, w2, nf, final_norm=False)
    return x_lat
```

```python
import functools

import numpy as np
import jax
import jax.numpy as jnp
from jax import lax
from jax.experimental import pallas as pl
from jax.experimental.pallas import tpu as pltpu

F32 = jnp.float32
BF16 = jnp.bfloat16

D_MODEL = 1024
GRID_W = 64
F_WIDTH = 512
FGROUP_DIM = 128
HEAD_DIM = 64
N_RHEADS = 8
R_WIDTH = N_RHEADS * HEAD_DIM
D_LORA = 64
D_GATE_LORA = 128
RWKV_IN = 3 * R_WIDTH + 4 * D_LORA + D_GATE_LORA
GATE_W = 2 * D_MODEL
D_FF = 4 * D_MODEL
N_MOD = 6
NORM_EPS = 1e-6
GN_EPS = 64e-5
L2_EPS = 1e-12

CHUNK = 64
GROUP_LANES = 256
HEADS_PER_GROUP = GROUP_LANES // HEAD_DIM
HEAD_SHIFT = HEAD_DIM.bit_length() - 1
N_GROUPS = R_WIDTH // GROUP_LANES
MOD_ROWS = 16
VMEM_LIMIT = 56 * 1024 * 1024


def _bf(x):
    return x.astype(BF16)


def _dot(a, b):
    return jnp.dot(a, b, preferred_element_type=F32)


def _dot_nt(a, b):
    return lax.dot_general(a, b, (((1,), (1,)), ((), ())), preferred_element_type=F32)


def _dot_tn(a, b):
    return lax.dot_general(a, b, (((0,), (0,)), ((), ())), preferred_element_type=F32)


def _split2(x):
    hi = _bf(x)
    lo = _bf(x - hi.astype(F32))
    return hi, lo


def _split3(x):
    hi = _bf(x)
    r1 = x - hi.astype(F32)
    mid = _bf(r1)
    lo = _bf(r1 - mid.astype(F32))
    return hi, mid, lo


def _dot_exact_rhs(x, w):
    hi, lo = _split2(x)
    return _dot(hi, w) + _dot(lo, w)


def _params(*sem):
    return pltpu.CompilerParams(dimension_semantics=sem, vmem_limit_bytes=VMEM_LIMIT)


def _const_spec(shape):
    zeros = (0,) * len(shape)
    return pl.BlockSpec(shape, lambda *_: zeros)


def _linear_kernel(x_ref, w_ref, b_ref, o_ref, *, silu):
    x = x_ref[...]
    if silu:
        x = x * jax.nn.sigmoid(x)
    o_ref[...] = _dot(_bf(x), _bf(w_ref[...])) + b_ref[...]


def _modulation(cc, w, b):
    n = w.shape[1]
    tn = 512
    return pl.pallas_call(
        functools.partial(_linear_kernel, silu=True),
        grid=(n // tn,),
        in_specs=[_const_spec(cc.shape),
                  pl.BlockSpec((w.shape[0], tn), lambda j: (0, j)),
                  pl.BlockSpec((1, tn), lambda j: (0, j))],
        out_specs=pl.BlockSpec((cc.shape[0], tn), lambda j: (0, j)),
        out_shape=jax.ShapeDtypeStruct((cc.shape[0], n), F32),
        compiler_params=_params("parallel"),
        name="modulation",
    )(cc, w, b.reshape(1, n))


def _fold_kernel(w_ref, c_ref, o_ref):
    wh, wl = _split2(w_ref[...])
    ch, cl = _split2(c_ref[...])
    o_ref[...] = _bf(_dot(wh, ch) + _dot(wl, ch) + _dot(wh, cl))


def _fold_channel_dft(w_f, cdft):
    return pl.pallas_call(
        _fold_kernel,
        grid=(1,),
        in_specs=[_const_spec(w_f.shape), _const_spec(cdft.shape)],
        out_specs=_const_spec((w_f.shape[0], cdft.shape[1])),
        out_shape=jax.ShapeDtypeStruct((w_f.shape[0], cdft.shape[1]), BF16),
        compiler_params=_params("arbitrary"),
        name="fold_channel_dft",
    )(w_f, cdft)


def _rms(x):
    return x * lax.rsqrt(jnp.mean(x * x, axis=-1, keepdims=True) + NORM_EPS)


def _inproj_kernel(x_ref, mod_ref, n1_ref, wrw_ref, *rest, need_out):
    if need_out:
        wz_ref, wg_ref, rw_ref, z_ref, g_ref = rest
    else:
        (rw_ref,) = rest
    x = x_ref[0]
    shift = mod_ref[0, 0:1, :]
    scale = mod_ref[0, 1:2, :]
    h = _bf(_rms(x) * n1_ref[...] * (1.0 + scale) + shift)
    rw_ref[0] = _dot(h, wrw_ref[...])
    if need_out:
        z = _dot(h, wz_ref[...])
        z_ref[0, 0] = _bf(z[:, :F_WIDTH])
        z_ref[0, 1] = _bf(z[:, F_WIDTH:])
        g_ref[0] = _bf(_dot(h, wg_ref[...]))


def _inproj(x, mod, mod_row, n1, wrw, wz, wg, need_out):
    b, l, d = x.shape
    tm = min(256, l)
    row_spec = lambda w: pl.BlockSpec((1, tm, w), lambda bi, i: (bi, i, 0))
    in_specs = [row_spec(d),
                pl.BlockSpec((1, N_MOD, d), lambda bi, i: (mod_row(bi), 0, 0)),
                _const_spec((1, d)), _const_spec(wrw.shape)]
    args = [x, mod, n1, wrw]
    out_specs = [row_spec(RWKV_IN)]
    out_shape = [jax.ShapeDtypeStruct((b, l, RWKV_IN), F32)]
    if need_out:
        in_specs += [_const_spec(wz.shape), _const_spec(wg.shape)]
        args += [wz, wg]
        out_specs += [pl.BlockSpec((1, 2, tm, F_WIDTH), lambda bi, i: (bi, 0, i, 0)), row_spec(GATE_W)]
        out_shape += [jax.ShapeDtypeStruct((b, 2, l, F_WIDTH), BF16),
                      jax.ShapeDtypeStruct((b, l, GATE_W), BF16)]
    return pl.pallas_call(
        functools.partial(_inproj_kernel, need_out=need_out),
        grid=(b, l // tm),
        in_specs=in_specs, out_specs=out_specs, out_shape=out_shape,
        compiler_params=_params("parallel", "parallel"),
        name="inproj",
    )(*args)


def _log_sigmoid(x):
    return jnp.minimum(x, 0.0) - jnp.log1p(jnp.exp(-jnp.abs(x)))


P_R, P_V, P_KK, P_KD, P_LW, P_B = 0, 1, 2, 3, 5, 7
N_PREP = 9


def _scan_kernel(*refs, grid_mode, seq_len, tm):
    reff_ref, yloc_ref, m_ref, gs_ref, g_ref, bonus_ref, pp_ref = refs[-7:]
    _prep_tile(refs[:-7], pp_ref, g_ref, bonus_ref, grid_mode=grid_mode, seq_len=seq_len, tm=tm)
    _chunk_algebra(pp_ref, reff_ref, yloc_ref, m_ref, gs_ref, tc=tm)


def _prep_tile(in_refs, pp_ref, g_ref, bonus_ref, *, grid_mode, seq_len, tm):
    if grid_mode:
        (rw_ref, prev_ref, next_ref, mu_ref, w0_ref, a0_ref, wup_ref, aup_ref, gup_ref,
         kkw_ref, ka_ref, rk_ref, eh_ref) = in_refs
    else:
        (rw_ref, mu_ref, w0_ref, a0_ref, wup_ref, aup_ref, gup_ref,
         kkw_ref, ka_ref, rk_ref, eh_ref) = in_refs
    i = pl.program_id(1)
    x = rw_ref[0]
    t_loc = lax.broadcasted_iota(jnp.int32, (tm, 128), 0)
    t_glob = t_loc + i * tm
    lane = lax.broadcasted_iota(jnp.int32, (tm, 128), 1)
    if grid_mode:
        ext = jnp.concatenate([prev_ref[0], x, next_ref[0]], axis=0)
        n_ext = tm + 2 * GRID_W
        col = jnp.bitwise_and(t_loc, GRID_W - 1)
        masks = [col != 0, col != GRID_W - 1, t_glob >= GRID_W, t_glob < seq_len - GRID_W]
        n_parts = 4
    else:
        ext = x
        n_ext = tm
        masks = [t_glob != 0, t_glob != seq_len - 1]
        n_parts = 2
    part_w = RWKV_IN // n_parts

    def shifted(j, part):
        e = ext[:, 128 * j:128 * (j + 1)]
        if grid_mode:
            if part == 0:
                s = pltpu.roll(e, 1, 0)[GRID_W:GRID_W + tm]
            elif part == 1:
                s = pltpu.roll(e, n_ext - 1, 0)[GRID_W:GRID_W + tm]
            elif part == 2:
                s = e[0:tm]
            else:
                s = e[2 * GRID_W:2 * GRID_W + tm]
        else:
            s = pltpu.roll(e, 1, 0) if part == 0 else pltpu.roll(e, n_ext - 1, 0)
        return jnp.where(masks[part], s, 0.0)

    blocks = []
    for j in range(RWKV_IN // 128):
        p_lo = (128 * j) // part_w
        p_hi = (128 * j + 127) // part_w
        s = shifted(j, p_lo)
        if p_hi != p_lo:
            s = jnp.where(lane + 128 * j < part_w * p_hi, s, shifted(j, p_hi))
        xj = x[:, 128 * j:128 * (j + 1)]
        blocks.append(xj + mu_ref[:, 128 * j:128 * (j + 1)] * (s - xj))

    nb = R_WIDTH // 128
    r = jnp.concatenate(blocks[0:nb], axis=1)
    k = jnp.concatenate(blocks[nb:2 * nb], axis=1)
    v = jnp.concatenate(blocks[2 * nb:3 * nb], axis=1)
    wd, ad, gd = blocks[3 * nb], blocks[3 * nb + 1], blocks[3 * nb + 2]

    w_logit = w0_ref[...] + _dot(_bf(jnp.tanh(wd)), wup_ref[...])
    lw = -jnp.exp(_log_sigmoid(w_logit) - 0.5)
    a = jax.nn.sigmoid(a0_ref[...] + _dot(_bf(ad), aup_ref[...]))
    g = _dot(_bf(jax.nn.sigmoid(gd)), gup_ref[...])

    eh = eh_ref[...]
    kx = k * kkw_ref[...]
    nrm = jnp.maximum(jnp.sqrt(_dot_exact_rhs(kx * kx, eh)), L2_EPS)
    kk = kx / nrm
    ka = ka_ref[...]
    kd0 = k * (1.0 + (a[:, :R_WIDTH] - 1.0) * ka)
    kd1 = k * (1.0 + (a[:, R_WIDTH:] - 1.0) * ka)
    bonus = _dot_exact_rhs(r * (kd0 + kd1) * rk_ref[...], eh) * v

    pp_ref[P_R] = r
    pp_ref[P_V] = v
    pp_ref[P_KK] = kk
    pp_ref[P_KD] = kd0
    pp_ref[P_KD + 1] = kd1
    pp_ref[P_LW] = lw[:, :R_WIDTH]
    pp_ref[P_LW + 1] = lw[:, R_WIDTH:]
    pp_ref[P_B] = kk * a[:, :R_WIDTH]
    pp_ref[P_B + 1] = kk * a[:, R_WIDTH:]
    g_ref[0] = g
    bonus_ref[0] = bonus


def _scan_pass(rw, grid_mode, mu, w0c, a0c, wup, aup, gup, kkw, ka, rk, eh):
    b, l, _ = rw.shape
    tm = min(256, l) if grid_mode else l
    nrow = l // GRID_W
    per = tm // GRID_W
    row_spec = pl.BlockSpec((1, tm, RWKV_IN), lambda bi, i: (bi, i, 0))
    in_specs = [row_spec]
    args = [rw]
    if grid_mode:
        in_specs += [
            pl.BlockSpec((1, GRID_W, RWKV_IN), lambda bi, i: (bi, jnp.maximum(i * per - 1, 0), 0)),
            pl.BlockSpec((1, GRID_W, RWKV_IN), lambda bi, i: (bi, jnp.minimum((i + 1) * per, nrow - 1), 0)),
        ]
        args += [rw, rw]
    consts = [mu, w0c, a0c, wup, aup, gup, kkw, ka, rk, eh]
    in_specs += [_const_spec(c.shape) for c in consts]
    args += consts
    one = pl.BlockSpec((1, tm, R_WIDTH), lambda bi, i: (bi, i, 0))
    two = pl.BlockSpec((2, 1, tm, R_WIDTH), lambda bi, i: (0, bi, i, 0))
    s1 = jax.ShapeDtypeStruct((b, l, R_WIDTH), F32)
    s2 = jax.ShapeDtypeStruct((2, b, l, R_WIDTH), F32)
    return pl.pallas_call(
        functools.partial(_scan_kernel, grid_mode=grid_mode, seq_len=l, tm=tm),
        grid=(b, l // tm),
        in_specs=in_specs,
        out_specs=[two, two, two, two, one, one],
        out_shape=[s2, s2, s2, s2, s1, s1],
        scratch_shapes=[pltpu.VMEM((N_PREP, tm, R_WIDTH), F32)],
        compiler_params=_params("parallel", "parallel"),
        name="rwkv_scan",
    )(*args)


def _block_diag_mask():
    rr = jnp.right_shift(lax.broadcasted_iota(jnp.int32, (GROUP_LANES, GROUP_LANES), 0), HEAD_SHIFT)
    cc = jnp.right_shift(lax.broadcasted_iota(jnp.int32, (GROUP_LANES, GROUP_LANES), 1), HEAD_SHIFT)
    return rr == cc


def _block_diag(x, mask):
    tiled = jnp.concatenate([x] * HEADS_PER_GROUP, axis=0)
    return jnp.where(mask, tiled, jnp.zeros_like(tiled))


def _diag_blocks(q, lane_head):
    out = jnp.where(lane_head == 0, q[0:HEAD_DIM], 0.0)
    for h in range(1, HEADS_PER_GROUP):
        out = out + jnp.where(lane_head == h, q[h * HEAD_DIM:(h + 1) * HEAD_DIM], 0.0)
    return out


def _chunk_algebra(pp_ref, reff_ref, yloc_ref, m_ref, gs_ref, *, tc):
    t = lax.broadcasted_iota(jnp.int32, (CHUNK, GROUP_LANES), 0)
    lane = lax.broadcasted_iota(jnp.int32, (CHUNK, GROUP_LANES), 1)
    s = jnp.bitwise_and(lane, CHUNK - 1)
    lane_head = jnp.right_shift(lane, HEAD_SHIFT)
    eye = jnp.where(s == t, 1.0, 0.0)
    before = (s < t, s > t)
    upto = (s <= t, s >= t)
    bdmask = _block_diag_mask()
    t64 = lax.broadcasted_iota(jnp.int32, (CHUNK, CHUNK), 0)
    s64 = lax.broadcasted_iota(jnp.int32, (CHUNK, CHUNK), 1)
    tri = (jnp.where(s64 <= t64, 1.0, 0.0).astype(BF16), jnp.where(s64 >= t64, 1.0, 0.0).astype(BF16))

    bdiag = lambda x: _block_diag(_bf(x), bdmask)
    stack = lambda a, b: _bf(jnp.concatenate([a, b], axis=0))
    units = [(d, slice(ci * CHUNK, (ci + 1) * CHUNK), slice(gi * GROUP_LANES, (gi + 1) * GROUP_LANES))
             for d in range(2) for ci in range(tc // CHUNK) for gi in range(N_GROUPS)]
    nu = range(len(units))
    dirs = [d for d, _, _ in units]

    cum_chunk = {}
    for d in range(2):
        for ci in range(tc // CHUNK):
            h3 = _split3(pp_ref[P_LW + d, ci * CHUNK:(ci + 1) * CHUNK, :])
            cum_chunk[d, ci] = _dot(tri[d], h3[0]) + _dot(tri[d], h3[1]) + _dot(tri[d], h3[2])
    kt, rt, kh, bh, kb, bb, vv, e_tot = [], [], [], [], [], [], [], []
    for u, (d, rows, lanes) in enumerate(units):
        lw = pp_ref[P_LW + d, rows, lanes]
        cum = cum_chunk[d, rows.start // CHUNK][:, lanes]
        e_neg = jnp.exp(-cum)
        et = jnp.exp(jnp.sum(lw, axis=0, keepdims=True))
        kt.append(pp_ref[P_KK, rows, lanes] * jnp.exp(cum - lw))
        rt.append(pp_ref[P_R, rows, lanes] * jnp.exp(cum))
        kh.append(pp_ref[P_KD + d, rows, lanes] * e_neg)
        bh.append(pp_ref[P_B + d, rows, lanes] * e_neg)
        kb.append(kh[u] * et)
        bb.append(bh[u] * et)
        vv.append(pp_ref[P_V, rows, lanes])
        e_tot.append(et)

    lhs = [stack(kt[u], rt[u]) for u in nu]
    sk = [_dot_nt(lhs[u], bdiag(kh[u])) for u in nu]
    sb = [_dot_nt(lhs[u], bdiag(bh[u])) for u in nu]
    a_k = [jnp.where(before[dirs[u]], sk[u][:CHUNK], 0.0) for u in nu]
    a_rk = [jnp.where(upto[dirs[u]], sk[u][CHUNK:], 0.0) for u in nu]
    n = [jnp.where(before[dirs[u]], -sb[u][:CHUNK], 0.0) for u in nu]
    a_rb = [_bf(jnp.where(upto[dirs[u]], sb[u][CHUNK:], 0.0)) for u in nu]

    tinv = [eye + n[u] for u in nu]
    p = [_dot(_bf(n[u]), bdiag(n[u])) for u in nu]
    av = [_dot(stack(a_k[u], a_rk[u]), bdiag(vv[u])) for u in nu]
    for _ in range(4):
        tp = [_dot(stack(tinv[u], p[u]), bdiag(p[u])) for u in nu]
        tinv = [tinv[u] + tp[u][:CHUNK] for u in nu]
        p = [tp[u][CHUNK:] for u in nu]
    tb = [_bf(tinv[u] + _dot(_bf(tinv[u]), bdiag(p[u]))) for u in nu]

    uk = [_dot(tb[u], bdiag(kt[u])) for u in nu]
    uv = [_dot(tb[u], bdiag(av[u][:CHUNK])) for u in nu]
    reff = [rt[u] - _dot(a_rb[u], bdiag(uk[u])) for u in nu]
    yloc = [av[u][CHUNK:] - _dot(a_rb[u], bdiag(uv[u])) for u in nu]
    q = [_dot_tn(_bf(bb[u]), _bf(uk[u])) for u in nu]
    q2 = [_dot_tn(stack(kb[u], -bb[u]), stack(vv[u], uv[u])) for u in nu]
    for u, (d, rows, lanes) in enumerate(units):
        reff_ref[d, 0, rows, lanes] = reff[u]
        yloc_ref[d, 0, rows, lanes] = yloc[u]
        m_ref[d, 0, rows, lanes] = eye * e_tot[u] - _diag_blocks(q[u], lane_head)
        gs_ref[d, 0, rows, lanes] = _diag_blocks(q2[u], lane_head)


def _state_kernel(m0_ref, g0_ref, re0_ref, yl0_ref, m1_ref, g1_ref, re1_ref, yl1_ref, s0_ref,
                  y0_ref, y1_ref, h_ref, *, batch):
    c = pl.program_id(0)

    @pl.when(c == 0)
    def _():
        h_ref[...] = s0_ref[...]

    bdmask = _block_diag_mask()
    per_dir = ((m0_ref, g0_ref, re0_ref, yl0_ref, y0_ref), (m1_ref, g1_ref, re1_ref, yl1_ref, y1_ref))
    units = [(d, bi, slice(gi * GROUP_LANES, (gi + 1) * GROUP_LANES))
             for d in range(2) for bi in range(batch) for gi in range(N_GROUPS)]
    hs = [_split2(h_ref[d, bi, :, lanes]) for d, bi, lanes in units]
    ms = [_split2(per_dir[d][0][0, bi, :, lanes]) for d, bi, lanes in units]
    rs = [_split2(per_dir[d][2][0, bi, :, lanes]) for d, bi, lanes in units]
    o1 = [_dot(jnp.concatenate([ms[u][0], ms[u][1], rs[u][0], rs[u][1]], axis=0), _block_diag(hs[u][0], bdmask))
          for u in range(len(units))]
    o2 = [_dot(jnp.concatenate([ms[u][0], rs[u][0]], axis=0), _block_diag(hs[u][1], bdmask))
          for u in range(len(units))]
    for u, (d, bi, lanes) in enumerate(units):
        _, g_ref, _, yl_ref, y_ref = per_dir[d]
        mh_new = o1[u][0:CHUNK] + o1[u][CHUNK:2 * CHUNK] + o2[u][0:CHUNK]
        rh_new = o1[u][2 * CHUNK:3 * CHUNK] + o1[u][3 * CHUNK:] + o2[u][CHUNK:]
        y_ref[bi, :, lanes] = yl_ref[0, bi, :, lanes] + rh_new
        h_ref[d, bi, :, lanes] = mh_new + g_ref[0, bi, :, lanes]


def _state_pass(m, g, reff, yloc, s0):
    _, b, l, _ = m.shape
    nc = l // CHUNK
    blk = (1, b, CHUNK, R_WIDTH)
    fwd = pl.BlockSpec(blk, lambda c: (0, 0, c, 0))
    bwd = pl.BlockSpec(blk, lambda c: (1, 0, nc - 1 - c, 0))
    st = pl.BlockSpec((2, b, HEAD_DIM, R_WIDTH), lambda c: (0, 0, 0, 0))
    sy = jax.ShapeDtypeStruct((b, l, R_WIDTH), F32)
    return pl.pallas_call(
        functools.partial(_state_kernel, batch=b),
        grid=(nc,),
        in_specs=[fwd, fwd, fwd, fwd, bwd, bwd, bwd, bwd, st],
        out_specs=[pl.BlockSpec((b, CHUNK, R_WIDTH), lambda c: (0, c, 0)),
                   pl.BlockSpec((b, CHUNK, R_WIDTH), lambda c: (0, nc - 1 - c, 0)),
                   st],
        out_shape=[sy, sy, jax.ShapeDtypeStruct((2, b, HEAD_DIM, R_WIDTH), F32)],
        compiler_params=_params("arbitrary"),
        name="rwkv_state",
    )(m, g, reff, yloc, m, g, reff, yloc, s0)


def _posdft_kernel(cs_ref, z_ref, o_ref):
    l = z_ref.shape[2]
    z = z_ref[0].reshape(2 * l, F_WIDTH)
    o_ref[0] = _bf(_dot(cs_ref[...], z))


def _pos_dft(z, cs):
    b, _, l, _ = z.shape
    tm = min(256, l)
    return pl.pallas_call(
        _posdft_kernel,
        grid=(b, l // tm),
        in_specs=[pl.BlockSpec((tm, 2 * l), lambda bi, i: (i, 0)),
                  pl.BlockSpec((1, 2, l, F_WIDTH), lambda bi, i: (bi, 0, 0, 0))],
        out_specs=pl.BlockSpec((1, tm, F_WIDTH), lambda bi, i: (bi, i, 0)),
        out_shape=jax.ShapeDtypeStruct((b, l, F_WIDTH), BF16),
        compiler_params=_params("parallel", "parallel"),
        name="pos_dft",
    )(cs, z)


def _merge_kernel(f_ref, y0_ref, y1_ref, g_ref, bonus_ref, gates_ref, x_ref, mod_ref,
                  lnw_ref, lnb_ref, eh_ref, wf_ref, wr_ref, wo_ref, o_ref):
    y = y0_ref[0] + y1_ref[0]
    eh = eh_ref[...]
    inv_n = 1.0 / HEAD_DIM
    mu = _dot_exact_rhs(y, eh) * inv_n
    dlt = y - mu
    var = _dot_exact_rhs(dlt * dlt, eh) * inv_n
    yn = dlt * lax.rsqrt(var + GN_EPS) * lnw_ref[...] + lnb_ref[...]
    rwkv = _bf((yn + bonus_ref[0]) * g_ref[0])
    fo = _dot(f_ref[0], wf_ref[...])
    ro = _dot(rwkv, wr_ref[...])
    gates = gates_ref[0].astype(F32)
    merged = jax.nn.sigmoid(gates[:, :D_MODEL]) * fo + jax.nn.sigmoid(gates[:, D_MODEL:]) * ro
    out = _dot(_bf(merged), wo_ref[...])
    o_ref[0] = x_ref[0] + mod_ref[0, 2:3, :] * out


def _merge(f, y0, y1, g, bonus, gates, x, mod, mod_row, lnw, lnb, eh, wf, wr, wo):
    b, l, d = x.shape
    tm = min(256, l)
    row = lambda w: pl.BlockSpec((1, tm, w), lambda bi, i: (bi, i, 0))
    consts = [lnw, lnb, eh, wf, wr, wo]
    return pl.pallas_call(
        _merge_kernel,
        grid=(b, l // tm),
        in_specs=[row(F_WIDTH), row(R_WIDTH), row(R_WIDTH), row(R_WIDTH), row(R_WIDTH), row(GATE_W), row(d),
                  pl.BlockSpec((1, N_MOD, d), lambda bi, i: (mod_row(bi), 0, 0))]
                 + [_const_spec(c.shape) for c in consts],
        out_specs=row(d),
        out_shape=jax.ShapeDtypeStruct((b, l, d), F32),
        compiler_params=_params("parallel", "parallel"),
        name="merge",
    )(f, y0, y1, g, bonus, gates, x, mod, *consts)


def _mlp_kernel(x_ref, mod_ref, n2_ref, w1_ref, w2_ref, nf_ref, o_ref, *, final_norm):
    x = x_ref[0]
    h = _bf(_rms(x) * n2_ref[...] * (1.0 + mod_ref[0, 4:5, :]) + mod_ref[0, 3:4, :])
    acc = jnp.zeros(x.shape, F32)
    step = 1024
    for j in range(D_FF // step):
        u = jnp.maximum(_dot(h, w1_ref[:, j * step:(j + 1) * step]), 0.0)
        acc = acc + _dot(_bf(u * u), w2_ref[j * step:(j + 1) * step, :])
    x2 = x + mod_ref[0, 5:6, :] * acc
    if final_norm:
        x2 = _rms(x2) * nf_ref[...]
    o_ref[0] = x2


def _mlp(x, mod, mod_row, n2, w1, w2, nf, final_norm):
    b, l, d = x.shape
    tm = min(512, l)
    row = pl.BlockSpec((1, tm, d), lambda bi, i: (bi, i, 0))
    once = lambda shape: pl.BlockSpec(shape, lambda bi, i: (0, 0), pipeline_mode=pl.Buffered(1))
    return pl.pallas_call(
        functools.partial(_mlp_kernel, final_norm=final_norm),
        grid=(b, l // tm),
        in_specs=[row, pl.BlockSpec((1, N_MOD, d), lambda bi, i: (mod_row(bi), 0, 0)),
                  _const_spec((1, d)), once(w1.shape), once(w2.shape), _const_spec((1, d))],
        out_specs=row,
        out_shape=jax.ShapeDtypeStruct((b, l, d), F32),
        compiler_params=_params("parallel", "parallel"),
        name="mlp",
    )(x, mod, n2, w1, w2, nf)


def _channel_dft():
    n = FGROUP_DIM
    jk = np.outer(np.arange(n), np.arange(n)) % n
    ang = 2.0 * np.pi * jk / n
    c = np.cos(ang) / np.sqrt(n)
    s = np.sin(ang) / np.sqrt(n)
    g = F_WIDTH // n
    out = np.zeros((F_WIDTH, 2 * F_WIDTH), np.float32)
    for i in range(g):
        out[i * n:(i + 1) * n, i * n:(i + 1) * n] = c
        out[i * n:(i + 1) * n, F_WIDTH + i * n:F_WIDTH + (i + 1) * n] = s
    return jnp.asarray(out)


def _position_dft(l):
    jk = np.outer(np.arange(l), np.arange(l)) % l
    ang = 2.0 * np.pi * jk / l
    cs = np.concatenate([np.cos(ang), -np.sin(ang)], axis=1) / np.sqrt(l)
    return jnp.asarray(cs.astype(np.float32)).astype(BF16)


def _head_ones():
    h = np.arange(R_WIDTH) // HEAD_DIM
    return jnp.asarray((h[:, None] == h[None, :]).astype(np.float32)).astype(BF16)


def _two_dir_lora(w):
    z = jnp.zeros_like(w[0])
    return _bf(jnp.concatenate([jnp.concatenate([w[0], z], axis=1),
                                jnp.concatenate([z, w[1]], axis=1)], axis=0))


def _token_mixer(x, mod, mod_row, grid_mode, need_out, s0, lw_, consts):
    rw_out = _inproj(x, mod, mod_row, lw_["n1"], lw_["wrw"], lw_["wz"], lw_["wg"], need_out)
    rw = rw_out[0]
    reff, yloc, m, gs, g, bonus = _scan_pass(rw, grid_mode, lw_["mu"], lw_["w0"], lw_["a0"], lw_["wup"],
                                             lw_["aup"], lw_["gup"], lw_["kkw"], lw_["ka"], lw_["rk"],
                                             consts["eh"])
    y0, y1, s_fin = _state_pass(m, gs, reff, yloc, s0)
    if not need_out:
        return None, s_fin
    z, gates = rw_out[1], rw_out[2]
    f = _pos_dft(z, consts["pos_dft"][x.shape[1]])
    x1 = _merge(f, y0, y1, g, bonus, gates, x, mod, mod_row, lw_["lnw"], lw_["lnb"], consts["eh"],
                lw_["wf"], lw_["wr"], lw_["wo"])
    return x1, s_fin


def kernel(x, c, ctx, c_ctx, w_mod, b_mod, norm1, norm2, w_in, mu_shift, w0, w_up, a0, a_up, g_up,
           k_k, k_a, r_k, ln_x_w, ln_x_b, w_fourier_up, w_rwkv_up, w_out, mlp_w1, mlp_w2, norm_f):
    depth = w_mod.shape[0]
    batch, seq, d = x.shape
    ctx_len = ctx.shape[1]
    assert d == D_MODEL and batch + 1 <= MOD_ROWS
    assert seq % GRID_W == 0 and ctx_len % CHUNK == 0 and seq % CHUNK == 0

    consts = {
        "eh": _head_ones(),
        "pos_dft": {n: _position_dft(n) for n in {seq, ctx_len}},
    }
    cdft = _channel_dft()
    cc = jnp.zeros((MOD_ROWS, d), F32).at[:batch].set(c).at[batch].set(c_ctx)
    row1 = lambda a: a.reshape(1, -1)
    lat_row = lambda bi: bi
    ctx_row = lambda bi: batch

    x_lat, x_ctx = x, ctx
    s_zero = jnp.zeros((2, batch, HEAD_DIM, R_WIDTH), F32)
    for l in range(depth):
        last = l == depth - 1
        mod = _modulation(cc, w_mod[l], b_mod[l]).reshape(MOD_ROWS, N_MOD, d)
        lw_ = {
            "n1": row1(norm1[l]),
            "wz": _fold_channel_dft(w_in[l][:, :F_WIDTH], cdft),
            "wrw": _bf(w_in[l][:, F_WIDTH:F_WIDTH + RWKV_IN]),
            "wg": _bf(w_in[l][:, F_WIDTH + RWKV_IN:]),
            "mu": row1(mu_shift[l]),
            "w0": row1(w0[l]), "a0": row1(a0[l]),
            "wup": _two_dir_lora(w_up[l]), "aup": _two_dir_lora(a_up[l]), "gup": _bf(g_up[l]),
            "kkw": row1(k_k[l]), "ka": row1(k_a[l]), "rk": row1(r_k[l]),
            "lnw": row1(ln_x_w[l]), "lnb": row1(ln_x_b[l]),
            "wf": _bf(w_fourier_up[l]), "wr": _bf(w_rwkv_up[l]), "wo": _bf(w_out[l]),
        }
        n2 = row1(norm2[l])
        w1 = _bf(mlp_w1[l])
        w2 = _bf(mlp_w2[l])
        nf = row1(norm_f)

        ctx1, s_ctx = _token_mixer(x_ctx, mod, ctx_row, False, not last, s_zero, lw_, consts)
        lat1, _ = _token_mixer(x_lat, mod, lat_row, True, True, s_ctx, lw_, consts)
        x_lat = _mlp(lat1, mod, lat_row, n2, w1, w2, nf, final_norm=last)
        if not last:
            x_ctx = _mlp(ctx1, mod, ctx_row, n2, w1, w2, nf, final_norm=False)
    return x_lat
```

```python
import functools

import numpy as np
import jax
import jax.numpy as jnp
from jax import lax
from jax.experimental import pallas as pl
from jax.experimental.pallas import tpu as pltpu

F32 = jnp.float32
BF16 = jnp.bfloat16

D_MODEL = 1024
GRID_W = 64
F_WIDTH = 512
FGROUP_DIM = 128
HEAD_DIM = 64
N_RHEADS = 8
R_WIDTH = N_RHEADS * HEAD_DIM
D_LORA = 64
D_GATE_LORA = 128
RWKV_IN = 3 * R_WIDTH + 4 * D_LORA + D_GATE_LORA
GATE_W = 2 * D_MODEL
D_FF = 4 * D_MODEL
N_MOD = 6
NORM_EPS = 1e-6
GN_EPS = 64e-5
L2_EPS = 1e-12

CHUNK = 64
GROUP_LANES = 256
HEADS_PER_GROUP = GROUP_LANES // HEAD_DIM
HEAD_SHIFT = HEAD_DIM.bit_length() - 1
N_GROUPS = R_WIDTH // GROUP_LANES
MOD_ROWS = 16
VMEM_LIMIT = 56 * 1024 * 1024


def _bf(x):
    return x.astype(BF16)


def _dot(a, b):
    return jnp.dot(a, b, preferred_element_type=F32)


def _dot_nt(a, b):
    return lax.dot_general(a, b, (((1,), (1,)), ((), ())), preferred_element_type=F32)


def _dot_tn(a, b):
    return lax.dot_general(a, b, (((0,), (0,)), ((), ())), preferred_element_type=F32)


def _split2(x):
    hi = _bf(x)
    lo = _bf(x - hi.astype(F32))
    return hi, lo


def _split3(x):
    hi = _bf(x)
    r1 = x - hi.astype(F32)
    mid = _bf(r1)
    lo = _bf(r1 - mid.astype(F32))
    return hi, mid, lo


def _dot_exact_rhs(x, w):
    hi, lo = _split2(x)
    return _dot(hi, w) + _dot(lo, w)


def _params(*sem):
    return pltpu.CompilerParams(dimension_semantics=sem, vmem_limit_bytes=VMEM_LIMIT)


def _const_spec(shape):
    zeros = (0,) * len(shape)
    return pl.BlockSpec(shape, lambda *_: zeros)


def _linear_kernel(x_ref, w_ref, b_ref, o_ref, *, silu):
    x = x_ref[...]
    if silu:
        x = x * jax.nn.sigmoid(x)
    o_ref[...] = _dot(_bf(x), _bf(w_ref[...])) + b_ref[...]


def _modulation(cc, w, b):
    n = w.shape[1]
    tn = 512
    return pl.pallas_call(
        functools.partial(_linear_kernel, silu=True),
        grid=(n // tn,),
        in_specs=[_const_spec(cc.shape),
                  pl.BlockSpec((w.shape[0], tn), lambda j: (0, j)),
                  pl.BlockSpec((1, tn), lambda j: (0, j))],
        out_specs=pl.BlockSpec((cc.shape[0], tn), lambda j: (0, j)),
        out_shape=jax.ShapeDtypeStruct((cc.shape[0], n), F32),
        compiler_params=_params("parallel"),
        name="modulation",
    )(cc, w, b.reshape(1, n))


def _fold_kernel(w_ref, c_ref, o_ref):
    wh, wl = _split2(w_ref[...])
    ch, cl = _split2(c_ref[...])
    o_ref[...] = _bf(_dot(wh, ch) + _dot(wl, ch) + _dot(wh, cl))


def _fold_channel_dft(w_f, cdft):
    return pl.pallas_call(
        _fold_kernel,
        grid=(1,),
        in_specs=[_const_spec(w_f.shape), _const_spec(cdft.shape)],
        out_specs=_const_spec((w_f.shape[0], cdft.shape[1])),
        out_shape=jax.ShapeDtypeStruct((w_f.shape[0], cdft.shape[1]), BF16),
        compiler_params=_params("arbitrary"),
        name="fold_channel_dft",
    )(w_f, cdft)


def _rms(x):
    return x * lax.rsqrt(jnp.mean(x * x, axis=-1, keepdims=True) + NORM_EPS)


def _inproj_kernel(x_ref, mod_ref, n1_ref, win_ref, *rest, need_out):
    if need_out:
        wz_ref, rw_ref, z_ref, g_ref = rest
    else:
        (rw_ref,) = rest
    x = x_ref[0]
    shift = mod_ref[0, 0:1, :]
    scale = mod_ref[0, 1:2, :]
    h = _bf(_rms(x) * n1_ref[...] * (1.0 + scale) + shift)
    rw_ref[0] = _dot(h, win_ref[:, F_WIDTH:F_WIDTH + RWKV_IN])
    if need_out:
        z = _dot(h, wz_ref[...])
        z_ref[0, 0] = _bf(z[:, :F_WIDTH])
        z_ref[0, 1] = _bf(z[:, F_WIDTH:])
        g_ref[0] = _bf(_dot(h, win_ref[:, F_WIDTH + RWKV_IN:]))


def _inproj(x, mod, mod_row, n1, win, wz, need_out):
    b, l, d = x.shape
    tm = min(256, l)
    row_spec = lambda w: pl.BlockSpec((1, tm, w), lambda bi, i: (bi, i, 0))
    in_specs = [row_spec(d),
                pl.BlockSpec((1, N_MOD, d), lambda bi, i: (mod_row(bi), 0, 0)),
                _const_spec((1, d)), _const_spec(win.shape)]
    args = [x, mod, n1, win]
    out_specs = [row_spec(RWKV_IN)]
    out_shape = [jax.ShapeDtypeStruct((b, l, RWKV_IN), F32)]
    if need_out:
        in_specs += [_const_spec(wz.shape)]
        args += [wz]
        out_specs += [pl.BlockSpec((1, 2, tm, F_WIDTH), lambda bi, i: (bi, 0, i, 0)), row_spec(GATE_W)]
        out_shape += [jax.ShapeDtypeStruct((b, 2, l, F_WIDTH), BF16),
                      jax.ShapeDtypeStruct((b, l, GATE_W), BF16)]
    return pl.pallas_call(
        functools.partial(_inproj_kernel, need_out=need_out),
        grid=(b, l // tm),
        in_specs=in_specs, out_specs=out_specs, out_shape=out_shape,
        compiler_params=_params("parallel", "parallel"),
        name="inproj",
    )(*args)


def _log_sigmoid(x):
    return jnp.minimum(x, 0.0) - jnp.log1p(jnp.exp(-jnp.abs(x)))


P_R, P_V, P_KK, P_KD, P_LW, P_B = 0, 1, 2, 3, 5, 7
N_PREP = 9


def _scan_kernel(*refs, grid_mode, seq_len, tm):
    reff_ref, yloc_ref, m_ref, gs_ref, g_ref, bonus_ref, pp_ref = refs[-7:]
    _prep_tile(refs[:-7], pp_ref, g_ref, bonus_ref, grid_mode=grid_mode, seq_len=seq_len, tm=tm)
    _chunk_algebra(pp_ref, reff_ref, yloc_ref, m_ref, gs_ref, tc=tm)


def _prep_tile(in_refs, pp_ref, g_ref, bonus_ref, *, grid_mode, seq_len, tm):
    if grid_mode:
        (rw_ref, prev_ref, next_ref, mu_ref, w0_ref, a0_ref, wup_ref, aup_ref, gup_ref,
         kkw_ref, ka_ref, rk_ref, eh_ref) = in_refs
    else:
        (rw_ref, mu_ref, w0_ref, a0_ref, wup_ref, aup_ref, gup_ref,
         kkw_ref, ka_ref, rk_ref, eh_ref) = in_refs
    i = pl.program_id(1)
    x = rw_ref[0]
    t_loc = lax.broadcasted_iota(jnp.int32, (tm, 128), 0)
    t_glob = t_loc + i * tm
    lane = lax.broadcasted_iota(jnp.int32, (tm, 128), 1)
    if grid_mode:
        ext = jnp.concatenate([prev_ref[0], x, next_ref[0]], axis=0)
        n_ext = tm + 2 * GRID_W
        col = jnp.bitwise_and(t_loc, GRID_W - 1)
        masks = [col != 0, col != GRID_W - 1, t_glob >= GRID_W, t_glob < seq_len - GRID_W]
        n_parts = 4
    else:
        ext = x
        n_ext = tm
        masks = [t_glob != 0, t_glob != seq_len - 1]
        n_parts = 2
    part_w = RWKV_IN // n_parts

    def shifted(j, part):
        e = ext[:, 128 * j:128 * (j + 1)]
        if grid_mode:
            if part == 0:
                s = pltpu.roll(e, 1, 0)[GRID_W:GRID_W + tm]
            elif part == 1:
                s = pltpu.roll(e, n_ext - 1, 0)[GRID_W:GRID_W + tm]
            elif part == 2:
                s = e[0:tm]
            else:
                s = e[2 * GRID_W:2 * GRID_W + tm]
        else:
            s = pltpu.roll(e, 1, 0) if part == 0 else pltpu.roll(e, n_ext - 1, 0)
        return jnp.where(masks[part], s, 0.0)

    blocks = []
    for j in range(RWKV_IN // 128):
        p_lo = (128 * j) // part_w
        p_hi = (128 * j + 127) // part_w
        s = shifted(j, p_lo)
        if p_hi != p_lo:
            s = jnp.where(lane + 128 * j < part_w * p_hi, s, shifted(j, p_hi))
        xj = x[:, 128 * j:128 * (j + 1)]
        blocks.append(xj + mu_ref[:, 128 * j:128 * (j + 1)] * (s - xj))

    nb = R_WIDTH // 128
    r = jnp.concatenate(blocks[0:nb], axis=1)
    k = jnp.concatenate(blocks[nb:2 * nb], axis=1)
    v = jnp.concatenate(blocks[2 * nb:3 * nb], axis=1)
    wd, ad, gd = blocks[3 * nb], blocks[3 * nb + 1], blocks[3 * nb + 2]

    w_logit = w0_ref[...] + _dot(_bf(jnp.tanh(wd)), wup_ref[...])
    lw = -jnp.exp(_log_sigmoid(w_logit) - 0.5)
    a = jax.nn.sigmoid(a0_ref[...] + _dot(_bf(ad), aup_ref[...]))
    g = _dot(_bf(jax.nn.sigmoid(gd)), gup_ref[...])

    eh = eh_ref[...]
    kx = k * kkw_ref[...]
    nrm = jnp.maximum(jnp.sqrt(_dot_exact_rhs(kx * kx, eh)), L2_EPS)
    kk = kx / nrm
    ka = ka_ref[...]
    kd0 = k * (1.0 + (a[:, :R_WIDTH] - 1.0) * ka)
    kd1 = k * (1.0 + (a[:, R_WIDTH:] - 1.0) * ka)
    bonus = _dot_exact_rhs(r * (kd0 + kd1) * rk_ref[...], eh) * v

    pp_ref[P_R] = r
    pp_ref[P_V] = v
    pp_ref[P_KK] = kk
    pp_ref[P_KD] = kd0
    pp_ref[P_KD + 1] = kd1
    pp_ref[P_LW] = lw[:, :R_WIDTH]
    pp_ref[P_LW + 1] = lw[:, R_WIDTH:]
    pp_ref[P_B] = kk * a[:, :R_WIDTH]
    pp_ref[P_B + 1] = kk * a[:, R_WIDTH:]
    g_ref[0] = _bf(g)
    bonus_ref[0] = _bf(bonus)


def _scan_pass(rw, grid_mode, mu, w0c, a0c, wup, aup, gup, kkw, ka, rk, eh):
    b, l, _ = rw.shape
    tm = min(256, l) if grid_mode else l
    nrow = l // GRID_W
    per = tm // GRID_W
    row_spec = pl.BlockSpec((1, tm, RWKV_IN), lambda bi, i: (bi, i, 0))
    in_specs = [row_spec]
    args = [rw]
    if grid_mode:
        in_specs += [
            pl.BlockSpec((1, GRID_W, RWKV_IN), lambda bi, i: (bi, jnp.maximum(i * per - 1, 0), 0)),
            pl.BlockSpec((1, GRID_W, RWKV_IN), lambda bi, i: (bi, jnp.minimum((i + 1) * per, nrow - 1), 0)),
        ]
        args += [rw, rw]
    consts = [mu, w0c, a0c, wup, aup, gup, kkw, ka, rk, eh]
    in_specs += [_const_spec(c.shape) for c in consts]
    args += consts
    one = pl.BlockSpec((1, tm, R_WIDTH), lambda bi, i: (bi, i, 0))
    two = pl.BlockSpec((2, 1, tm, R_WIDTH), lambda bi, i: (0, bi, i, 0))
    s1 = jax.ShapeDtypeStruct((b, l, R_WIDTH), BF16)
    s2 = jax.ShapeDtypeStruct((2, b, l, R_WIDTH), F32)
    s2h = jax.ShapeDtypeStruct((2, b, l, R_WIDTH), BF16)
    return pl.pallas_call(
        functools.partial(_scan_kernel, grid_mode=grid_mode, seq_len=l, tm=tm),
        grid=(b, l // tm),
        in_specs=in_specs,
        out_specs=[two, two, two, two, one, one],
        out_shape=[s2h, s2h, s2, s2, s1, s1],
        scratch_shapes=[pltpu.VMEM((N_PREP, tm, R_WIDTH), F32)],
        compiler_params=_params("parallel", "parallel"),
        name="rwkv_scan",
    )(*args)


def _block_diag_mask():
    rr = jnp.right_shift(lax.broadcasted_iota(jnp.int32, (GROUP_LANES, GROUP_LANES), 0), HEAD_SHIFT)
    cc = jnp.right_shift(lax.broadcasted_iota(jnp.int32, (GROUP_LANES, GROUP_LANES), 1), HEAD_SHIFT)
    return rr == cc


def _block_diag(x, mask):
    tiled = jnp.concatenate([x] * HEADS_PER_GROUP, axis=0)
    return jnp.where(mask, tiled, jnp.zeros_like(tiled))


def _diag_blocks(q, lane_head):
    out = jnp.where(lane_head == 0, q[0:HEAD_DIM], 0.0)
    for h in range(1, HEADS_PER_GROUP):
        out = out + jnp.where(lane_head == h, q[h * HEAD_DIM:(h + 1) * HEAD_DIM], 0.0)
    return out


def _chunk_algebra(pp_ref, reff_ref, yloc_ref, m_ref, gs_ref, *, tc):
    t = lax.broadcasted_iota(jnp.int32, (CHUNK, GROUP_LANES), 0)
    lane = lax.broadcasted_iota(jnp.int32, (CHUNK, GROUP_LANES), 1)
    s = jnp.bitwise_and(lane, CHUNK - 1)
    lane_head = jnp.right_shift(lane, HEAD_SHIFT)
    eye = jnp.where(s == t, 1.0, 0.0)
    before = (s < t, s > t)
    upto = (s <= t, s >= t)
    bdmask = _block_diag_mask()
    t64 = lax.broadcasted_iota(jnp.int32, (CHUNK, CHUNK), 0)
    s64 = lax.broadcasted_iota(jnp.int32, (CHUNK, CHUNK), 1)
    tri = (jnp.where(s64 <= t64, 1.0, 0.0).astype(BF16), jnp.where(s64 >= t64, 1.0, 0.0).astype(BF16))

    bdiag = lambda x: _block_diag(_bf(x), bdmask)
    stack = lambda a, b: _bf(jnp.concatenate([a, b], axis=0))
    units = [(d, slice(ci * CHUNK, (ci + 1) * CHUNK), slice(gi * GROUP_LANES, (gi + 1) * GROUP_LANES))
             for d in range(2) for ci in range(tc // CHUNK) for gi in range(N_GROUPS)]
    nu = range(len(units))
    dirs = [d for d, _, _ in units]

    cum_chunk = {}
    for d in range(2):
        for ci in range(tc // CHUNK):
            h3 = _split3(pp_ref[P_LW + d, ci * CHUNK:(ci + 1) * CHUNK, :])
            cum_chunk[d, ci] = _dot(tri[d], h3[0]) + _dot(tri[d], h3[1]) + _dot(tri[d], h3[2])
    kt, rt, kh, bh, kb, bb, vv, e_tot = [], [], [], [], [], [], [], []
    for u, (d, rows, lanes) in enumerate(units):
        lw = pp_ref[P_LW + d, rows, lanes]
        cum = cum_chunk[d, rows.start // CHUNK][:, lanes]
        e_neg = jnp.exp(-cum)
        et = jnp.exp(jnp.sum(lw, axis=0, keepdims=True))
        kt.append(pp_ref[P_KK, rows, lanes] * jnp.exp(cum - lw))
        rt.append(pp_ref[P_R, rows, lanes] * jnp.exp(cum))
        kh.append(pp_ref[P_KD + d, rows, lanes] * e_neg)
        bh.append(pp_ref[P_B + d, rows, lanes] * e_neg)
        kb.append(kh[u] * et)
        bb.append(bh[u] * et)
        vv.append(pp_ref[P_V, rows, lanes])
        e_tot.append(et)

    lhs = [stack(kt[u], rt[u]) for u in nu]
    sk = [_dot_nt(lhs[u], bdiag(kh[u])) for u in nu]
    sb = [_dot_nt(lhs[u], bdiag(bh[u])) for u in nu]
    a_k = [jnp.where(before[dirs[u]], sk[u][:CHUNK], 0.0) for u in nu]
    a_rk = [jnp.where(upto[dirs[u]], sk[u][CHUNK:], 0.0) for u in nu]
    n = [jnp.where(before[dirs[u]], -sb[u][:CHUNK], 0.0) for u in nu]
    a_rb = [_bf(jnp.where(upto[dirs[u]], sb[u][CHUNK:], 0.0)) for u in nu]

    tinv = [eye + n[u] for u in nu]
    p = [_dot(_bf(n[u]), bdiag(n[u])) for u in nu]
    av = [_dot(stack(a_k[u], a_rk[u]), bdiag(vv[u])) for u in nu]
    for _ in range(4):
        tp = [_dot(stack(tinv[u], p[u]), bdiag(p[u])) for u in nu]
        tinv = [tinv[u] + tp[u][:CHUNK] for u in nu]
        p = [tp[u][CHUNK:] for u in nu]
    tb = [_bf(tinv[u] + _dot(_bf(tinv[u]), bdiag(p[u]))) for u in nu]

    uk = [_dot(tb[u], bdiag(kt[u])) for u in nu]
    uv = [_dot(tb[u], bdiag(av[u][:CHUNK])) for u in nu]
    reff = [rt[u] - _dot(a_rb[u], bdiag(uk[u])) for u in nu]
    yloc = [av[u][CHUNK:] - _dot(a_rb[u], bdiag(uv[u])) for u in nu]
    q = [_dot_tn(_bf(bb[u]), _bf(uk[u])) for u in nu]
    q2 = [_dot_tn(stack(kb[u], -bb[u]), stack(vv[u], uv[u])) for u in nu]
    for u, (d, rows, lanes) in enumerate(units):
        reff_ref[d, 0, rows, lanes] = _bf(reff[u])
        yloc_ref[d, 0, rows, lanes] = _bf(yloc[u])
        m_ref[d, 0, rows, lanes] = eye * e_tot[u] - _diag_blocks(q[u], lane_head)
        gs_ref[d, 0, rows, lanes] = _diag_blocks(q2[u], lane_head)


def _state_kernel(m0_ref, g0_ref, re0_ref, yl0_ref, m1_ref, g1_ref, re1_ref, yl1_ref, s0_ref,
                  y0_ref, y1_ref, h_ref, *, batch):
    c = pl.program_id(0)

    @pl.when(c == 0)
    def _():
        h_ref[...] = s0_ref[...]

    bdmask = _block_diag_mask()
    per_dir = ((m0_ref, g0_ref, re0_ref, yl0_ref, y0_ref), (m1_ref, g1_ref, re1_ref, yl1_ref, y1_ref))
    units = [(d, bi, slice(gi * GROUP_LANES, (gi + 1) * GROUP_LANES))
             for d in range(2) for bi in range(batch) for gi in range(N_GROUPS)]
    hs = [_split2(h_ref[d, bi, :, lanes]) for d, bi, lanes in units]
    ms = [_split2(per_dir[d][0][0, bi, :, lanes]) for d, bi, lanes in units]
    rs = [per_dir[d][2][0, bi, :, lanes] for d, bi, lanes in units]
    o1 = [_dot(jnp.concatenate([ms[u][0], ms[u][1], rs[u]], axis=0), _block_diag(hs[u][0], bdmask))
          for u in range(len(units))]
    o2 = [_dot(jnp.concatenate([ms[u][0], rs[u]], axis=0), _block_diag(hs[u][1], bdmask))
          for u in range(len(units))]
    for u, (d, bi, lanes) in enumerate(units):
        _, g_ref, _, yl_ref, y_ref = per_dir[d]
        mh_new = o1[u][0:CHUNK] + o1[u][CHUNK:2 * CHUNK] + o2[u][0:CHUNK]
        rh_new = o1[u][2 * CHUNK:] + o2[u][CHUNK:]
        y_ref[bi, :, lanes] = _bf(yl_ref[0, bi, :, lanes].astype(F32) + rh_new)
        h_ref[d, bi, :, lanes] = mh_new + g_ref[0, bi, :, lanes]


def _state_pass(m, g, reff, yloc, s0):
    _, b, l, _ = m.shape
    nc = l // CHUNK
    blk = (1, b, CHUNK, R_WIDTH)
    fwd = pl.BlockSpec(blk, lambda c: (0, 0, c, 0))
    bwd = pl.BlockSpec(blk, lambda c: (1, 0, nc - 1 - c, 0))
    st = pl.BlockSpec((2, b, HEAD_DIM, R_WIDTH), lambda c: (0, 0, 0, 0))
    sy = jax.ShapeDtypeStruct((b, l, R_WIDTH), BF16)
    return pl.pallas_call(
        functools.partial(_state_kernel, batch=b),
        grid=(nc,),
        in_specs=[fwd, fwd, fwd, fwd, bwd, bwd, bwd, bwd, st],
        out_specs=[pl.BlockSpec((b, CHUNK, R_WIDTH), lambda c: (0, c, 0)),
                   pl.BlockSpec((b, CHUNK, R_WIDTH), lambda c: (0, nc - 1 - c, 0)),
                   st],
        out_shape=[sy, sy, jax.ShapeDtypeStruct((2, b, HEAD_DIM, R_WIDTH), F32)],
        compiler_params=_params("arbitrary"),
        name="rwkv_state",
    )(m, g, reff, yloc, m, g, reff, yloc, s0)


def _posdft_kernel(cs_ref, z_ref, o_ref):
    l = z_ref.shape[2]
    z = z_ref[0].reshape(2 * l, F_WIDTH)
    o_ref[0] = _bf(_dot(cs_ref[...], z))


def _posdft_fold_kernel(w_ref, rev_ref, z_ref, o_ref, f_ref, *, tm):
    l = z_ref.shape[2]
    half = l // 2
    nb = l // REV_BLOCK
    i = pl.program_id(1)

    @pl.when(i == 0)
    def _():
        for part in range(2):
            for j in range(half // REV_BLOCK):
                blk = z_ref[0, part, j * REV_BLOCK:(j + 1) * REV_BLOCK, :]
                src_a = z_ref[0, part, l - (j + 1) * REV_BLOCK:l - j * REV_BLOCK, :]
                jb = (nb - j) % nb
                src_b = z_ref[0, part, jb * REV_BLOCK:(jb + 1) * REV_BLOCK, :]
                rev = _dot(rev_ref[0], src_a) + _dot(rev_ref[1], src_b)
                sgn = 1.0 if part == 0 else -1.0
                f_ref[part * half + j * REV_BLOCK:part * half + (j + 1) * REV_BLOCK, :] = _bf(
                    blk.astype(F32) + sgn * rev)

    row = lax.broadcasted_iota(jnp.int32, (tm, F_WIDTH), 0) + i * tm
    alt = (1 - 2 * jnp.bitwise_and(row, 1)).astype(F32) * (1.0 / float(np.sqrt(l)))
    mid = z_ref[0, 0, half:half + 1, :].astype(F32)
    o_ref[0] = _bf(_dot(w_ref[...], f_ref[...]) + alt * mid)


REV_BLOCK = 256


def _pos_dft(z, cs):
    b, _, l, _ = z.shape
    tm = min(256, l)
    if isinstance(cs, tuple):
        w, rev = cs
        return pl.pallas_call(
            functools.partial(_posdft_fold_kernel, tm=tm),
            grid=(b, l // tm),
            in_specs=[pl.BlockSpec((tm, l), lambda bi, i: (i, 0)),
                      _const_spec(rev.shape),
                      pl.BlockSpec((1, 2, l, F_WIDTH), lambda bi, i: (bi, 0, 0, 0))],
            out_specs=pl.BlockSpec((1, tm, F_WIDTH), lambda bi, i: (bi, i, 0)),
            out_shape=jax.ShapeDtypeStruct((b, l, F_WIDTH), BF16),
            scratch_shapes=[pltpu.VMEM((l, F_WIDTH), BF16)],
            compiler_params=_params("parallel", "arbitrary"),
            name="pos_dft_fold",
        )(w, rev, z)
    return pl.pallas_call(
        _posdft_kernel,
        grid=(b, l // tm),
        in_specs=[pl.BlockSpec((tm, 2 * l), lambda bi, i: (i, 0)),
                  pl.BlockSpec((1, 2, l, F_WIDTH), lambda bi, i: (bi, 0, 0, 0))],
        out_specs=pl.BlockSpec((1, tm, F_WIDTH), lambda bi, i: (bi, i, 0)),
        out_shape=jax.ShapeDtypeStruct((b, l, F_WIDTH), BF16),
        compiler_params=_params("parallel", "parallel"),
        name="pos_dft",
    )(cs, z)


def _merge_mlp_kernel(f_ref, y0_ref, y1_ref, g_ref, bonus_ref, gates_ref, x_ref, mod_ref,
                      lnw_ref, lnb_ref, n2_ref, nf_ref, eh_ref, wf_ref, wr_ref, wo_ref, w1_ref, w2_ref,
                      o_ref, *, final_norm):
    y = y0_ref[0].astype(F32) + y1_ref[0].astype(F32)
    eh = eh_ref[...]
    inv_n = 1.0 / HEAD_DIM
    mu = _dot_exact_rhs(y, eh) * inv_n
    dlt = y - mu
    var = _dot_exact_rhs(dlt * dlt, eh) * inv_n
    yn = dlt * lax.rsqrt(var + GN_EPS) * lnw_ref[...] + lnb_ref[...]
    rwkv = _bf((yn + bonus_ref[0].astype(F32)) * g_ref[0].astype(F32))
    fo = _dot(f_ref[0], wf_ref[...])
    ro = _dot(rwkv, wr_ref[...])
    gates = gates_ref[0].astype(F32)
    merged = jax.nn.sigmoid(gates[:, :D_MODEL]) * fo + jax.nn.sigmoid(gates[:, D_MODEL:]) * ro
    x = x_ref[0] + mod_ref[0, 2:3, :] * _dot(_bf(merged), wo_ref[...])
    h = _bf(_rms(x) * n2_ref[...] * (1.0 + mod_ref[0, 4:5, :]) + mod_ref[0, 3:4, :])
    acc = jnp.zeros(x.shape, F32)
    step = 1024
    for j in range(D_FF // step):
        u = jnp.maximum(_dot(h, w1_ref[:, j * step:(j + 1) * step]), 0.0)
        acc = acc + _dot(_bf(u * u), w2_ref[j * step:(j + 1) * step, :])
    x2 = x + mod_ref[0, 5:6, :] * acc
    if final_norm:
        x2 = _rms(x2) * nf_ref[...]
    o_ref[0] = x2


def _merge_mlp(f, y0, y1, g, bonus, gates, x, mod, mod_row, lnw, lnb, n2, nf, eh, wf, wr, wo, w1, w2,
               final_norm):
    b, l, d = x.shape
    tm = min(256, l)
    row = lambda w: pl.BlockSpec((1, tm, w), lambda bi, i: (bi, i, 0))
    once = lambda a: pl.BlockSpec(a.shape, lambda bi, i: (0, 0), pipeline_mode=pl.Buffered(1))
    consts = [lnw, lnb, n2, nf, eh, wf, wr, wo, w1, w2]
    return pl.pallas_call(
        functools.partial(_merge_mlp_kernel, final_norm=final_norm),
        grid=(b, l // tm),
        in_specs=[row(F_WIDTH), row(R_WIDTH), row(R_WIDTH), row(R_WIDTH), row(R_WIDTH), row(GATE_W), row(d),
                  pl.BlockSpec((1, N_MOD, d), lambda bi, i: (mod_row(bi), 0, 0))]
                 + [once(c) for c in consts],
        out_specs=row(d),
        out_shape=jax.ShapeDtypeStruct((b, l, d), F32),
        compiler_params=_params("parallel", "parallel"),
        name="merge_mlp",
    )(f, y0, y1, g, bonus, gates, x, mod, *consts)


def _channel_dft():
    n = FGROUP_DIM
    jk = np.outer(np.arange(n), np.arange(n)) % n
    ang = 2.0 * np.pi * jk / n
    c = np.cos(ang) / np.sqrt(n)
    s = np.sin(ang) / np.sqrt(n)
    g = F_WIDTH // n
    out = np.zeros((F_WIDTH, 2 * F_WIDTH), np.float32)
    for i in range(g):
        out[i * n:(i + 1) * n, i * n:(i + 1) * n] = c
        out[i * n:(i + 1) * n, F_WIDTH + i * n:F_WIDTH + (i + 1) * n] = s
    return jnp.asarray(out)


def _position_dft(l):
    fold = (l // 2) % REV_BLOCK == 0
    nk = l // 2 if fold else l
    jk = np.outer(np.arange(l), np.arange(nk)) % l
    ang = 2.0 * np.pi * jk / l
    c = np.cos(ang) / np.sqrt(l)
    if fold:
        c[:, 0] *= 0.5
    cs = jnp.asarray(np.concatenate([c, -np.sin(ang) / np.sqrt(l)], axis=1).astype(np.float32)).astype(BF16)
    if not fold:
        return cs
    rev = np.zeros((2, REV_BLOCK, REV_BLOCK), np.float32)
    idx = np.arange(1, REV_BLOCK)
    rev[0, idx, REV_BLOCK - idx] = 1.0
    rev[1, 0, 0] = 1.0
    return cs, jnp.asarray(rev).astype(BF16)


def _head_ones():
    h = np.arange(R_WIDTH) // HEAD_DIM
    return jnp.asarray((h[:, None] == h[None, :]).astype(np.float32)).astype(BF16)


def _two_dir_lora(w):
    z = jnp.zeros_like(w[0])
    return _bf(jnp.concatenate([jnp.concatenate([w[0], z], axis=1),
                                jnp.concatenate([z, w[1]], axis=1)], axis=0))


def _layer(x, mod, mod_row, grid_mode, need_out, final_norm, s0, lw_, consts):
    rw_out = _inproj(x, mod, mod_row, lw_["n1"], lw_["win"], lw_["wz"], need_out)
    rw = rw_out[0]
    reff, yloc, m, gs, g, bonus = _scan_pass(rw, grid_mode, lw_["mu"], lw_["w0"], lw_["a0"], lw_["wup"],
                                             lw_["aup"], lw_["gup"], lw_["kkw"], lw_["ka"], lw_["rk"],
                                             consts["eh"])
    y0, y1, s_fin = _state_pass(m, gs, reff, yloc, s0)
    if not need_out:
        return None, s_fin
    z, gates = rw_out[1], rw_out[2]
    f = _pos_dft(z, consts["pos_dft"][x.shape[1]])
    x2 = _merge_mlp(f, y0, y1, g, bonus, gates, x, mod, mod_row, lw_["lnw"], lw_["lnb"], lw_["n2"],
                    consts["nf"], consts["eh"], lw_["wf"], lw_["wr"], lw_["wo"], lw_["w1"], lw_["w2"],
                    final_norm)
    return x2, s_fin


def kernel(x, c, ctx, c_ctx, w_mod, b_mod, norm1, norm2, w_in, mu_shift, w0, w_up, a0, a_up, g_up,
           k_k, k_a, r_k, ln_x_w, ln_x_b, w_fourier_up, w_rwkv_up, w_out, mlp_w1, mlp_w2, norm_f):
    depth = w_mod.shape[0]
    batch, seq, d = x.shape
    ctx_len = ctx.shape[1]
    assert d == D_MODEL and batch + 1 <= MOD_ROWS
    assert seq % GRID_W == 0 and ctx_len % CHUNK == 0 and seq % CHUNK == 0

    consts = {
        "eh": _head_ones(),
        "pos_dft": {n: _position_dft(n) for n in {seq, ctx_len}},
        "nf": norm_f.reshape(1, -1),
    }
    cdft = _channel_dft()
    cc = jnp.zeros((MOD_ROWS, d), F32).at[:batch].set(c).at[batch].set(c_ctx)
    row1 = lambda a: a.reshape(1, -1)
    lat_row = lambda bi: bi
    ctx_row = lambda bi: batch

    x_lat, x_ctx = x, ctx
    s_zero = jnp.zeros((2, batch, HEAD_DIM, R_WIDTH), F32)
    for l in range(depth):
        last = l == depth - 1
        mod = _modulation(cc, w_mod[l], b_mod[l]).reshape(MOD_ROWS, N_MOD, d)
        lw_ = {
            "n1": row1(norm1[l]),
            "wz": _fold_channel_dft(w_in[l][:, :F_WIDTH], cdft),
            "win": _bf(w_in[l]),
            "mu": row1(mu_shift[l]),
            "w0": row1(w0[l]), "a0": row1(a0[l]),
            "wup": _two_dir_lora(w_up[l]), "aup": _two_dir_lora(a_up[l]), "gup": _bf(g_up[l]),
            "kkw": row1(k_k[l]), "ka": row1(k_a[l]), "rk": row1(r_k[l]),
            "lnw": row1(ln_x_w[l]), "lnb": row1(ln_x_b[l]),
            "wf": _bf(w_fourier_up[l]), "wr": _bf(w_rwkv_up[l]), "wo": _bf(w_out[l]),
            "n2": row1(norm2[l]), "w1": _bf(mlp_w1[l]), "w2": _bf(mlp_w2[l]),
        }
        x_ctx, s_ctx = _layer(x_ctx, mod, ctx_row, False, not last, False, s_zero, lw_, consts)
        x_lat, _ = _layer(x_lat, mod, lat_row, True, True, last, s_ctx, lw_, consts)
    return x_lat
```

```python
import functools

import numpy as np
import jax
import jax.numpy as jnp
from jax import lax
from jax.experimental import pallas as pl
from jax.experimental.pallas import tpu as pltpu

F32 = jnp.float32
BF16 = jnp.bfloat16

D_MODEL = 1024
GRID_W = 64
F_WIDTH = 512
FGROUP_DIM = 128
HEAD_DIM = 64
N_RHEADS = 8
R_WIDTH = N_RHEADS * HEAD_DIM
D_LORA = 64
D_GATE_LORA = 128
RWKV_IN = 3 * R_WIDTH + 4 * D_LORA + D_GATE_LORA
GATE_W = 2 * D_MODEL
D_FF = 4 * D_MODEL
N_MOD = 6
NORM_EPS = 1e-6
GN_EPS = 64e-5
L2_EPS = 1e-12

CHUNK = 64
PREP_ROWS = 128
GROUP_LANES = 256
HEADS_PER_GROUP = GROUP_LANES // HEAD_DIM
LANE_TILE = 128
REV_BLOCK = 256
N_GROUPS = R_WIDTH // GROUP_LANES
MOD_ROWS = 16
VMEM_LIMIT = 56 * 1024 * 1024


def _bf(x):
    return x.astype(BF16)


def _dot(a, b):
    return jnp.dot(a, b, preferred_element_type=F32)


def _dot_nt(a, b):
    return lax.dot_general(a, b, (((1,), (1,)), ((), ())), preferred_element_type=F32)


def _split2(x):
    hi = _bf(x)
    lo = _bf(x - hi.astype(F32))
    return hi, lo


def _split3(x):
    hi = _bf(x)
    r1 = x - hi.astype(F32)
    mid = _bf(r1)
    lo = _bf(r1 - mid.astype(F32))
    return hi, mid, lo


def _dot_exact_rhs(x, w):
    hi, lo = _split2(x)
    return _dot(hi, w) + _dot(lo, w)


def _params(*sem):
    return pltpu.CompilerParams(dimension_semantics=sem, vmem_limit_bytes=VMEM_LIMIT)


def _const_spec(shape):
    zeros = (0,) * len(shape)
    return pl.BlockSpec(shape, lambda *_: zeros)


def _layer_spec(a, l, single_buffer=False):
    tail = tuple(a.shape[1:])
    idx = (l,) + (0,) * len(tail)
    kw = {"pipeline_mode": pl.Buffered(1)} if single_buffer else {}
    return pl.BlockSpec((None,) + tail, lambda *_: idx, **kw)


def _mod_spec(mod, l, mod_row):
    return pl.BlockSpec((None, 1) + tuple(mod.shape[2:]), lambda bi, i: (l, mod_row(bi), 0, 0))


def _modulation_kernel(x_ref, w_ref, b_ref, o_ref):
    x = x_ref[...]
    x = x * jax.nn.sigmoid(x)
    o_ref[0] = _dot(_bf(x), _bf(w_ref[0])) + b_ref[0]


def _modulation(cc, w, b):
    depth, d, n = w.shape
    tn = 512
    return pl.pallas_call(
        _modulation_kernel,
        grid=(depth, n // tn),
        in_specs=[_const_spec(cc.shape),
                  pl.BlockSpec((1, d, tn), lambda l, j: (l, 0, j)),
                  pl.BlockSpec((1, 1, tn), lambda l, j: (l, 0, j))],
        out_specs=pl.BlockSpec((1, cc.shape[0], tn), lambda l, j: (l, 0, j)),
        out_shape=jax.ShapeDtypeStruct((depth, cc.shape[0], n), F32),
        compiler_params=_params("parallel", "parallel"),
        name="modulation",
    )(cc, w, b.reshape(depth, 1, n))


def _fold_kernel(w_ref, c_ref, o_ref):
    wh, wl = _split2(w_ref[...])
    ch, cl = _split2(c_ref[...])
    o_ref[...] = _bf(_dot(wh, ch) + _dot(wl, ch) + _dot(wh, cl))


def _fold_channel_dft(w_in, l, cdft):
    d = w_in.shape[1]
    return pl.pallas_call(
        _fold_kernel,
        grid=(1,),
        in_specs=[pl.BlockSpec((None, d, F_WIDTH), lambda i: (l, 0, 0)), _const_spec(cdft.shape)],
        out_specs=_const_spec((d, cdft.shape[1])),
        out_shape=jax.ShapeDtypeStruct((d, cdft.shape[1]), BF16),
        compiler_params=_params("arbitrary"),
        name="fold_channel_dft",
    )(w_in, cdft)


def _rms(x):
    return x * lax.rsqrt(jnp.mean(x * x, axis=-1, keepdims=True) + NORM_EPS)


def _inproj_kernel(x_ref, mod_ref, n1_ref, win_ref, *rest, need_out):
    if need_out:
        wz_ref, rw_ref, z_ref, g_ref = rest
    else:
        (rw_ref,) = rest
    x = x_ref[0]
    shift = mod_ref[0, 0:1, :]
    scale = mod_ref[0, 1:2, :]
    h = _bf(_rms(x) * n1_ref[...] * (1.0 + scale) + shift)
    rw_ref[0] = _dot(h, win_ref[:, F_WIDTH:F_WIDTH + RWKV_IN])
    if need_out:
        z = _dot(h, wz_ref[...])
        z_ref[0, 0] = _bf(z[:, :F_WIDTH])
        z_ref[0, 1] = _bf(z[:, F_WIDTH:])
        g_ref[0] = _bf(_dot(h, win_ref[:, F_WIDTH + RWKV_IN:]))


def _inproj(x, mod, mod_row, layer, n1, win, wz, need_out):
    b, l, d = x.shape
    tm = min(256, l)
    row_spec = lambda w: pl.BlockSpec((1, tm, w), lambda bi, i: (bi, i, 0))
    in_specs = [row_spec(d), _mod_spec(mod, layer, mod_row), _layer_spec(n1, layer), _layer_spec(win, layer)]
    args = [x, mod, n1, win]
    out_specs = [row_spec(RWKV_IN)]
    out_shape = [jax.ShapeDtypeStruct((b, l, RWKV_IN), F32)]
    if need_out:
        in_specs += [_const_spec(wz.shape)]
        args += [wz]
        out_specs += [pl.BlockSpec((1, 2, tm, F_WIDTH), lambda bi, i: (bi, 0, i, 0)), row_spec(GATE_W)]
        out_shape += [jax.ShapeDtypeStruct((b, 2, l, F_WIDTH), BF16),
                      jax.ShapeDtypeStruct((b, l, GATE_W), BF16)]
    return pl.pallas_call(
        functools.partial(_inproj_kernel, need_out=need_out),
        grid=(b, l // tm),
        in_specs=in_specs, out_specs=out_specs, out_shape=out_shape,
        compiler_params=_params("parallel", "parallel"),
        name="inproj",
    )(*args)


def _log_sigmoid(x):
    return jnp.minimum(x, 0.0) - jnp.log1p(jnp.exp(-jnp.abs(x)))


P_R, P_V, P_KK, P_KD, P_LW, P_B = 0, 1, 2, 3, 5, 7
N_PREP = 9


def _scan_kernel(*refs, grid_mode, seq_len, tm):
    reff_ref, yloc_ref, m_ref, gs_ref, g_ref, bonus_ref, pp_ref = refs[-7:]
    _prep_tile(refs[:-7], pp_ref, g_ref, bonus_ref, grid_mode=grid_mode, seq_len=seq_len, tm=tm)
    _chunk_algebra(pp_ref, reff_ref, yloc_ref, m_ref, gs_ref, tc=tm)


def _prep_tile(in_refs, pp_ref, g_ref, bonus_ref, *, grid_mode, seq_len, tm):
    sb = min(PREP_ROWS, tm) if grid_mode else tm
    for r0 in range(0, tm, sb):
        _prep_rows(r0, sb, in_refs, pp_ref, g_ref, bonus_ref, grid_mode=grid_mode, seq_len=seq_len, tm=tm)


def _prep_rows(r0, sb, in_refs, pp_ref, g_ref, bonus_ref, *, grid_mode, seq_len, tm):
    if grid_mode:
        (rw_ref, prev_ref, next_ref, mu_ref, w0_ref, a0_ref, wup_ref, aup_ref, gup_ref,
         kkw_ref, ka_ref, rk_ref, eh_ref) = in_refs
    else:
        (rw_ref, mu_ref, w0_ref, a0_ref, wup_ref, aup_ref, gup_ref,
         kkw_ref, ka_ref, rk_ref, eh_ref) = in_refs
    i = pl.program_id(1)
    rows = slice(r0, r0 + sb)
    t_loc = lax.broadcasted_iota(jnp.int32, (sb, 128), 0) + r0
    t_glob = t_loc + i * tm
    lane = lax.broadcasted_iota(jnp.int32, (sb, 128), 1)
    if grid_mode:
        col = jnp.bitwise_and(t_loc, GRID_W - 1)
        masks = [col != 0, col != GRID_W - 1, t_glob >= GRID_W, t_glob < seq_len - GRID_W]
        n_parts = 4
    else:
        masks = [t_glob != 0, t_glob != seq_len - 1]
        n_parts = 2
    part_w = RWKV_IN // n_parts

    def shifted(j, part):
        cols = slice(128 * j, 128 * (j + 1))
        if part == 0:
            s = pltpu.roll(rw_ref[0, rows, cols], 1, 0)
        elif part == 1:
            s = pltpu.roll(rw_ref[0, rows, cols], sb - 1, 0)
        elif part == 2:
            if r0 > 0:
                s = rw_ref[0, r0 - GRID_W:r0 + sb - GRID_W, cols]
            elif sb == GRID_W:
                s = prev_ref[0, :, cols]
            else:
                s = jnp.concatenate([prev_ref[0, :, cols], rw_ref[0, 0:sb - GRID_W, cols]], axis=0)
        else:
            if r0 + sb < tm:
                s = rw_ref[0, r0 + GRID_W:r0 + sb + GRID_W, cols]
            elif sb == GRID_W:
                s = next_ref[0, :, cols]
            else:
                s = jnp.concatenate([rw_ref[0, r0 + GRID_W:tm, cols], next_ref[0, :, cols]], axis=0)
        return jnp.where(masks[part], s, 0.0)

    blocks = []
    for j in range(RWKV_IN // 128):
        p_lo = (128 * j) // part_w
        p_hi = (128 * j + 127) // part_w
        s = shifted(j, p_lo)
        if p_hi != p_lo:
            s = jnp.where(lane + 128 * j < part_w * p_hi, s, shifted(j, p_hi))
        xj = rw_ref[0, rows, 128 * j:128 * (j + 1)]
        blocks.append(xj + mu_ref[:, 128 * j:128 * (j + 1)] * (s - xj))

    nb = R_WIDTH // 128
    r = jnp.concatenate(blocks[0:nb], axis=1)
    k = jnp.concatenate(blocks[nb:2 * nb], axis=1)
    v = jnp.concatenate(blocks[2 * nb:3 * nb], axis=1)
    wd, ad, gd = blocks[3 * nb], blocks[3 * nb + 1], blocks[3 * nb + 2]

    w_logit = w0_ref[...] + _dot(_bf(jnp.tanh(wd)), wup_ref[...])
    lw = -jnp.exp(_log_sigmoid(w_logit) - 0.5)
    a = jax.nn.sigmoid(a0_ref[...] + _dot(_bf(ad), aup_ref[...]))
    g = _dot(_bf(jax.nn.sigmoid(gd)), gup_ref[...])

    eh = eh_ref[...]
    kx = k * kkw_ref[...]
    nrm = jnp.maximum(jnp.sqrt(_dot_exact_rhs(kx * kx, eh)), L2_EPS)
    kk = kx / nrm
    ka = ka_ref[...]
    kd0 = k * (1.0 + (a[:, :R_WIDTH] - 1.0) * ka)
    kd1 = k * (1.0 + (a[:, R_WIDTH:] - 1.0) * ka)
    bonus = _dot_exact_rhs(r * (kd0 + kd1) * rk_ref[...], eh) * v

    pp_ref[P_R, rows] = r
    pp_ref[P_V, rows] = v
    pp_ref[P_KK, rows] = kk
    pp_ref[P_KD, rows] = kd0
    pp_ref[P_KD + 1, rows] = kd1
    pp_ref[P_LW, rows] = lw[:, :R_WIDTH]
    pp_ref[P_LW + 1, rows] = lw[:, R_WIDTH:]
    pp_ref[P_B, rows] = kk * a[:, :R_WIDTH]
    pp_ref[P_B + 1, rows] = kk * a[:, R_WIDTH:]
    g_ref[0, rows] = _bf(g)
    bonus_ref[0, rows] = _bf(bonus)


def _scan_pass(rw, grid_mode, layer, mu, w0c, a0c, wup, aup, gup, kkw, ka, rk, eh):
    b, l, _ = rw.shape
    tm = min(256, l) if grid_mode else l
    nrow = l // GRID_W
    per = tm // GRID_W
    row_spec = pl.BlockSpec((1, tm, RWKV_IN), lambda bi, i: (bi, i, 0))
    in_specs = [row_spec]
    args = [rw]
    if grid_mode:
        in_specs += [
            pl.BlockSpec((1, GRID_W, RWKV_IN), lambda bi, i: (bi, jnp.maximum(i * per - 1, 0), 0)),
            pl.BlockSpec((1, GRID_W, RWKV_IN), lambda bi, i: (bi, jnp.minimum((i + 1) * per, nrow - 1), 0)),
        ]
        args += [rw, rw]
    stacked = [mu, w0c, a0c, wup, aup, gup, kkw, ka, rk]
    in_specs += [_layer_spec(c, layer) for c in stacked] + [_const_spec(eh.shape)]
    args += stacked + [eh]
    one = pl.BlockSpec((1, tm, R_WIDTH), lambda bi, i: (bi, i, 0))
    two = pl.BlockSpec((2, 1, tm, R_WIDTH), lambda bi, i: (0, bi, i, 0))
    s1 = jax.ShapeDtypeStruct((b, l, R_WIDTH), BF16)
    s2 = jax.ShapeDtypeStruct((2, b, l, R_WIDTH), F32)
    s2h = jax.ShapeDtypeStruct((2, b, l, R_WIDTH), BF16)
    return pl.pallas_call(
        functools.partial(_scan_kernel, grid_mode=grid_mode, seq_len=l, tm=tm),
        grid=(b, l // tm),
        in_specs=in_specs,
        out_specs=[two, two, two, two, one, one],
        out_shape=[s2h, s2h, s2, s2, s1, s1],
        scratch_shapes=[pltpu.VMEM((N_PREP, tm, R_WIDTH), F32)],
        compiler_params=_params("parallel", "parallel"),
        name="rwkv_scan",
    )(*args)


def _block_diag_mask():
    return lax.broadcasted_iota(jnp.int32, (CHUNK, LANE_TILE), 1) < HEAD_DIM


def _block_diag(x, first_head):
    zero = jnp.zeros((CHUNK, LANE_TILE), x.dtype)
    per_tile = LANE_TILE // HEAD_DIM
    blocks = []
    for h in range(HEADS_PER_GROUP):
        tile = h // per_tile
        piece = x[:, tile * LANE_TILE:(tile + 1) * LANE_TILE]
        piece = jnp.where(first_head, piece, zero) if h % per_tile == 0 else jnp.where(first_head, zero, piece)
        row = [zero] * (GROUP_LANES // LANE_TILE)
        row[tile] = piece
        blocks.append(jnp.concatenate(row, axis=1))
    return jnp.concatenate(blocks, axis=0)


def _chunk_algebra(pp_ref, reff_ref, yloc_ref, m_ref, gs_ref, *, tc):
    t = lax.broadcasted_iota(jnp.int32, (CHUNK, GROUP_LANES), 0)
    lane = lax.broadcasted_iota(jnp.int32, (CHUNK, GROUP_LANES), 1)
    s = jnp.bitwise_and(lane, CHUNK - 1)
    eye = jnp.where(s == t, 1.0, 0.0)
    before = (s < t, s > t)
    upto = (s <= t, s >= t)
    bdmask = _block_diag_mask()
    t64 = lax.broadcasted_iota(jnp.int32, (CHUNK, CHUNK), 0)
    s64 = lax.broadcasted_iota(jnp.int32, (CHUNK, CHUNK), 1)
    tri = (jnp.where(s64 <= t64, 1.0, 0.0).astype(BF16), jnp.where(s64 >= t64, 1.0, 0.0).astype(BF16))

    bdiag = lambda x: _block_diag(_bf(x), bdmask)
    stack = lambda a, b: _bf(jnp.concatenate([a, b], axis=0))
    units = [(d, slice(ci * CHUNK, (ci + 1) * CHUNK), slice(gi * GROUP_LANES, (gi + 1) * GROUP_LANES))
             for d in range(2) for ci in range(tc // CHUNK) for gi in range(N_GROUPS)]
    nu = range(len(units))
    dirs = [d for d, _, _ in units]

    cum_chunk = {}
    for d in range(2):
        for ci in range(tc // CHUNK):
            h3 = _split3(pp_ref[P_LW + d, ci * CHUNK:(ci + 1) * CHUNK, :])
            cum_chunk[d, ci] = _dot(tri[d], h3[0]) + _dot(tri[d], h3[1]) + _dot(tri[d], h3[2])
    kt, rt, kh, bh, vv, dec = [], [], [], [], [], []
    for u, (d, rows, lanes) in enumerate(units):
        lw = pp_ref[P_LW + d, rows, lanes]
        cum = cum_chunk[d, rows.start // CHUNK][:, lanes]
        e_neg = jnp.exp(-cum)
        kt.append(pp_ref[P_KK, rows, lanes] * jnp.exp(cum - lw))
        rt.append(pp_ref[P_R, rows, lanes] * jnp.exp(cum))
        kh.append(pp_ref[P_KD + d, rows, lanes] * e_neg)
        bh.append(pp_ref[P_B + d, rows, lanes] * e_neg)
        vv.append(pp_ref[P_V, rows, lanes])
        dec.append(eye * jnp.exp(jnp.sum(lw, axis=0, keepdims=True)))

    lhs = []
    for u in nu:
        dh, dl = _split2(dec[u])
        lhs.append(jnp.concatenate([_bf(kt[u]), _bf(rt[u]), dh, dl], axis=0))
    sk = [_dot_nt(lhs[u], bdiag(kh[u])) for u in nu]
    sb = [_dot_nt(lhs[u], bdiag(bh[u])) for u in nu]
    a_k = [jnp.where(before[dirs[u]], sk[u][:CHUNK], 0.0) for u in nu]
    a_rk = [jnp.where(upto[dirs[u]], sk[u][CHUNK:2 * CHUNK], 0.0) for u in nu]
    kb_t = [sk[u][2 * CHUNK:3 * CHUNK] + sk[u][3 * CHUNK:] for u in nu]
    n = [jnp.where(before[dirs[u]], -sb[u][:CHUNK], 0.0) for u in nu]
    a_rb = [jnp.where(upto[dirs[u]], sb[u][CHUNK:2 * CHUNK], 0.0) for u in nu]
    bb_t = [sb[u][2 * CHUNK:3 * CHUNK] + sb[u][3 * CHUNK:] for u in nu]

    tinv = [eye + n[u] for u in nu]
    p = [_dot(_bf(n[u]), bdiag(n[u])) for u in nu]
    av = [_dot(_bf(jnp.concatenate([a_k[u], a_rk[u], kb_t[u]], axis=0)), bdiag(vv[u])) for u in nu]
    for _ in range(4):
        tp = [_dot(stack(tinv[u], p[u]), bdiag(p[u])) for u in nu]
        tinv = [tinv[u] + tp[u][:CHUNK] for u in nu]
        p = [tp[u][CHUNK:] for u in nu]
    tb = [_bf(tinv[u] + _dot(_bf(tinv[u]), bdiag(p[u]))) for u in nu]

    uk = [_dot(tb[u], bdiag(kt[u])) for u in nu]
    uv = [_dot(tb[u], bdiag(av[u][:CHUNK])) for u in nu]
    rows_b = [stack(a_rb[u], bb_t[u]) for u in nu]
    bk = [_dot(rows_b[u], bdiag(uk[u])) for u in nu]
    bv = [_dot(rows_b[u], bdiag(uv[u])) for u in nu]
    for u, (d, rows, lanes) in enumerate(units):
        reff_ref[d, 0, rows, lanes] = _bf(rt[u] - bk[u][:CHUNK])
        yloc_ref[d, 0, rows, lanes] = _bf(av[u][CHUNK:2 * CHUNK] - bv[u][:CHUNK])
        m_ref[d, 0, rows, lanes] = dec[u] - bk[u][CHUNK:]
        gs_ref[d, 0, rows, lanes] = av[u][2 * CHUNK:] - bv[u][CHUNK:]


def _state_kernel(m0_ref, g0_ref, re0_ref, yl0_ref, m1_ref, g1_ref, re1_ref, yl1_ref, s0_ref,
                  y0_ref, y1_ref, h_ref, *, batch):
    c = pl.program_id(0)

    @pl.when(c == 0)
    def _():
        h_ref[...] = s0_ref[...]

    bdmask = _block_diag_mask()
    per_dir = ((m0_ref, g0_ref, re0_ref, yl0_ref, y0_ref), (m1_ref, g1_ref, re1_ref, yl1_ref, y1_ref))
    units = [(d, bi, slice(gi * GROUP_LANES, (gi + 1) * GROUP_LANES))
             for d in range(2) for bi in range(batch) for gi in range(N_GROUPS)]
    hs = [_split2(h_ref[d, bi, :, lanes]) for d, bi, lanes in units]
    ms = [_split2(per_dir[d][0][0, bi, :, lanes]) for d, bi, lanes in units]
    rs = [per_dir[d][2][0, bi, :, lanes] for d, bi, lanes in units]
    o1 = [_dot(jnp.concatenate([ms[u][0], ms[u][1], rs[u]], axis=0), _block_diag(hs[u][0], bdmask))
          for u in range(len(units))]
    o2 = [_dot(jnp.concatenate([ms[u][0], rs[u]], axis=0), _block_diag(hs[u][1], bdmask))
          for u in range(len(units))]
    for u, (d, bi, lanes) in enumerate(units):
        _, g_ref, _, yl_ref, y_ref = per_dir[d]
        mh_new = o1[u][0:CHUNK] + o1[u][CHUNK:2 * CHUNK] + o2[u][0:CHUNK]
        rh_new = o1[u][2 * CHUNK:] + o2[u][CHUNK:]
        y_ref[bi, :, lanes] = _bf(yl_ref[0, bi, :, lanes].astype(F32) + rh_new)
        h_ref[d, bi, :, lanes] = mh_new + g_ref[0, bi, :, lanes]


def _state_pass(m, g, reff, yloc, s0):
    _, b, l, _ = m.shape
    nc = l // CHUNK
    blk = (1, b, CHUNK, R_WIDTH)
    fwd = pl.BlockSpec(blk, lambda c: (0, 0, c, 0))
    bwd = pl.BlockSpec(blk, lambda c: (1, 0, nc - 1 - c, 0))
    st = pl.BlockSpec((2, b, HEAD_DIM, R_WIDTH), lambda c: (0, 0, 0, 0))
    sy = jax.ShapeDtypeStruct((b, l, R_WIDTH), BF16)
    return pl.pallas_call(
        functools.partial(_state_kernel, batch=b),
        grid=(nc,),
        in_specs=[fwd, fwd, fwd, fwd, bwd, bwd, bwd, bwd, st],
        out_specs=[pl.BlockSpec((b, CHUNK, R_WIDTH), lambda c: (0, c, 0)),
                   pl.BlockSpec((b, CHUNK, R_WIDTH), lambda c: (0, nc - 1 - c, 0)),
                   st],
        out_shape=[sy, sy, jax.ShapeDtypeStruct((2, b, HEAD_DIM, R_WIDTH), F32)],
        compiler_params=_params("arbitrary"),
        name="rwkv_state",
    )(m, g, reff, yloc, m, g, reff, yloc, s0)


def _posdft_kernel(cs_ref, z_ref, o_ref):
    l = z_ref.shape[2]
    z = z_ref[0].reshape(2 * l, F_WIDTH)
    o_ref[0] = _bf(_dot(cs_ref[...], z))


def _posdft_fold_kernel(w_ref, rev_ref, z_ref, o_ref, f_ref, *, tm):
    l = z_ref.shape[2]
    half = l // 2
    nb = l // REV_BLOCK
    i = pl.program_id(1)

    @pl.when(i == 0)
    def _():
        for part in range(2):
            for j in range(half // REV_BLOCK):
                blk = z_ref[0, part, j * REV_BLOCK:(j + 1) * REV_BLOCK, :]
                src_a = z_ref[0, part, l - (j + 1) * REV_BLOCK:l - j * REV_BLOCK, :]
                jb = (nb - j) % nb
                src_b = z_ref[0, part, jb * REV_BLOCK:(jb + 1) * REV_BLOCK, :]
                rev = _dot(rev_ref[0], src_a) + _dot(rev_ref[1], src_b)
                sgn = 1.0 if part == 0 else -1.0
                f_ref[part * half + j * REV_BLOCK:part * half + (j + 1) * REV_BLOCK, :] = _bf(
                    blk.astype(F32) + sgn * rev)

    row = lax.broadcasted_iota(jnp.int32, (tm, F_WIDTH), 0) + i * tm
    alt = (1 - 2 * jnp.bitwise_and(row, 1)).astype(F32) * (1.0 / float(np.sqrt(l)))
    mid = z_ref[0, 0, half:half + 1, :].astype(F32)
    o_ref[0] = _bf(_dot(w_ref[...], f_ref[...]) + alt * mid)


def _pos_dft(z, cs):
    b, _, l, _ = z.shape
    tm = min(512, l)
    if isinstance(cs, tuple):
        w, rev = cs
        return pl.pallas_call(
            functools.partial(_posdft_fold_kernel, tm=tm),
            grid=(b, l // tm),
            in_specs=[pl.BlockSpec((tm, l), lambda bi, i: (i, 0)),
                      _const_spec(rev.shape),
                      pl.BlockSpec((1, 2, l, F_WIDTH), lambda bi, i: (bi, 0, 0, 0))],
            out_specs=pl.BlockSpec((1, tm, F_WIDTH), lambda bi, i: (bi, i, 0)),
            out_shape=jax.ShapeDtypeStruct((b, l, F_WIDTH), BF16),
            scratch_shapes=[pltpu.VMEM((l, F_WIDTH), BF16)],
            compiler_params=_params("parallel", "arbitrary"),
            name="pos_dft_fold",
        )(w, rev, z)
    return pl.pallas_call(
        _posdft_kernel,
        grid=(b, l // tm),
        in_specs=[pl.BlockSpec((tm, 2 * l), lambda bi, i: (i, 0)),
                  pl.BlockSpec((1, 2, l, F_WIDTH), lambda bi, i: (bi, 0, 0, 0))],
        out_specs=pl.BlockSpec((1, tm, F_WIDTH), lambda bi, i: (bi, i, 0)),
        out_shape=jax.ShapeDtypeStruct((b, l, F_WIDTH), BF16),
        compiler_params=_params("parallel", "parallel"),
        name="pos_dft",
    )(cs, z)


def _merge_mlp_kernel(f_ref, y0_ref, y1_ref, g_ref, bonus_ref, gates_ref, x_ref, mod_ref,
                      lnw_ref, lnb_ref, n2_ref, nf_ref, eh_ref, wf_ref, wr_ref, wo_ref, w1_ref, w2_ref,
                      o_ref, *, final_norm):
    y = y0_ref[0].astype(F32) + y1_ref[0].astype(F32)
    eh = eh_ref[...]
    inv_n = 1.0 / HEAD_DIM
    mu = _dot_exact_rhs(y, eh) * inv_n
    dlt = y - mu
    var = _dot_exact_rhs(dlt * dlt, eh) * inv_n
    yn = dlt * lax.rsqrt(var + GN_EPS) * lnw_ref[...] + lnb_ref[...]
    rwkv = _bf((yn + bonus_ref[0].astype(F32)) * g_ref[0].astype(F32))
    fo = _dot(f_ref[0], wf_ref[...])
    ro = _dot(rwkv, wr_ref[...])
    gates = gates_ref[0].astype(F32)
    merged = jax.nn.sigmoid(gates[:, :D_MODEL]) * fo + jax.nn.sigmoid(gates[:, D_MODEL:]) * ro
    x = x_ref[0] + mod_ref[0, 2:3, :] * _dot(_bf(merged), wo_ref[...])
    h = _bf(_rms(x) * n2_ref[...] * (1.0 + mod_ref[0, 4:5, :]) + mod_ref[0, 3:4, :])
    acc = jnp.zeros(x.shape, F32)
    step = 1024
    for j in range(D_FF // step):
        u = jnp.maximum(_dot(h, w1_ref[:, j * step:(j + 1) * step]), 0.0)
        acc = acc + _dot(_bf(u * u), w2_ref[j * step:(j + 1) * step, :])
    x2 = x + mod_ref[0, 5:6, :] * acc
    if final_norm:
        x2 = _rms(x2) * nf_ref[...]
    o_ref[0] = x2


def _merge_mlp(f, y0, y1, g, bonus, gates, x, mod, mod_row, layer, lnw, lnb, n2, nf, eh, wf, wr, wo, w1, w2,
               final_norm):
    b, l, d = x.shape
    tm = min(256, l)
    row = lambda w: pl.BlockSpec((1, tm, w), lambda bi, i: (bi, i, 0))
    once = lambda a: pl.BlockSpec(a.shape, lambda bi, i: (0, 0), pipeline_mode=pl.Buffered(1))
    stacked = lambda a: _layer_spec(a, layer, single_buffer=True)
    return pl.pallas_call(
        functools.partial(_merge_mlp_kernel, final_norm=final_norm),
        grid=(b, l // tm),
        in_specs=[row(F_WIDTH), row(R_WIDTH), row(R_WIDTH), row(R_WIDTH), row(R_WIDTH), row(GATE_W), row(d),
                  _mod_spec(mod, layer, mod_row),
                  stacked(lnw), stacked(lnb), stacked(n2), once(nf), once(eh),
                  stacked(wf), stacked(wr), stacked(wo), stacked(w1), stacked(w2)],
        out_specs=row(d),
        out_shape=jax.ShapeDtypeStruct((b, l, d), F32),
        compiler_params=_params("parallel", "parallel"),
        name="merge_mlp",
    )(f, y0, y1, g, bonus, gates, x, mod, lnw, lnb, n2, nf, eh, wf, wr, wo, w1, w2)


def _channel_dft():
    n = FGROUP_DIM
    jk = np.outer(np.arange(n), np.arange(n)) % n
    ang = 2.0 * np.pi * jk / n
    c = np.cos(ang) / np.sqrt(n)
    s = np.sin(ang) / np.sqrt(n)
    g = F_WIDTH // n
    out = np.zeros((F_WIDTH, 2 * F_WIDTH), np.float32)
    for i in range(g):
        out[i * n:(i + 1) * n, i * n:(i + 1) * n] = c
        out[i * n:(i + 1) * n, F_WIDTH + i * n:F_WIDTH + (i + 1) * n] = s
    return jnp.asarray(out)


def _position_dft(l):
    fold = (l // 2) % REV_BLOCK == 0
    nk = l // 2 if fold else l
    jk = np.outer(np.arange(l), np.arange(nk)) % l
    ang = 2.0 * np.pi * jk / l
    c = np.cos(ang) / np.sqrt(l)
    if fold:
        c[:, 0] *= 0.5
    cs = jnp.asarray(np.concatenate([c, -np.sin(ang) / np.sqrt(l)], axis=1).astype(np.float32)).astype(BF16)
    if not fold:
        return cs
    rev = np.zeros((2, REV_BLOCK, REV_BLOCK), np.float32)
    idx = np.arange(1, REV_BLOCK)
    rev[0, idx, REV_BLOCK - idx] = 1.0
    rev[1, 0, 0] = 1.0
    return cs, jnp.asarray(rev).astype(BF16)


def _head_ones():
    h = np.arange(R_WIDTH) // HEAD_DIM
    return jnp.asarray((h[:, None] == h[None, :]).astype(np.float32)).astype(BF16)


def _two_dir_lora(w):
    z = jnp.zeros_like(w[:, 0])
    return _bf(jnp.concatenate([jnp.concatenate([w[:, 0], z], axis=2),
                                jnp.concatenate([z, w[:, 1]], axis=2)], axis=1))


def _layer(x, mod_row, layer, grid_mode, need_out, final_norm, s0, p):
    rw_out = _inproj(x, p["mod"], mod_row, layer, p["n1"], p["win"], p["wz"][layer], need_out)
    reff, yloc, m, gs, g, bonus = _scan_pass(rw_out[0], grid_mode, layer, p["mu"], p["w0"], p["a0"], p["wup"],
                                             p["aup"], p["gup"], p["kkw"], p["ka"], p["rk"], p["eh"])
    y0, y1, s_fin = _state_pass(m, gs, reff, yloc, s0)
    if not need_out:
        return None, s_fin
    z, gates = rw_out[1], rw_out[2]
    f = _pos_dft(z, p["pos_dft"][x.shape[1]])
    x2 = _merge_mlp(f, y0, y1, g, bonus, gates, x, p["mod"], mod_row, layer, p["lnw"], p["lnb"], p["n2"],
                    p["nf"], p["eh"], p["wf"], p["wr"], p["wo"], p["w1"], p["w2"], final_norm)
    return x2, s_fin


def kernel(x, c, ctx, c_ctx, w_mod, b_mod, norm1, norm2, w_in, mu_shift, w0, w_up, a0, a_up, g_up,
           k_k, k_a, r_k, ln_x_w, ln_x_b, w_fourier_up, w_rwkv_up, w_out, mlp_w1, mlp_w2, norm_f):
    depth = w_mod.shape[0]
    batch, seq, d = x.shape
    ctx_len = ctx.shape[1]
    assert d == D_MODEL and batch + 1 <= MOD_ROWS
    assert seq % GRID_W == 0 and ctx_len % CHUNK == 0 and seq % CHUNK == 0

    cc = jnp.zeros((MOD_ROWS, d), F32).at[:batch].set(c).at[batch].set(c_ctx)
    cdft = _channel_dft()
    rows = lambda a: a.reshape(depth, 1, -1)
    p = {
        "eh": _head_ones(),
        "pos_dft": {n: _position_dft(n) for n in {seq, ctx_len}},
        "nf": norm_f.reshape(1, -1),
        "mod": _modulation(cc, w_mod, b_mod).reshape(depth, MOD_ROWS, N_MOD, d),
        "wz": [_fold_channel_dft(w_in, layer, cdft) for layer in range(depth)],
        "win": _bf(w_in),
        "n1": rows(norm1), "n2": rows(norm2), "mu": rows(mu_shift), "w0": rows(w0), "a0": rows(a0),
        "wup": _two_dir_lora(w_up), "aup": _two_dir_lora(a_up), "gup": _bf(g_up),
        "kkw": rows(k_k), "ka": rows(k_a), "rk": rows(r_k), "lnw": rows(ln_x_w), "lnb": rows(ln_x_b),
        "wf": _bf(w_fourier_up), "wr": _bf(w_rwkv_up), "wo": _bf(w_out),
        "w1": _bf(mlp_w1), "w2": _bf(mlp_w2),
    }
    lat_row = lambda bi: bi
    ctx_row = lambda bi: batch

    x_lat, x_ctx = x, ctx
    s_zero = jnp.zeros((2, batch, HEAD_DIM, R_WIDTH), F32)
    for layer in range(depth):
        last = layer == depth - 1
        x_ctx, s_ctx = _layer(x_ctx, ctx_row, layer, False, not last, False, s_zero, p)
        x_lat, _ = _layer(x_lat, lat_row, layer, True, True, last, s_ctx, p)
    return x_lat
```

```python
import functools

import numpy as np
import jax
import jax.numpy as jnp
from jax import lax
from jax.experimental import pallas as pl
from jax.experimental.pallas import tpu as pltpu

F32 = jnp.float32
BF16 = jnp.bfloat16

D_MODEL = 1024
GRID_W = 64
F_WIDTH = 512
FGROUP_DIM = 128
HEAD_DIM = 64
N_RHEADS = 8
R_WIDTH = N_RHEADS * HEAD_DIM
D_LORA = 64
D_GATE_LORA = 128
RWKV_IN = 3 * R_WIDTH + 4 * D_LORA + D_GATE_LORA
GATE_W = 2 * D_MODEL
D_FF = 4 * D_MODEL
N_MOD = 6
NORM_EPS = 1e-6
GN_EPS = 64e-5
L2_EPS = 1e-12

CHUNK = 64
PREP_ROWS = 128
GROUP_LANES = 256
HEADS_PER_GROUP = GROUP_LANES // HEAD_DIM
LANE_TILE = 128
REV_BLOCK = 256
N_GROUPS = R_WIDTH // GROUP_LANES
MOD_ROWS = 16
VMEM_LIMIT = 56 * 1024 * 1024


def _bf(x):
    return x.astype(BF16)


def _dot(a, b):
    return jnp.dot(a, b, preferred_element_type=F32)


def _dot_nt(a, b):
    return lax.dot_general(a, b, (((1,), (1,)), ((), ())), preferred_element_type=F32)


def _split2(x):
    hi = _bf(x)
    lo = _bf(x - hi.astype(F32))
    return hi, lo


def _head_sum(x, ones):
    return _dot(_bf(x), ones)


def _params(*sem):
    return pltpu.CompilerParams(dimension_semantics=sem, vmem_limit_bytes=VMEM_LIMIT)


def _const_spec(shape):
    zeros = (0,) * len(shape)
    return pl.BlockSpec(shape, lambda *_: zeros)


def _layer_spec(a, l, single_buffer=False):
    tail = tuple(a.shape[1:])
    idx = (l,) + (0,) * len(tail)
    kw = {"pipeline_mode": pl.Buffered(1)} if single_buffer else {}
    return pl.BlockSpec((None,) + tail, lambda *_: idx, **kw)


def _mod_spec(mod, l, mod_row):
    return pl.BlockSpec((None, 1) + tuple(mod.shape[2:]), lambda bi, i: (l, mod_row(bi), 0, 0))


def _modulation_kernel(x_ref, w_ref, b_ref, o_ref):
    x = x_ref[...]
    x = x * jax.nn.sigmoid(x)
    o_ref[0] = _dot(_bf(x), _bf(w_ref[0])) + b_ref[0]


def _modulation(cc, w, b):
    depth, d, n = w.shape
    tn = 512
    return pl.pallas_call(
        _modulation_kernel,
        grid=(depth, n // tn),
        in_specs=[_const_spec(cc.shape),
                  pl.BlockSpec((1, d, tn), lambda l, j: (l, 0, j)),
                  pl.BlockSpec((1, 1, tn), lambda l, j: (l, 0, j))],
        out_specs=pl.BlockSpec((1, cc.shape[0], tn), lambda l, j: (l, 0, j)),
        out_shape=jax.ShapeDtypeStruct((depth, cc.shape[0], n), F32),
        compiler_params=_params("parallel", "parallel"),
        name="modulation",
    )(cc, w, b.reshape(depth, 1, n))


def _fold_kernel(w_ref, c_ref, o_ref):
    wh, wl = _split2(w_ref[...])
    ch, cl = _split2(c_ref[...])
    o_ref[...] = _bf(_dot(wh, ch) + _dot(wl, ch) + _dot(wh, cl))


def _fold_channel_dft(w_in, l, cdft):
    d = w_in.shape[1]
    return pl.pallas_call(
        _fold_kernel,
        grid=(1,),
        in_specs=[pl.BlockSpec((None, d, F_WIDTH), lambda i: (l, 0, 0)), _const_spec(cdft.shape)],
        out_specs=_const_spec((d, cdft.shape[1])),
        out_shape=jax.ShapeDtypeStruct((d, cdft.shape[1]), BF16),
        compiler_params=_params("arbitrary"),
        name="fold_channel_dft",
    )(w_in, cdft)


def _rms(x):
    return x * lax.rsqrt(jnp.mean(x * x, axis=-1, keepdims=True) + NORM_EPS)


def _inproj_kernel(x_ref, mod_ref, n1_ref, win_ref, *rest, need_out):
    if need_out:
        wz_ref, rw_ref, z_ref, g_ref = rest
    else:
        (rw_ref,) = rest
    x = x_ref[0]
    shift = mod_ref[0, 0:1, :]
    scale = mod_ref[0, 1:2, :]
    h = _bf(_rms(x) * n1_ref[...] * (1.0 + scale) + shift)
    rw_ref[0] = _dot(h, win_ref[:, F_WIDTH:F_WIDTH + RWKV_IN])
    if need_out:
        z = _dot(h, wz_ref[...])
        z_ref[0, 0] = _bf(z[:, :F_WIDTH])
        z_ref[0, 1] = _bf(z[:, F_WIDTH:])
        g_ref[0] = _bf(_dot(h, win_ref[:, F_WIDTH + RWKV_IN:]))


def _inproj(x, mod, mod_row, layer, n1, win, wz, need_out):
    b, l, d = x.shape
    tm = min(256, l)
    row_spec = lambda w: pl.BlockSpec((1, tm, w), lambda bi, i: (bi, i, 0))
    in_specs = [row_spec(d), _mod_spec(mod, layer, mod_row), _layer_spec(n1, layer), _layer_spec(win, layer)]
    args = [x, mod, n1, win]
    out_specs = [row_spec(RWKV_IN)]
    out_shape = [jax.ShapeDtypeStruct((b, l, RWKV_IN), F32)]
    if need_out:
        in_specs += [_const_spec(wz.shape)]
        args += [wz]
        out_specs += [pl.BlockSpec((1, 2, tm, F_WIDTH), lambda bi, i: (bi, 0, i, 0)), row_spec(GATE_W)]
        out_shape += [jax.ShapeDtypeStruct((b, 2, l, F_WIDTH), BF16),
                      jax.ShapeDtypeStruct((b, l, GATE_W), BF16)]
    return pl.pallas_call(
        functools.partial(_inproj_kernel, need_out=need_out),
        grid=(b, l // tm),
        in_specs=in_specs, out_specs=out_specs, out_shape=out_shape,
        compiler_params=_params("parallel", "parallel"),
        name="inproj",
    )(*args)


def _log_sigmoid(x):
    return jnp.minimum(x, 0.0) - jnp.log1p(jnp.exp(-jnp.abs(x)))


P_KT, P_RT, P_KH, P_BH, P_V = 0, 2, 4, 6, 8
N_PLANES = 9
N_SLOTS = 2


def _co_emit(main, filler, every):
    for k, _ in enumerate(main):
        if k % every == 0:
            next(filler, None)
    for _ in filler:
        pass


def _scan_kernel(*refs, grid_mode, seq_len, tm, n_tiles, n_steps):
    n_in = len(refs) - 8
    in_refs = refs[:n_in]
    reff_ref, yloc_ref, m_ref, gs_ref, g_ref, bonus_ref, pp_ref, et_ref = refs[n_in:]
    s = pl.program_id(0)
    slot_prep = jnp.bitwise_and(s, 1)
    slot_alg = 1 - slot_prep
    tile = jnp.minimum(s, n_steps - 2) % n_tiles

    @pl.when(s == 0)
    def _():
        pp_ref[1] = jnp.zeros(pp_ref.shape[1:], pp_ref.dtype)
        et_ref[1] = jnp.zeros(et_ref.shape[1:], et_ref.dtype)

    prep = _prep_tile(in_refs, pp_ref, et_ref, slot_prep, g_ref, bonus_ref, tile,
                      grid_mode=grid_mode, seq_len=seq_len, tm=tm)
    alg = _chunk_algebra(pp_ref, et_ref, slot_alg, reff_ref, yloc_ref, m_ref, gs_ref, tc=tm)
    _co_emit(alg, prep, every=3)


def _prep_tile(in_refs, pp_ref, et_ref, slot, g_ref, bonus_ref, tile, *, grid_mode, seq_len, tm):
    sb = min(PREP_ROWS, tm) if grid_mode else tm
    for r0 in range(0, tm, sb):
        yield from _prep_rows(r0, sb, in_refs, pp_ref, et_ref, slot, g_ref, bonus_ref, tile,
                              grid_mode=grid_mode, seq_len=seq_len, tm=tm)


def _prep_rows(r0, sb, in_refs, pp_ref, et_ref, slot, g_ref, bonus_ref, tile, *, grid_mode, seq_len, tm):
    if grid_mode:
        (rw_ref, prev_ref, next_ref, mu_ref, w0_ref, a0_ref, wup_ref, aup_ref, gup_ref,
         kkw_ref, ka_ref, rk_ref, eh_ref) = in_refs
    else:
        (rw_ref, mu_ref, w0_ref, a0_ref, wup_ref, aup_ref, gup_ref,
         kkw_ref, ka_ref, rk_ref, eh_ref) = in_refs
    rows = slice(r0, r0 + sb)
    t_loc = lax.broadcasted_iota(jnp.int32, (sb, 128), 0) + r0
    t_glob = t_loc + tile * tm
    lane = lax.broadcasted_iota(jnp.int32, (sb, 128), 1)
    if grid_mode:
        col = jnp.bitwise_and(t_loc, GRID_W - 1)
        masks = [col != 0, col != GRID_W - 1, t_glob >= GRID_W, t_glob < seq_len - GRID_W]
        n_parts = 4
    else:
        masks = [t_glob != 0, t_glob != seq_len - 1]
        n_parts = 2
    part_w = RWKV_IN // n_parts

    def shifted(j, part):
        cols = slice(128 * j, 128 * (j + 1))
        if part == 0:
            s = pltpu.roll(rw_ref[0, rows, cols], 1, 0)
        elif part == 1:
            s = pltpu.roll(rw_ref[0, rows, cols], sb - 1, 0)
        elif part == 2:
            if r0 > 0:
                s = rw_ref[0, r0 - GRID_W:r0 + sb - GRID_W, cols]
            elif sb == GRID_W:
                s = prev_ref[0, :, cols]
            else:
                s = jnp.concatenate([prev_ref[0, :, cols], rw_ref[0, 0:sb - GRID_W, cols]], axis=0)
        else:
            if r0 + sb < tm:
                s = rw_ref[0, r0 + GRID_W:r0 + sb + GRID_W, cols]
            elif sb == GRID_W:
                s = next_ref[0, :, cols]
            else:
                s = jnp.concatenate([rw_ref[0, r0 + GRID_W:tm, cols], next_ref[0, :, cols]], axis=0)
        return jnp.where(masks[part], s, 0.0)

    blocks = []
    for j in range(RWKV_IN // 128):
        p_lo = (128 * j) // part_w
        p_hi = (128 * j + 127) // part_w
        s = shifted(j, p_lo)
        if p_hi != p_lo:
            s = jnp.where(lane + 128 * j < part_w * p_hi, s, shifted(j, p_hi))
        xj = rw_ref[0, rows, 128 * j:128 * (j + 1)]
        blocks.append(xj + mu_ref[:, 128 * j:128 * (j + 1)] * (s - xj))
        yield

    nb = R_WIDTH // 128
    r = jnp.concatenate(blocks[0:nb], axis=1)
    k = jnp.concatenate(blocks[nb:2 * nb], axis=1)
    v = jnp.concatenate(blocks[2 * nb:3 * nb], axis=1)
    wd, ad, gd = blocks[3 * nb], blocks[3 * nb + 1], blocks[3 * nb + 2]

    tw = _bf(jnp.tanh(wd))
    adb = _bf(ad)
    lw_cols, a_cols = [], []
    for c0 in range(0, 2 * R_WIDTH, GROUP_LANES):
        cols = slice(c0, c0 + GROUP_LANES)
        w_logit = w0_ref[:, cols] + _dot(tw, wup_ref[:, cols])
        lw_cols.append(-jnp.exp(_log_sigmoid(w_logit) - 0.5))
        yield
        a_cols.append(jax.nn.sigmoid(a0_ref[:, cols] + _dot(adb, aup_ref[:, cols])))
        yield
    lw = jnp.concatenate(lw_cols, axis=1)
    a = jnp.concatenate(a_cols, axis=1)
    g_ref[0, rows] = _bf(_dot(_bf(jax.nn.sigmoid(gd)), gup_ref[...]))
    yield

    eh = eh_ref[...]
    kx = k * kkw_ref[...]
    kk = kx / jnp.maximum(jnp.sqrt(_head_sum(kx * kx, eh)), L2_EPS)
    yield
    ka = ka_ref[...]
    kd = (k * (1.0 + (a[:, :R_WIDTH] - 1.0) * ka), k * (1.0 + (a[:, R_WIDTH:] - 1.0) * ka))
    bonus_ref[0, rows] = _bf(_head_sum(r * (kd[0] + kd[1]) * rk_ref[...], eh) * v)
    pp_ref[slot, P_V, rows] = _bf(v)
    yield

    t64 = lax.broadcasted_iota(jnp.int32, (CHUNK, CHUNK), 0)
    s64 = lax.broadcasted_iota(jnp.int32, (CHUNK, CHUNK), 1)
    tri = (jnp.where(s64 <= t64, 1.0, 0.0).astype(BF16), jnp.where(s64 >= t64, 1.0, 0.0).astype(BF16))
    for d in range(2):
        lw_d = lw[:, d * R_WIDTH:(d + 1) * R_WIDTH]
        b_d = kk * a[:, d * R_WIDTH:(d + 1) * R_WIDTH]
        for c0 in range(0, sb, CHUNK):
            loc = slice(c0, c0 + CHUNK)
            dst = slice(r0 + c0, r0 + c0 + CHUNK)
            lwc = lw_d[loc]
            hi, lo = _split2(lwc)
            cum = _dot(tri[d], hi) + _dot(tri[d], lo)
            e_neg = jnp.exp(-cum)
            pp_ref[slot, P_KT + d, dst] = _bf(kk[loc] * jnp.exp(cum - lwc))
            pp_ref[slot, P_RT + d, dst] = _bf(r[loc] * jnp.exp(cum))
            pp_ref[slot, P_KH + d, dst] = _bf(kd[d][loc] * e_neg)
            pp_ref[slot, P_BH + d, dst] = _bf(b_d[loc] * e_neg)
            et = jnp.exp(jnp.sum(lwc, axis=0, keepdims=True))
            et_ref[slot, d, (r0 + c0) // CHUNK] = jnp.broadcast_to(et, (8, R_WIDTH))
            yield


def _scan_pass(rw, grid_mode, layer, mu, w0c, a0c, wup, aup, gup, kkw, ka, rk, eh):
    b, l, _ = rw.shape
    tm = min(256, l) if grid_mode else l
    nt = l // tm
    n_steps = b * nt + 1
    nrow = l // GRID_W
    per = tm // GRID_W
    prep_bi = lambda s: (jnp.minimum(s, n_steps - 2) // nt, jnp.minimum(s, n_steps - 2) % nt)
    alg_bi = lambda s: (jnp.maximum(s - 1, 0) // nt, jnp.maximum(s - 1, 0) % nt)

    def rw_map(s):
        bi, i = prep_bi(s)
        return bi, i, 0

    def prev_map(s):
        bi, i = prep_bi(s)
        return bi, jnp.maximum(i * per - 1, 0), 0

    def next_map(s):
        bi, i = prep_bi(s)
        return bi, jnp.minimum((i + 1) * per, nrow - 1), 0

    def out_map(s):
        bi, i = alg_bi(s)
        return 0, bi, i, 0

    in_specs = [pl.BlockSpec((1, tm, RWKV_IN), rw_map)]
    args = [rw]
    if grid_mode:
        in_specs += [pl.BlockSpec((1, GRID_W, RWKV_IN), prev_map), pl.BlockSpec((1, GRID_W, RWKV_IN), next_map)]
        args += [rw, rw]
    stacked = [mu, w0c, a0c, wup, aup, gup, kkw, ka, rk]
    in_specs += [_layer_spec(c, layer) for c in stacked] + [_const_spec(eh.shape)]
    args += stacked + [eh]
    one = pl.BlockSpec((1, tm, R_WIDTH), rw_map)
    two = pl.BlockSpec((2, 1, tm, R_WIDTH), out_map)
    s1 = jax.ShapeDtypeStruct((b, l, R_WIDTH), BF16)
    s2 = jax.ShapeDtypeStruct((2, b, l, R_WIDTH), F32)
    s2h = jax.ShapeDtypeStruct((2, b, l, R_WIDTH), BF16)
    return pl.pallas_call(
        functools.partial(_scan_kernel, grid_mode=grid_mode, seq_len=l, tm=tm, n_tiles=nt, n_steps=n_steps),
        grid=(n_steps,),
        in_specs=in_specs,
        out_specs=[two, two, two, two, one, one],
        out_shape=[s2h, s2h, s2, s2, s1, s1],
        scratch_shapes=[pltpu.VMEM((N_SLOTS, N_PLANES, tm, R_WIDTH), BF16),
                        pltpu.VMEM((N_SLOTS, 2, tm // CHUNK, 8, R_WIDTH), F32)],
        compiler_params=_params("arbitrary"),
        name="rwkv_scan",
    )(*args)


def _block_diag_mask():
    return lax.broadcasted_iota(jnp.int32, (CHUNK, LANE_TILE), 1) < HEAD_DIM


def _block_diag(x, first_head):
    zero = jnp.zeros((CHUNK, LANE_TILE), x.dtype)
    per_tile = LANE_TILE // HEAD_DIM
    blocks = []
    for h in range(HEADS_PER_GROUP):
        tile = h // per_tile
        piece = x[:, tile * LANE_TILE:(tile + 1) * LANE_TILE]
        piece = jnp.where(first_head, piece, zero) if h % per_tile == 0 else jnp.where(first_head, zero, piece)
        row = [zero] * (GROUP_LANES // LANE_TILE)
        row[tile] = piece
        blocks.append(jnp.concatenate(row, axis=1))
    return jnp.concatenate(blocks, axis=0)


def _chunk_algebra(pp_ref, et_ref, slot, reff_ref, yloc_ref, m_ref, gs_ref, *, tc):
    t = lax.broadcasted_iota(jnp.int32, (CHUNK, GROUP_LANES), 0)
    lane = lax.broadcasted_iota(jnp.int32, (CHUNK, GROUP_LANES), 1)
    s = jnp.bitwise_and(lane, CHUNK - 1)
    eye = jnp.where(s == t, 1.0, 0.0)
    before = (s < t, s > t)
    upto = (s <= t, s >= t)
    bdmask = _block_diag_mask()
    bdiag = lambda x: _block_diag(_bf(x), bdmask)
    units = [(d, slice(ci * CHUNK, (ci + 1) * CHUNK), slice(gi * GROUP_LANES, (gi + 1) * GROUP_LANES))
             for d in range(2) for ci in range(tc // CHUNK) for gi in range(N_GROUPS)]
    nu = range(len(units))
    dirs = [d for d, _, _ in units]

    def staged(fn):
        out = []
        for u in nu:
            out.append(fn(u))
            yield
        return out

    kt = [pp_ref[slot, P_KT + d, rows, lanes] for d, rows, lanes in units]
    rt = [pp_ref[slot, P_RT + d, rows, lanes] for d, rows, lanes in units]
    vv = [pp_ref[slot, P_V, rows, lanes] for d, rows, lanes in units]
    dec = [eye * et_ref[slot, d, rows.start // CHUNK, 0:1, lanes] for d, rows, lanes in units]

    def nt_products(plane):
        def fn(u):
            d, rows, lanes = units[u]
            dh, dl = _split2(dec[u])
            lhs = jnp.concatenate([kt[u], rt[u], dh, dl], axis=0)
            return _dot_nt(lhs, _block_diag(pp_ref[slot, plane + d, rows, lanes], bdmask))
        return fn

    sk = yield from staged(nt_products(P_KH))
    sb = yield from staged(nt_products(P_BH))
    a_k = [jnp.where(before[dirs[u]], sk[u][:CHUNK], 0.0) for u in nu]
    a_rk = [jnp.where(upto[dirs[u]], sk[u][CHUNK:2 * CHUNK], 0.0) for u in nu]
    kb_t = [sk[u][2 * CHUNK:3 * CHUNK] + sk[u][3 * CHUNK:] for u in nu]
    n = [jnp.where(before[dirs[u]], -sb[u][:CHUNK], 0.0) for u in nu]
    a_rb = [jnp.where(upto[dirs[u]], sb[u][CHUNK:2 * CHUNK], 0.0) for u in nu]
    bb_t = [sb[u][2 * CHUNK:3 * CHUNK] + sb[u][3 * CHUNK:] for u in nu]

    tinv = [eye + n[u] for u in nu]
    p = yield from staged(lambda u: _dot(_bf(n[u]), bdiag(n[u])))
    av = yield from staged(
        lambda u: _dot(_bf(jnp.concatenate([a_k[u], a_rk[u], kb_t[u]], axis=0)), _block_diag(vv[u], bdmask)))
    for _ in range(4):
        tp = yield from staged(lambda u: _dot(_bf(jnp.concatenate([tinv[u], p[u]], axis=0)), bdiag(p[u])))
        tinv = [tinv[u] + tp[u][:CHUNK] for u in nu]
        p = [tp[u][CHUNK:] for u in nu]
    tb = yield from staged(lambda u: _bf(tinv[u] + _dot(_bf(tinv[u]), bdiag(p[u]))))

    uk = yield from staged(lambda u: _dot(tb[u], _block_diag(kt[u], bdmask)))
    uv = yield from staged(lambda u: _dot(tb[u], bdiag(av[u][:CHUNK])))
    rows_b = [_bf(jnp.concatenate([a_rb[u], bb_t[u]], axis=0)) for u in nu]
    bk = yield from staged(lambda u: _dot(rows_b[u], bdiag(uk[u])))
    bv = yield from staged(lambda u: _dot(rows_b[u], bdiag(uv[u])))
    for u, (d, rows, lanes) in enumerate(units):
        reff_ref[d, 0, rows, lanes] = _bf(rt[u].astype(F32) - bk[u][:CHUNK])
        yloc_ref[d, 0, rows, lanes] = _bf(av[u][CHUNK:2 * CHUNK] - bv[u][:CHUNK])
        m_ref[d, 0, rows, lanes] = dec[u] - bk[u][CHUNK:]
        gs_ref[d, 0, rows, lanes] = av[u][2 * CHUNK:] - bv[u][CHUNK:]
        yield


def _state_kernel(m0_ref, g0_ref, re0_ref, yl0_ref, m1_ref, g1_ref, re1_ref, yl1_ref, s0_ref,
                  y0_ref, y1_ref, h_ref, *, batch):
    c = pl.program_id(0)

    @pl.when(c == 0)
    def _():
        h_ref[...] = s0_ref[...]

    bdmask = _block_diag_mask()
    per_dir = ((m0_ref, g0_ref, re0_ref, yl0_ref, y0_ref), (m1_ref, g1_ref, re1_ref, yl1_ref, y1_ref))
    units = [(d, bi, slice(gi * GROUP_LANES, (gi + 1) * GROUP_LANES))
             for d in range(2) for bi in range(batch) for gi in range(N_GROUPS)]
    hs = [_split2(h_ref[d, bi, :, lanes]) for d, bi, lanes in units]
    ms = [_split2(per_dir[d][0][0, bi, :, lanes]) for d, bi, lanes in units]
    rs = [per_dir[d][2][0, bi, :, lanes] for d, bi, lanes in units]
    o1 = [_dot(jnp.concatenate([ms[u][0], ms[u][1], rs[u]], axis=0), _block_diag(hs[u][0], bdmask))
          for u in range(len(units))]
    o2 = [_dot(jnp.concatenate([ms[u][0], rs[u]], axis=0), _block_diag(hs[u][1], bdmask))
          for u in range(len(units))]
    for u, (d, bi, lanes) in enumerate(units):
        _, g_ref, _, yl_ref, y_ref = per_dir[d]
        mh_new = o1[u][0:CHUNK] + o1[u][CHUNK:2 * CHUNK] + o2[u][0:CHUNK]
        rh_new = o1[u][2 * CHUNK:] + o2[u][CHUNK:]
        y_ref[bi, :, lanes] = _bf(yl_ref[0, bi, :, lanes].astype(F32) + rh_new)
        h_ref[d, bi, :, lanes] = mh_new + g_ref[0, bi, :, lanes]


def _state_pass(m, g, reff, yloc, s0):
    _, b, l, _ = m.shape
    nc = l // CHUNK
    blk = (1, b, CHUNK, R_WIDTH)
    fwd = pl.BlockSpec(blk, lambda c: (0, 0, c, 0))
    bwd = pl.BlockSpec(blk, lambda c: (1, 0, nc - 1 - c, 0))
    st = pl.BlockSpec((2, b, HEAD_DIM, R_WIDTH), lambda c: (0, 0, 0, 0))
    sy = jax.ShapeDtypeStruct((b, l, R_WIDTH), BF16)
    return pl.pallas_call(
        functools.partial(_state_kernel, batch=b),
        grid=(nc,),
        in_specs=[fwd, fwd, fwd, fwd, bwd, bwd, bwd, bwd, st],
        out_specs=[pl.BlockSpec((b, CHUNK, R_WIDTH), lambda c: (0, c, 0)),
                   pl.BlockSpec((b, CHUNK, R_WIDTH), lambda c: (0, nc - 1 - c, 0)),
                   st],
        out_shape=[sy, sy, jax.ShapeDtypeStruct((2, b, HEAD_DIM, R_WIDTH), F32)],
        compiler_params=_params("arbitrary"),
        name="rwkv_state",
    )(m, g, reff, yloc, m, g, reff, yloc, s0)


def _posdft_kernel(cs_ref, z_ref, o_ref):
    l = z_ref.shape[2]
    z = z_ref[0].reshape(2 * l, F_WIDTH)
    o_ref[0] = _bf(_dot(cs_ref[...], z))


def _posdft_fold_kernel(w_ref, rev_ref, z_ref, o_ref, f_ref, *, tm):
    l = z_ref.shape[2]
    half = l // 2
    nb = l // REV_BLOCK
    i = pl.program_id(1)

    @pl.when(i == 0)
    def _():
        for part in range(2):
            for j in range(half // REV_BLOCK):
                blk = z_ref[0, part, j * REV_BLOCK:(j + 1) * REV_BLOCK, :]
                src_a = z_ref[0, part, l - (j + 1) * REV_BLOCK:l - j * REV_BLOCK, :]
                jb = (nb - j) % nb
                src_b = z_ref[0, part, jb * REV_BLOCK:(jb + 1) * REV_BLOCK, :]
                rev = _dot(rev_ref[0], src_a) + _dot(rev_ref[1], src_b)
                sgn = 1.0 if part == 0 else -1.0
                f_ref[part * half + j * REV_BLOCK:part * half + (j + 1) * REV_BLOCK, :] = _bf(
                    blk.astype(F32) + sgn * rev)

    row = lax.broadcasted_iota(jnp.int32, (tm, F_WIDTH), 0) + i * tm
    alt = (1 - 2 * jnp.bitwise_and(row, 1)).astype(F32) * (1.0 / float(np.sqrt(l)))
    mid = z_ref[0, 0, half:half + 1, :].astype(F32)
    o_ref[0] = _bf(_dot(w_ref[...], f_ref[...]) + alt * mid)


def _pos_dft(z, cs):
    b, _, l, _ = z.shape
    tm = min(512, l)
    if isinstance(cs, tuple):
        w, rev = cs
        return pl.pallas_call(
            functools.partial(_posdft_fold_kernel, tm=tm),
            grid=(b, l // tm),
            in_specs=[pl.BlockSpec((tm, l), lambda bi, i: (i, 0)),
                      _const_spec(rev.shape),
                      pl.BlockSpec((1, 2, l, F_WIDTH), lambda bi, i: (bi, 0, 0, 0))],
            out_specs=pl.BlockSpec((1, tm, F_WIDTH), lambda bi, i: (bi, i, 0)),
            out_shape=jax.ShapeDtypeStruct((b, l, F_WIDTH), BF16),
            scratch_shapes=[pltpu.VMEM((l, F_WIDTH), BF16)],
            compiler_params=_params("parallel", "arbitrary"),
            name="pos_dft_fold",
        )(w, rev, z)
    return pl.pallas_call(
        _posdft_kernel,
        grid=(b, l // tm),
        in_specs=[pl.BlockSpec((tm, 2 * l), lambda bi, i: (i, 0)),
                  pl.BlockSpec((1, 2, l, F_WIDTH), lambda bi, i: (bi, 0, 0, 0))],
        out_specs=pl.BlockSpec((1, tm, F_WIDTH), lambda bi, i: (bi, i, 0)),
        out_shape=jax.ShapeDtypeStruct((b, l, F_WIDTH), BF16),
        compiler_params=_params("parallel", "parallel"),
        name="pos_dft",
    )(cs, z)


def _merge_mlp_kernel(f_ref, y0_ref, y1_ref, g_ref, bonus_ref, gates_ref, x_ref, mod_ref,
                      lnw_ref, lnb_ref, n2_ref, nf_ref, eh_ref, wf_ref, wr_ref, wo_ref, w1_ref, w2_ref,
                      o_ref, *, final_norm):
    y = y0_ref[0].astype(F32) + y1_ref[0].astype(F32)
    eh = eh_ref[...]
    inv_n = 1.0 / HEAD_DIM
    mu = _head_sum(y, eh) * inv_n
    dlt = y - mu
    var = _head_sum(dlt * dlt, eh) * inv_n
    yn = dlt * lax.rsqrt(var + GN_EPS) * lnw_ref[...] + lnb_ref[...]
    rwkv = _bf((yn + bonus_ref[0].astype(F32)) * g_ref[0].astype(F32))
    fo = _dot(f_ref[0], wf_ref[...])
    ro = _dot(rwkv, wr_ref[...])
    gates = gates_ref[0].astype(F32)
    merged = jax.nn.sigmoid(gates[:, :D_MODEL]) * fo + jax.nn.sigmoid(gates[:, D_MODEL:]) * ro
    x = x_ref[0] + mod_ref[0, 2:3, :] * _dot(_bf(merged), wo_ref[...])
    h = _bf(_rms(x) * n2_ref[...] * (1.0 + mod_ref[0, 4:5, :]) + mod_ref[0, 3:4, :])
    acc = jnp.zeros(x.shape, F32)
    step = 1024
    for j in range(D_FF // step):
        u = jnp.maximum(_dot(h, w1_ref[:, j * step:(j + 1) * step]), 0.0)
        acc = acc + _dot(_bf(u * u), w2_ref[j * step:(j + 1) * step, :])
    x2 = x + mod_ref[0, 5:6, :] * acc
    if final_norm:
        x2 = _rms(x2) * nf_ref[...]
    o_ref[0] = x2


def _merge_mlp(f, y0, y1, g, bonus, gates, x, mod, mod_row, layer, lnw, lnb, n2, nf, eh, wf, wr, wo, w1, w2,
               final_norm):
    b, l, d = x.shape
    tm = min(256, l)
    row = lambda w: pl.BlockSpec((1, tm, w), lambda bi, i: (bi, i, 0))
    once = lambda a: pl.BlockSpec(a.shape, lambda bi, i: (0, 0), pipeline_mode=pl.Buffered(1))
    stacked = lambda a: _layer_spec(a, layer, single_buffer=True)
    return pl.pallas_call(
        functools.partial(_merge_mlp_kernel, final_norm=final_norm),
        grid=(b, l // tm),
        in_specs=[row(F_WIDTH), row(R_WIDTH), row(R_WIDTH), row(R_WIDTH), row(R_WIDTH), row(GATE_W), row(d),
                  _mod_spec(mod, layer, mod_row),
                  stacked(lnw), stacked(lnb), stacked(n2), once(nf), once(eh),
                  stacked(wf), stacked(wr), stacked(wo), stacked(w1), stacked(w2)],
        out_specs=row(d),
        out_shape=jax.ShapeDtypeStruct((b, l, d), F32),
        compiler_params=_params("parallel", "parallel"),
        name="merge_mlp",
    )(f, y0, y1, g, bonus, gates, x, mod, lnw, lnb, n2, nf, eh, wf, wr, wo, w1, w2)


def _channel_dft():
    n = FGROUP_DIM
    jk = np.outer(np.arange(n), np.arange(n)) % n
    ang = 2.0 * np.pi * jk / n
    c = np.cos(ang) / np.sqrt(n)
    s = np.sin(ang) / np.sqrt(n)
    g = F_WIDTH // n
    out = np.zeros((F_WIDTH, 2 * F_WIDTH), np.float32)
    for i in range(g):
        out[i * n:(i + 1) * n, i * n:(i + 1) * n] = c
        out[i * n:(i + 1) * n, F_WIDTH + i * n:F_WIDTH + (i + 1) * n] = s
    return jnp.asarray(out)


def _position_dft(l):
    fold = (l // 2) % REV_BLOCK == 0
    nk = l // 2 if fold else l
    jk = np.outer(np.arange(l), np.arange(nk)) % l
    ang = 2.0 * np.pi * jk / l
    c = np.cos(ang) / np.sqrt(l)
    if fold:
        c[:, 0] *= 0.5
    cs = jnp.asarray(np.concatenate([c, -np.sin(ang) / np.sqrt(l)], axis=1).astype(np.float32)).astype(BF16)
    if not fold:
        return cs
    rev = np.zeros((2, REV_BLOCK, REV_BLOCK), np.float32)
    idx = np.arange(1, REV_BLOCK)
    rev[0, idx, REV_BLOCK - idx] = 1.0
    rev[1, 0, 0] = 1.0
    return cs, jnp.asarray(rev).astype(BF16)


def _head_ones():
    h = np.arange(R_WIDTH) // HEAD_DIM
    return jnp.asarray((h[:, None] == h[None, :]).astype(np.float32)).astype(BF16)


def _two_dir_lora(w):
    z = jnp.zeros_like(w[:, 0])
    return _bf(jnp.concatenate([jnp.concatenate([w[:, 0], z], axis=2),
                                jnp.concatenate([z, w[:, 1]], axis=2)], axis=1))


def _layer(x, mod_row, layer, grid_mode, need_out, final_norm, s0, p):
    rw_out = _inproj(x, p["mod"], mod_row, layer, p["n1"], p["win"], p["wz"][layer], need_out)
    reff, yloc, m, gs, g, bonus = _scan_pass(rw_out[0], grid_mode, layer, p["mu"], p["w0"], p["a0"], p["wup"],
                                             p["aup"], p["gup"], p["kkw"], p["ka"], p["rk"], p["eh"])
    y0, y1, s_fin = _state_pass(m, gs, reff, yloc, s0)
    if not need_out:
        return None, s_fin
    z, gates = rw_out[1], rw_out[2]
    f = _pos_dft(z, p["pos_dft"][x.shape[1]])
    x2 = _merge_mlp(f, y0, y1, g, bonus, gates, x, p["mod"], mod_row, layer, p["lnw"], p["lnb"], p["n2"],
                    p["nf"], p["eh"], p["wf"], p["wr"], p["wo"], p["w1"], p["w2"], final_norm)
    return x2, s_fin


def kernel(x, c, ctx, c_ctx, w_mod, b_mod, norm1, norm2, w_in, mu_shift, w0, w_up, a0, a_up, g_up,
           k_k, k_a, r_k, ln_x_w, ln_x_b, w_fourier_up, w_rwkv_up, w_out, mlp_w1, mlp_w2, norm_f):
    depth = w_mod.shape[0]
    batch, seq, d = x.shape
    ctx_len = ctx.shape[1]
    assert d == D_MODEL and batch + 1 <= MOD_ROWS
    assert seq % GRID_W == 0 and ctx_len % CHUNK == 0 and seq % CHUNK == 0

    cc = jnp.zeros((MOD_ROWS, d), F32).at[:batch].set(c).at[batch].set(c_ctx)
    cdft = _channel_dft()
    rows = lambda a: a.reshape(depth, 1, -1)
    p = {
        "eh": _head_ones(),
        "pos_dft": {n: _position_dft(n) for n in {seq, ctx_len}},
        "nf": norm_f.reshape(1, -1),
        "mod": _modulation(cc, w_mod, b_mod).reshape(depth, MOD_ROWS, N_MOD, d),
        "wz": [_fold_channel_dft(w_in, layer, cdft) for layer in range(depth)],
        "win": _bf(w_in),
        "n1": rows(norm1), "n2": rows(norm2), "mu": rows(mu_shift), "w0": rows(w0), "a0": rows(a0),
        "wup": _two_dir_lora(w_up), "aup": _two_dir_lora(a_up), "gup": _bf(g_up),
        "kkw": rows(k_k), "ka": rows(k_a), "rk": rows(r_k), "lnw": rows(ln_x_w), "lnb": rows(ln_x_b),
        "wf": _bf(w_fourier_up), "wr": _bf(w_rwkv_up), "wo": _bf(w_out),
        "w1": _bf(mlp_w1), "w2": _bf(mlp_w2),
    }
    lat_row = lambda bi: bi
    ctx_row = lambda bi: batch

    x_lat, x_ctx = x, ctx
    s_zero = jnp.zeros((2, batch, HEAD_DIM, R_WIDTH), F32)
    for layer in range(depth):
        last = layer == depth - 1
        x_ctx, s_ctx = _layer(x_ctx, ctx_row, layer, False, not last, False, s_zero, p)
        x_lat, _ = _layer(x_lat, lat_row, layer, True, True, last, s_ctx, p)
    return x_lat
```

```python
import functools

import numpy as np
import jax
import jax.numpy as jnp
from jax import lax
from jax.experimental import pallas as pl
from jax.experimental.pallas import tpu as pltpu

F32 = jnp.float32
BF16 = jnp.bfloat16

D_MODEL = 1024
GRID_W = 64
F_WIDTH = 512
FGROUP_DIM = 128
HEAD_DIM = 64
N_RHEADS = 8
R_WIDTH = N_RHEADS * HEAD_DIM
D_LORA = 64
D_GATE_LORA = 128
RWKV_IN = 3 * R_WIDTH + 4 * D_LORA + D_GATE_LORA
GATE_W = 2 * D_MODEL
D_FF = 4 * D_MODEL
N_MOD = 6
NORM_EPS = 1e-6
GN_EPS = 64e-5
L2_EPS = 1e-12

CHUNK = 64
PREP_ROWS = 128
GROUP_LANES = 256
HEADS_PER_GROUP = GROUP_LANES // HEAD_DIM
LANE_TILE = 128
REV_BLOCK = 256
DFT_PIECE = 128
N_GROUPS = R_WIDTH // GROUP_LANES
MOD_ROWS = 16
VMEM_LIMIT = 56 * 1024 * 1024


def _bf(x):
    return x.astype(BF16)


def _dot(a, b):
    return jnp.dot(a, b, preferred_element_type=F32)


def _dot_nt(a, b):
    return lax.dot_general(a, b, (((1,), (1,)), ((), ())), preferred_element_type=F32)


def _split2(x):
    hi = _bf(x)
    lo = _bf(x - hi.astype(F32))
    return hi, lo


def _head_sum(x, ones):
    return _dot(_bf(x), ones)


def _params(*sem):
    return pltpu.CompilerParams(dimension_semantics=sem, vmem_limit_bytes=VMEM_LIMIT)


def _const_spec(shape):
    zeros = (0,) * len(shape)
    return pl.BlockSpec(shape, lambda *_: zeros)


def _layer_spec(a, l, single_buffer=False):
    tail = tuple(a.shape[1:])
    idx = (l,) + (0,) * len(tail)
    kw = {"pipeline_mode": pl.Buffered(1)} if single_buffer else {}
    return pl.BlockSpec((None,) + tail, lambda *_: idx, **kw)


def _mod_spec(mod, l, mod_row):
    return pl.BlockSpec((None, 1) + tuple(mod.shape[2:]), lambda bi, i: (l, mod_row(bi), 0, 0))


def _modulation_kernel(x_ref, w_ref, b_ref, o_ref):
    x = x_ref[...]
    x = x * jax.nn.sigmoid(x)
    o_ref[0] = _dot(_bf(x), _bf(w_ref[0])) + b_ref[0]


def _modulation(cc, w, b):
    depth, d, n = w.shape
    tn = 512
    return pl.pallas_call(
        _modulation_kernel,
        grid=(depth, n // tn),
        in_specs=[_const_spec(cc.shape),
                  pl.BlockSpec((1, d, tn), lambda l, j: (l, 0, j)),
                  pl.BlockSpec((1, 1, tn), lambda l, j: (l, 0, j))],
        out_specs=pl.BlockSpec((1, cc.shape[0], tn), lambda l, j: (l, 0, j)),
        out_shape=jax.ShapeDtypeStruct((depth, cc.shape[0], n), F32),
        compiler_params=_params("parallel", "parallel"),
        name="modulation",
    )(cc, w, b.reshape(depth, 1, n))


def _fold_kernel(w_ref, c_ref, o_ref):
    wh, wl = _split2(w_ref[...])
    ch, cl = _split2(c_ref[...])
    o_ref[...] = _bf(_dot(wh, ch) + _dot(wl, ch) + _dot(wh, cl))


def _fold_channel_dft(w_in, l, cdft):
    d = w_in.shape[1]
    return pl.pallas_call(
        _fold_kernel,
        grid=(1,),
        in_specs=[pl.BlockSpec((None, d, F_WIDTH), lambda i: (l, 0, 0)), _const_spec(cdft.shape)],
        out_specs=_const_spec((d, cdft.shape[1])),
        out_shape=jax.ShapeDtypeStruct((d, cdft.shape[1]), BF16),
        compiler_params=_params("arbitrary"),
        name="fold_channel_dft",
    )(w_in, cdft)


def _rms(x):
    return x * lax.rsqrt(jnp.mean(x * x, axis=-1, keepdims=True) + NORM_EPS)


def _inproj_kernel(x_ref, mod_ref, n1_ref, win_ref, *rest, need_out):
    if need_out:
        wz_ref, rw_ref, z_ref, g_ref = rest
    else:
        (rw_ref,) = rest
    x = x_ref[0]
    shift = mod_ref[0, 0:1, :]
    scale = mod_ref[0, 1:2, :]
    h = _bf(_rms(x) * n1_ref[...] * (1.0 + scale) + shift)
    rw_ref[0] = _dot(h, win_ref[:, F_WIDTH:F_WIDTH + RWKV_IN])
    if need_out:
        z = _dot(h, wz_ref[...])
        z_ref[0, 0] = _bf(z[:, :F_WIDTH])
        z_ref[0, 1] = _bf(z[:, F_WIDTH:])
        g_ref[0] = _bf(_dot(h, win_ref[:, F_WIDTH + RWKV_IN:]))


def _inproj(x, mod, mod_row, layer, n1, win, wz, need_out):
    b, l, d = x.shape
    tm = min(512, l)
    row_spec = lambda w: pl.BlockSpec((1, tm, w), lambda bi, i: (bi, i, 0))
    in_specs = [row_spec(d), _mod_spec(mod, layer, mod_row), _layer_spec(n1, layer),
                _layer_spec(win, layer, single_buffer=True)]
    args = [x, mod, n1, win]
    out_specs = [row_spec(RWKV_IN)]
    out_shape = [jax.ShapeDtypeStruct((b, l, RWKV_IN), F32)]
    if need_out:
        in_specs += [_const_spec(wz.shape)]
        args += [wz]
        out_specs += [pl.BlockSpec((1, 2, tm, F_WIDTH), lambda bi, i: (bi, 0, i, 0)), row_spec(GATE_W)]
        out_shape += [jax.ShapeDtypeStruct((b, 2, l, F_WIDTH), BF16),
                      jax.ShapeDtypeStruct((b, l, GATE_W), BF16)]
    return pl.pallas_call(
        functools.partial(_inproj_kernel, need_out=need_out),
        grid=(b, l // tm),
        in_specs=in_specs, out_specs=out_specs, out_shape=out_shape,
        compiler_params=_params("parallel", "parallel"),
        name="inproj",
    )(*args)


def _log_sigmoid(x):
    return jnp.minimum(x, 0.0) - jnp.log1p(jnp.exp(-jnp.abs(x)))


P_KT, P_RT, P_KH, P_BH, P_V = 0, 2, 4, 6, 8
N_PLANES = 9
N_SLOTS = 2


def _co_emit(main, filler, every, start=0):
    for k, _ in enumerate(main):
        if k >= start and (k - start) % every == 0:
            next(filler, None)
    for _ in filler:
        pass


def _scan_kernel(*refs, grid_mode, seq_len, tm, n_tiles, n_steps):
    n_in = len(refs) - 8
    in_refs = refs[:n_in]
    reff_ref, yloc_ref, m_ref, gs_ref, g_ref, bonus_ref, pp_ref, et_ref = refs[n_in:]
    s = pl.program_id(0)
    slot_prep = jnp.bitwise_and(s, 1)
    slot_alg = 1 - slot_prep
    tile = jnp.minimum(s, n_steps - 2) % n_tiles

    @pl.when(s == 0)
    def _():
        pp_ref[1] = jnp.zeros(pp_ref.shape[1:], pp_ref.dtype)
        et_ref[1] = jnp.zeros(et_ref.shape[1:], et_ref.dtype)

    prep = _prep_tile(in_refs, pp_ref, et_ref, slot_prep, g_ref, bonus_ref, tile,
                      grid_mode=grid_mode, seq_len=seq_len, tm=tm)
    alg = _chunk_algebra(pp_ref, et_ref, slot_alg, reff_ref, yloc_ref, m_ref, gs_ref, tc=tm)
    _co_emit(alg, prep, every=3)


def _prep_tile(in_refs, pp_ref, et_ref, slot, g_ref, bonus_ref, tile, *, grid_mode, seq_len, tm):
    sb = min(PREP_ROWS, tm) if grid_mode else tm
    for r0 in range(0, tm, sb):
        yield from _prep_rows(r0, sb, in_refs, pp_ref, et_ref, slot, g_ref, bonus_ref, tile,
                              grid_mode=grid_mode, seq_len=seq_len, tm=tm)


def _prep_rows(r0, sb, in_refs, pp_ref, et_ref, slot, g_ref, bonus_ref, tile, *, grid_mode, seq_len, tm):
    if grid_mode:
        (rw_ref, prev_ref, next_ref, mu_ref, w0_ref, a0_ref, wup_ref, aup_ref, gup_ref,
         kkw_ref, ka_ref, rk_ref, eh_ref) = in_refs
    else:
        (rw_ref, mu_ref, w0_ref, a0_ref, wup_ref, aup_ref, gup_ref,
         kkw_ref, ka_ref, rk_ref, eh_ref) = in_refs
    rows = slice(r0, r0 + sb)
    t_loc = lax.broadcasted_iota(jnp.int32, (sb, 128), 0) + r0
    t_glob = t_loc + tile * tm
    lane = lax.broadcasted_iota(jnp.int32, (sb, 128), 1)
    if grid_mode:
        col = jnp.bitwise_and(t_loc, GRID_W - 1)
        masks = [col != 0, col != GRID_W - 1, t_glob >= GRID_W, t_glob < seq_len - GRID_W]
        n_parts = 4
    else:
        masks = [t_glob != 0, t_glob != seq_len - 1]
        n_parts = 2
    part_w = RWKV_IN // n_parts

    def shifted(j, part):
        cols = slice(128 * j, 128 * (j + 1))
        if part == 0:
            s = pltpu.roll(rw_ref[0, rows, cols], 1, 0)
        elif part == 1:
            s = pltpu.roll(rw_ref[0, rows, cols], sb - 1, 0)
        elif part == 2:
            if r0 > 0:
                s = rw_ref[0, r0 - GRID_W:r0 + sb - GRID_W, cols]
            elif sb == GRID_W:
                s = prev_ref[0, :, cols]
            else:
                s = jnp.concatenate([prev_ref[0, :, cols], rw_ref[0, 0:sb - GRID_W, cols]], axis=0)
        else:
            if r0 + sb < tm:
                s = rw_ref[0, r0 + GRID_W:r0 + sb + GRID_W, cols]
            elif sb == GRID_W:
                s = next_ref[0, :, cols]
            else:
                s = jnp.concatenate([rw_ref[0, r0 + GRID_W:tm, cols], next_ref[0, :, cols]], axis=0)
        return jnp.where(masks[part], s, 0.0)

    blocks = []
    for j in range(RWKV_IN // 128):
        p_lo = (128 * j) // part_w
        p_hi = (128 * j + 127) // part_w
        s = shifted(j, p_lo)
        if p_hi != p_lo:
            s = jnp.where(lane + 128 * j < part_w * p_hi, s, shifted(j, p_hi))
        xj = rw_ref[0, rows, 128 * j:128 * (j + 1)]
        blocks.append(xj + mu_ref[:, 128 * j:128 * (j + 1)] * (s - xj))
        yield

    nb = R_WIDTH // 128
    r = jnp.concatenate(blocks[0:nb], axis=1)
    k = jnp.concatenate(blocks[nb:2 * nb], axis=1)
    v = jnp.concatenate(blocks[2 * nb:3 * nb], axis=1)
    wd, ad, gd = blocks[3 * nb], blocks[3 * nb + 1], blocks[3 * nb + 2]

    tw = _bf(jnp.tanh(wd))
    adb = _bf(ad)
    lw_cols, a_cols = [], []
    for c0 in range(0, 2 * R_WIDTH, GROUP_LANES):
        cols = slice(c0, c0 + GROUP_LANES)
        w_logit = w0_ref[:, cols] + _dot(tw, wup_ref[:, cols])
        lw_cols.append(-jnp.exp(_log_sigmoid(w_logit) - 0.5))
        yield
        a_cols.append(jax.nn.sigmoid(a0_ref[:, cols] + _dot(adb, aup_ref[:, cols])))
        yield
    lw = jnp.concatenate(lw_cols, axis=1)
    a = jnp.concatenate(a_cols, axis=1)
    g_ref[0, rows] = _bf(_dot(_bf(jax.nn.sigmoid(gd)), gup_ref[...]))
    yield

    eh = eh_ref[...]
    kx = k * kkw_ref[...]
    kk = kx / jnp.maximum(jnp.sqrt(_head_sum(kx * kx, eh)), L2_EPS)
    yield
    ka = ka_ref[...]
    kd = (k * (1.0 + (a[:, :R_WIDTH] - 1.0) * ka), k * (1.0 + (a[:, R_WIDTH:] - 1.0) * ka))
    bonus_ref[0, rows] = _bf(_head_sum(r * (kd[0] + kd[1]) * rk_ref[...], eh) * v)
    pp_ref[slot, P_V, rows] = _bf(v)
    yield

    t64 = lax.broadcasted_iota(jnp.int32, (CHUNK, CHUNK), 0)
    s64 = lax.broadcasted_iota(jnp.int32, (CHUNK, CHUNK), 1)
    tri = (jnp.where(s64 <= t64, 1.0, 0.0).astype(BF16), jnp.where(s64 >= t64, 1.0, 0.0).astype(BF16))
    for d in range(2):
        lw_d = lw[:, d * R_WIDTH:(d + 1) * R_WIDTH]
        b_d = kk * a[:, d * R_WIDTH:(d + 1) * R_WIDTH]
        for c0 in range(0, sb, CHUNK):
            loc = slice(c0, c0 + CHUNK)
            dst = slice(r0 + c0, r0 + c0 + CHUNK)
            lwc = lw_d[loc]
            hi, lo = _split2(lwc)
            cum = _dot(tri[d], hi) + _dot(tri[d], lo)
            e_neg = jnp.exp(-cum)
            pp_ref[slot, P_KT + d, dst] = _bf(kk[loc] * jnp.exp(cum - lwc))
            pp_ref[slot, P_RT + d, dst] = _bf(r[loc] * jnp.exp(cum))
            pp_ref[slot, P_KH + d, dst] = _bf(kd[d][loc] * e_neg)
            pp_ref[slot, P_BH + d, dst] = _bf(b_d[loc] * e_neg)
            et = jnp.exp(jnp.sum(lwc, axis=0, keepdims=True))
            et_ref[slot, d, (r0 + c0) // CHUNK] = jnp.broadcast_to(et, (8, R_WIDTH))
            yield


def _scan_pass(rw, grid_mode, layer, mu, w0c, a0c, wup, aup, gup, kkw, ka, rk, eh):
    b, l, _ = rw.shape
    tm = min(256, l) if grid_mode else l
    nt = l // tm
    n_steps = b * nt + 1
    nrow = l // GRID_W
    per = tm // GRID_W
    prep_bi = lambda s: (jnp.minimum(s, n_steps - 2) // nt, jnp.minimum(s, n_steps - 2) % nt)
    alg_bi = lambda s: (jnp.maximum(s - 1, 0) // nt, jnp.maximum(s - 1, 0) % nt)

    def rw_map(s):
        bi, i = prep_bi(s)
        return bi, i, 0

    def prev_map(s):
        bi, i = prep_bi(s)
        return bi, jnp.maximum(i * per - 1, 0), 0

    def next_map(s):
        bi, i = prep_bi(s)
        return bi, jnp.minimum((i + 1) * per, nrow - 1), 0

    def out_map(s):
        bi, i = alg_bi(s)
        return 0, bi, i, 0

    in_specs = [pl.BlockSpec((1, tm, RWKV_IN), rw_map)]
    args = [rw]
    if grid_mode:
        in_specs += [pl.BlockSpec((1, GRID_W, RWKV_IN), prev_map), pl.BlockSpec((1, GRID_W, RWKV_IN), next_map)]
        args += [rw, rw]
    stacked = [mu, w0c, a0c, wup, aup, gup, kkw, ka, rk]
    in_specs += [_layer_spec(c, layer) for c in stacked] + [_const_spec(eh.shape)]
    args += stacked + [eh]
    one = pl.BlockSpec((1, tm, R_WIDTH), rw_map)
    two = pl.BlockSpec((2, 1, tm, R_WIDTH), out_map)
    s1 = jax.ShapeDtypeStruct((b, l, R_WIDTH), BF16)
    s2 = jax.ShapeDtypeStruct((2, b, l, R_WIDTH), F32)
    s2h = jax.ShapeDtypeStruct((2, b, l, R_WIDTH), BF16)
    return pl.pallas_call(
        functools.partial(_scan_kernel, grid_mode=grid_mode, seq_len=l, tm=tm, n_tiles=nt, n_steps=n_steps),
        grid=(n_steps,),
        in_specs=in_specs,
        out_specs=[two, two, two, two, one, one],
        out_shape=[s2h, s2h, s2, s2, s1, s1],
        scratch_shapes=[pltpu.VMEM((N_SLOTS, N_PLANES, tm, R_WIDTH), BF16),
                        pltpu.VMEM((N_SLOTS, 2, tm // CHUNK, 8, R_WIDTH), F32)],
        compiler_params=_params("arbitrary"),
        name="rwkv_scan",
    )(*args)


def _block_diag_mask():
    return lax.broadcasted_iota(jnp.int32, (CHUNK, LANE_TILE), 1) < HEAD_DIM


def _block_diag(x, first_head):
    zero = jnp.zeros((CHUNK, LANE_TILE), x.dtype)
    per_tile = LANE_TILE // HEAD_DIM
    blocks = []
    for h in range(HEADS_PER_GROUP):
        tile = h // per_tile
        piece = x[:, tile * LANE_TILE:(tile + 1) * LANE_TILE]
        piece = jnp.where(first_head, piece, zero) if h % per_tile == 0 else jnp.where(first_head, zero, piece)
        row = [zero] * (GROUP_LANES // LANE_TILE)
        row[tile] = piece
        blocks.append(jnp.concatenate(row, axis=1))
    return jnp.concatenate(blocks, axis=0)


def _chunk_algebra(pp_ref, et_ref, slot, reff_ref, yloc_ref, m_ref, gs_ref, *, tc):
    t = lax.broadcasted_iota(jnp.int32, (CHUNK, GROUP_LANES), 0)
    lane = lax.broadcasted_iota(jnp.int32, (CHUNK, GROUP_LANES), 1)
    s = jnp.bitwise_and(lane, CHUNK - 1)
    eye = jnp.where(s == t, 1.0, 0.0)
    before = (s < t, s > t)
    upto = (s <= t, s >= t)
    bdmask = _block_diag_mask()
    bdiag = lambda x: _block_diag(_bf(x), bdmask)
    units = [(d, slice(ci * CHUNK, (ci + 1) * CHUNK), slice(gi * GROUP_LANES, (gi + 1) * GROUP_LANES))
             for d in range(2) for ci in range(tc // CHUNK) for gi in range(N_GROUPS)]
    nu = range(len(units))
    dirs = [d for d, _, _ in units]

    def staged(fn):
        out = []
        for u in nu:
            out.append(fn(u))
            yield
        return out

    kt = [pp_ref[slot, P_KT + d, rows, lanes] for d, rows, lanes in units]
    rt = [pp_ref[slot, P_RT + d, rows, lanes] for d, rows, lanes in units]
    vv = [pp_ref[slot, P_V, rows, lanes] for d, rows, lanes in units]
    dec = [eye * et_ref[slot, d, rows.start // CHUNK, 0:1, lanes] for d, rows, lanes in units]

    def nt_products(plane):
        def fn(u):
            d, rows, lanes = units[u]
            dh, dl = _split2(dec[u])
            lhs = jnp.concatenate([kt[u], rt[u], dh, dl], axis=0)
            return _dot_nt(lhs, _block_diag(pp_ref[slot, plane + d, rows, lanes], bdmask))
        return fn

    sk = yield from staged(nt_products(P_KH))
    sb = yield from staged(nt_products(P_BH))
    a_k = [jnp.where(before[dirs[u]], sk[u][:CHUNK], 0.0) for u in nu]
    a_rk = [jnp.where(upto[dirs[u]], sk[u][CHUNK:2 * CHUNK], 0.0) for u in nu]
    kb_t = [sk[u][2 * CHUNK:3 * CHUNK] + sk[u][3 * CHUNK:] for u in nu]
    n = [jnp.where(before[dirs[u]], -sb[u][:CHUNK], 0.0) for u in nu]
    a_rb = [jnp.where(upto[dirs[u]], sb[u][CHUNK:2 * CHUNK], 0.0) for u in nu]
    bb_t = [sb[u][2 * CHUNK:3 * CHUNK] + sb[u][3 * CHUNK:] for u in nu]

    tinv = [eye + n[u] for u in nu]
    p = yield from staged(lambda u: _dot(_bf(n[u]), bdiag(n[u])))
    av = yield from staged(
        lambda u: _dot(_bf(jnp.concatenate([a_k[u], a_rk[u], kb_t[u]], axis=0)), _block_diag(vv[u], bdmask)))
    for _ in range(4):
        tp = yield from staged(lambda u: _dot(_bf(jnp.concatenate([tinv[u], p[u]], axis=0)), bdiag(p[u])))
        tinv = [tinv[u] + tp[u][:CHUNK] for u in nu]
        p = [tp[u][CHUNK:] for u in nu]
    tb = yield from staged(lambda u: _bf(tinv[u] + _dot(_bf(tinv[u]), bdiag(p[u]))))

    uk = yield from staged(lambda u: _dot(tb[u], _block_diag(kt[u], bdmask)))
    uv = yield from staged(lambda u: _dot(tb[u], bdiag(av[u][:CHUNK])))
    rows_b = [_bf(jnp.concatenate([a_rb[u], bb_t[u]], axis=0)) for u in nu]
    bk = yield from staged(lambda u: _dot(rows_b[u], bdiag(uk[u])))
    bv = yield from staged(lambda u: _dot(rows_b[u], bdiag(uv[u])))
    for u, (d, rows, lanes) in enumerate(units):
        reff_ref[d, 0, rows, lanes] = _bf(rt[u].astype(F32) - bk[u][:CHUNK])
        yloc_ref[d, 0, rows, lanes] = _bf(av[u][CHUNK:2 * CHUNK] - bv[u][:CHUNK])
        m_ref[d, 0, rows, lanes] = dec[u] - bk[u][CHUNK:]
        gs_ref[d, 0, rows, lanes] = av[u][2 * CHUNK:] - bv[u][CHUNK:]
        yield


def _state_steps(m0_ref, g0_ref, re0_ref, yl0_ref, m1_ref, g1_ref, re1_ref, yl1_ref, s0_ref,
                 y0_ref, y1_ref, h_ref, *, batch):
    c = pl.program_id(0)

    @pl.when(c == 0)
    def _():
        h_ref[...] = s0_ref[...]

    bdmask = _block_diag_mask()
    per_dir = ((m0_ref, g0_ref, re0_ref, yl0_ref, y0_ref), (m1_ref, g1_ref, re1_ref, yl1_ref, y1_ref))
    units = [(d, bi, slice(gi * GROUP_LANES, (gi + 1) * GROUP_LANES))
             for d in range(2) for bi in range(batch) for gi in range(N_GROUPS)]
    hs = [_split2(h_ref[d, bi, :, lanes]) for d, bi, lanes in units]
    ms = [_split2(per_dir[d][0][0, bi, :, lanes]) for d, bi, lanes in units]
    rs = [per_dir[d][2][0, bi, :, lanes] for d, bi, lanes in units]
    o1, o2 = [], []
    for u in range(len(units)):
        o1.append(_dot(jnp.concatenate([ms[u][0], ms[u][1], rs[u]], axis=0), _block_diag(hs[u][0], bdmask)))
        yield
    for u in range(len(units)):
        o2.append(_dot(jnp.concatenate([ms[u][0], rs[u]], axis=0), _block_diag(hs[u][1], bdmask)))
        yield
    for u, (d, bi, lanes) in enumerate(units):
        _, g_ref, _, yl_ref, y_ref = per_dir[d]
        mh_new = o1[u][0:CHUNK] + o1[u][CHUNK:2 * CHUNK] + o2[u][0:CHUNK]
        rh_new = o1[u][2 * CHUNK:] + o2[u][CHUNK:]
        y_ref[bi, :, lanes] = _bf(yl_ref[0, bi, :, lanes].astype(F32) + rh_new)
        h_ref[d, bi, :, lanes] = mh_new + g_ref[0, bi, :, lanes]
        yield


def _state_kernel(*refs, batch):
    for _ in _state_steps(*refs, batch=batch):
        pass


def _state_pass(m, g, reff, yloc, s0):
    _, b, l, _ = m.shape
    nc = l // CHUNK
    blk = (1, b, CHUNK, R_WIDTH)
    fwd = pl.BlockSpec(blk, lambda c: (0, 0, c, 0))
    bwd = pl.BlockSpec(blk, lambda c: (1, 0, nc - 1 - c, 0))
    st = pl.BlockSpec((2, b, HEAD_DIM, R_WIDTH), lambda c: (0, 0, 0, 0))
    sy = jax.ShapeDtypeStruct((b, l, R_WIDTH), BF16)
    return pl.pallas_call(
        functools.partial(_state_kernel, batch=b),
        grid=(nc,),
        in_specs=[fwd, fwd, fwd, fwd, bwd, bwd, bwd, bwd, st],
        out_specs=[pl.BlockSpec((b, CHUNK, R_WIDTH), lambda c: (0, c, 0)),
                   pl.BlockSpec((b, CHUNK, R_WIDTH), lambda c: (0, nc - 1 - c, 0)),
                   st],
        out_shape=[sy, sy, jax.ShapeDtypeStruct((2, b, HEAD_DIM, R_WIDTH), F32)],
        compiler_params=_params("arbitrary"),
        name="rwkv_state",
    )(m, g, reff, yloc, m, g, reff, yloc, s0)


def _posdft_kernel(cs_ref, z_ref, o_ref):
    l = z_ref.shape[2]
    z = z_ref[0].reshape(2 * l, F_WIDTH)
    o_ref[0] = _bf(_dot(cs_ref[...], z))


def _posdft_fold_steps(w_ref, rev_ref, z_ref, o_ref, f_ref, i, *, tm):
    l = z_ref.shape[2]
    half = l // 2
    nb = l // REV_BLOCK

    @pl.when(i == 0)
    def _():
        for part in range(2):
            for j in range(half // REV_BLOCK):
                blk = z_ref[0, part, j * REV_BLOCK:(j + 1) * REV_BLOCK, :]
                src_a = z_ref[0, part, l - (j + 1) * REV_BLOCK:l - j * REV_BLOCK, :]
                jb = (nb - j) % nb
                src_b = z_ref[0, part, jb * REV_BLOCK:(jb + 1) * REV_BLOCK, :]
                rev = _dot(rev_ref[0], src_a) + _dot(rev_ref[1], src_b)
                sgn = 1.0 if part == 0 else -1.0
                f_ref[part * half + j * REV_BLOCK:part * half + (j + 1) * REV_BLOCK, :] = _bf(
                    blk.astype(F32) + sgn * rev)

    mid = z_ref[0, 0, half:half + 1, :].astype(F32) * (1.0 / float(np.sqrt(l)))
    piece = min(DFT_PIECE, tm)
    for r0 in range(0, tm, piece):
        row = lax.broadcasted_iota(jnp.int32, (piece, F_WIDTH), 0) + r0
        alt = (1 - 2 * jnp.bitwise_and(row, 1)).astype(F32)
        o_ref[0, r0:r0 + piece] = _bf(_dot(w_ref[r0:r0 + piece, :], f_ref[...]) + alt * mid)
        yield


def _posdft_fold_kernel(w_ref, rev_ref, z_ref, o_ref, f_ref, *, tm):
    for _ in _posdft_fold_steps(w_ref, rev_ref, z_ref, o_ref, f_ref, pl.program_id(1), tm=tm):
        pass


def _state_dft_kernel(*refs, batch, tm, n_tiles):
    state_refs = refs[:9] + refs[12:15]
    w_ref, rev_ref, z_ref = refs[9:12]
    o_ref, f_ref = refs[15:]
    i = pl.program_id(0) % n_tiles
    dft = _posdft_fold_steps(w_ref, rev_ref, z_ref, o_ref, f_ref, i, tm=tm)
    state_yields = 3 * 2 * batch * N_GROUPS
    dft_yields = -(-tm // DFT_PIECE)
    _co_emit(_state_steps(*state_refs, batch=batch), dft, every=max(1, state_yields // dft_yields))


def _state_dft_pass(m, g, reff, yloc, s0, z, cs):
    _, b, l, _ = m.shape
    nc = l // CHUNK
    tm = (b * l) // nc
    nt = l // tm
    w, rev = cs
    blk = (1, b, CHUNK, R_WIDTH)
    fwd = pl.BlockSpec(blk, lambda c: (0, 0, c, 0))
    bwd = pl.BlockSpec(blk, lambda c: (1, 0, nc - 1 - c, 0))
    st = pl.BlockSpec((2, b, HEAD_DIM, R_WIDTH), lambda c: (0, 0, 0, 0))
    sy = jax.ShapeDtypeStruct((b, l, R_WIDTH), BF16)
    return pl.pallas_call(
        functools.partial(_state_dft_kernel, batch=b, tm=tm, n_tiles=nt),
        grid=(nc,),
        in_specs=[fwd, fwd, fwd, fwd, bwd, bwd, bwd, bwd, st,
                  pl.BlockSpec((tm, l), lambda c: (c % nt, 0)),
                  _const_spec(rev.shape),
                  pl.BlockSpec((1, 2, l, F_WIDTH), lambda c: (c // nt, 0, 0, 0))],
        out_specs=[pl.BlockSpec((b, CHUNK, R_WIDTH), lambda c: (0, c, 0)),
                   pl.BlockSpec((b, CHUNK, R_WIDTH), lambda c: (0, nc - 1 - c, 0)),
                   st,
                   pl.BlockSpec((1, tm, F_WIDTH), lambda c: (c // nt, c % nt, 0))],
        out_shape=[sy, sy, jax.ShapeDtypeStruct((2, b, HEAD_DIM, R_WIDTH), F32),
                   jax.ShapeDtypeStruct((b, l, F_WIDTH), BF16)],
        scratch_shapes=[pltpu.VMEM((l, F_WIDTH), BF16)],
        compiler_params=_params("arbitrary"),
        name="rwkv_state_dft",
    )(m, g, reff, yloc, m, g, reff, yloc, s0, w, rev, z)


def _can_fuse_state_dft(b, l, cs):
    tm = b * CHUNK
    return isinstance(cs, tuple) and l % tm == 0 and tm % 16 == 0


def _pos_dft(z, cs):
    b, _, l, _ = z.shape
    tm = min(512, l)
    if isinstance(cs, tuple):
        w, rev = cs
        return pl.pallas_call(
            functools.partial(_posdft_fold_kernel, tm=tm),
            grid=(b, l // tm),
            in_specs=[pl.BlockSpec((tm, l), lambda bi, i: (i, 0)),
                      _const_spec(rev.shape),
                      pl.BlockSpec((1, 2, l, F_WIDTH), lambda bi, i: (bi, 0, 0, 0))],
            out_specs=pl.BlockSpec((1, tm, F_WIDTH), lambda bi, i: (bi, i, 0)),
            out_shape=jax.ShapeDtypeStruct((b, l, F_WIDTH), BF16),
            scratch_shapes=[pltpu.VMEM((l, F_WIDTH), BF16)],
            compiler_params=_params("parallel", "arbitrary"),
            name="pos_dft_fold",
        )(w, rev, z)
    return pl.pallas_call(
        _posdft_kernel,
        grid=(b, l // tm),
        in_specs=[pl.BlockSpec((tm, 2 * l), lambda bi, i: (i, 0)),
                  pl.BlockSpec((1, 2, l, F_WIDTH), lambda bi, i: (bi, 0, 0, 0))],
        out_specs=pl.BlockSpec((1, tm, F_WIDTH), lambda bi, i: (bi, i, 0)),
        out_shape=jax.ShapeDtypeStruct((b, l, F_WIDTH), BF16),
        compiler_params=_params("parallel", "parallel"),
        name="pos_dft",
    )(cs, z)


def _merge_mlp_kernel(f_ref, y0_ref, y1_ref, g_ref, bonus_ref, gates_ref, x_ref, mod_ref,
                      lnw_ref, lnb_ref, n2_ref, nf_ref, eh_ref, wf_ref, wr_ref, wo_ref, w1_ref, w2_ref,
                      o_ref, *, final_norm):
    y = y0_ref[0].astype(F32) + y1_ref[0].astype(F32)
    eh = eh_ref[...]
    inv_n = 1.0 / HEAD_DIM
    mu = _head_sum(y, eh) * inv_n
    dlt = y - mu
    var = _head_sum(dlt * dlt, eh) * inv_n
    yn = dlt * lax.rsqrt(var + GN_EPS) * lnw_ref[...] + lnb_ref[...]
    rwkv = _bf((yn + bonus_ref[0].astype(F32)) * g_ref[0].astype(F32))
    fo = _dot(f_ref[0], wf_ref[...])
    ro = _dot(rwkv, wr_ref[...])
    gates = gates_ref[0].astype(F32)
    merged = jax.nn.sigmoid(gates[:, :D_MODEL]) * fo + jax.nn.sigmoid(gates[:, D_MODEL:]) * ro
    x = x_ref[0] + mod_ref[0, 2:3, :] * _dot(_bf(merged), wo_ref[...])
    h = _bf(_rms(x) * n2_ref[...] * (1.0 + mod_ref[0, 4:5, :]) + mod_ref[0, 3:4, :])
    acc = jnp.zeros(x.shape, F32)
    step = 1024
    for j in range(D_FF // step):
        u = jnp.maximum(_dot(h, w1_ref[:, j * step:(j + 1) * step]), 0.0)
        acc = acc + _dot(_bf(u * u), w2_ref[j * step:(j + 1) * step, :])
    x2 = x + mod_ref[0, 5:6, :] * acc
    if final_norm:
        x2 = _rms(x2) * nf_ref[...]
    o_ref[0] = x2


def _merge_mlp(f, y0, y1, g, bonus, gates, x, mod, mod_row, layer, lnw, lnb, n2, nf, eh, wf, wr, wo, w1, w2,
               final_norm):
    b, l, d = x.shape
    tm = min(256, l)
    row = lambda w: pl.BlockSpec((1, tm, w), lambda bi, i: (bi, i, 0))
    once = lambda a: pl.BlockSpec(a.shape, lambda bi, i: (0, 0), pipeline_mode=pl.Buffered(1))
    stacked = lambda a: _layer_spec(a, layer, single_buffer=True)
    return pl.pallas_call(
        functools.partial(_merge_mlp_kernel, final_norm=final_norm),
        grid=(b, l // tm),
        in_specs=[row(F_WIDTH), row(R_WIDTH), row(R_WIDTH), row(R_WIDTH), row(R_WIDTH), row(GATE_W), row(d),
                  _mod_spec(mod, layer, mod_row),
                  stacked(lnw), stacked(lnb), stacked(n2), once(nf), once(eh),
                  stacked(wf), stacked(wr), stacked(wo), stacked(w1), stacked(w2)],
        out_specs=row(d),
        out_shape=jax.ShapeDtypeStruct((b, l, d), F32),
        compiler_params=_params("parallel", "parallel"),
        name="merge_mlp",
    )(f, y0, y1, g, bonus, gates, x, mod, lnw, lnb, n2, nf, eh, wf, wr, wo, w1, w2)


def _channel_dft():
    n = FGROUP_DIM
    jk = np.outer(np.arange(n), np.arange(n)) % n
    ang = 2.0 * np.pi * jk / n
    c = np.cos(ang) / np.sqrt(n)
    s = np.sin(ang) / np.sqrt(n)
    g = F_WIDTH // n
    out = np.zeros((F_WIDTH, 2 * F_WIDTH), np.float32)
    for i in range(g):
        out[i * n:(i + 1) * n, i * n:(i + 1) * n] = c
        out[i * n:(i + 1) * n, F_WIDTH + i * n:F_WIDTH + (i + 1) * n] = s
    return jnp.asarray(out)


def _position_dft(l):
    fold = (l // 2) % REV_BLOCK == 0
    nk = l // 2 if fold else l
    jk = np.outer(np.arange(l), np.arange(nk)) % l
    ang = 2.0 * np.pi * jk / l
    c = np.cos(ang) / np.sqrt(l)
    if fold:
        c[:, 0] *= 0.5
    cs = jnp.asarray(np.concatenate([c, -np.sin(ang) / np.sqrt(l)], axis=1).astype(np.float32)).astype(BF16)
    if not fold:
        return cs
    rev = np.zeros((2, REV_BLOCK, REV_BLOCK), np.float32)
    idx = np.arange(1, REV_BLOCK)
    rev[0, idx, REV_BLOCK - idx] = 1.0
    rev[1, 0, 0] = 1.0
    return cs, jnp.asarray(rev).astype(BF16)


def _head_ones():
    h = np.arange(R_WIDTH) // HEAD_DIM
    return jnp.asarray((h[:, None] == h[None, :]).astype(np.float32)).astype(BF16)


def _two_dir_lora(w):
    z = jnp.zeros_like(w[:, 0])
    return _bf(jnp.concatenate([jnp.concatenate([w[:, 0], z], axis=2),
                                jnp.concatenate([z, w[:, 1]], axis=2)], axis=1))


def _layer(x, mod_row, layer, grid_mode, need_out, final_norm, s0, p):
    rw_out = _inproj(x, p["mod"], mod_row, layer, p["n1"], p["win"], p["wz"][layer], need_out)
    reff, yloc, m, gs, g, bonus = _scan_pass(rw_out[0], grid_mode, layer, p["mu"], p["w0"], p["a0"], p["wup"],
                                             p["aup"], p["gup"], p["kkw"], p["ka"], p["rk"], p["eh"])
    if not need_out:
        return None, _state_pass(m, gs, reff, yloc, s0)[2]
    z, gates = rw_out[1], rw_out[2]
    cs = p["pos_dft"][x.shape[1]]
    if _can_fuse_state_dft(x.shape[0], x.shape[1], cs):
        y0, y1, s_fin, f = _state_dft_pass(m, gs, reff, yloc, s0, z, cs)
    else:
        y0, y1, s_fin = _state_pass(m, gs, reff, yloc, s0)
        f = _pos_dft(z, cs)
    x2 = _merge_mlp(f, y0, y1, g, bonus, gates, x, p["mod"], mod_row, layer, p["lnw"], p["lnb"], p["n2"],
                    p["nf"], p["eh"], p["wf"], p["wr"], p["wo"], p["w1"], p["w2"], final_norm)
    return x2, s_fin


def kernel(x, c, ctx, c_ctx, w_mod, b_mod, norm1, norm2, w_in, mu_shift, w0, w_up, a0, a_up, g_up,
           k_k, k_a, r_k, ln_x_w, ln_x_b, w_fourier_up, w_rwkv_up, w_out, mlp_w1, mlp_w2, norm_f):
    depth = w_mod.shape[0]
    batch, seq, d = x.shape
    ctx_len = ctx.shape[1]
    assert d == D_MODEL and batch + 1 <= MOD_ROWS
    assert seq % GRID_W == 0 and ctx_len % CHUNK == 0 and seq % CHUNK == 0

    cc = jnp.zeros((MOD_ROWS, d), F32).at[:batch].set(c).at[batch].set(c_ctx)
    cdft = _channel_dft()
    rows = lambda a: a.reshape(depth, 1, -1)
    p = {
        "eh": _head_ones(),
        "pos_dft": {n: _position_dft(n) for n in {seq, ctx_len}},
        "nf": norm_f.reshape(1, -1),
        "mod": _modulation(cc, w_mod, b_mod).reshape(depth, MOD_ROWS, N_MOD, d),
        "wz": [_fold_channel_dft(w_in, layer, cdft) for layer in range(depth)],
        "win": _bf(w_in),
        "n1": rows(norm1), "n2": rows(norm2), "mu": rows(mu_shift), "w0": rows(w0), "a0": rows(a0),
        "wup": _two_dir_lora(w_up), "aup": _two_dir_lora(a_up), "gup": _bf(g_up),
        "kkw": rows(k_k), "ka": rows(k_a), "rk": rows(r_k), "lnw": rows(ln_x_w), "lnb": rows(ln_x_b),
        "wf": _bf(w_fourier_up), "wr": _bf(w_rwkv_up), "wo": _bf(w_out),
        "w1": _bf(mlp_w1), "w2": _bf(mlp_w2),
    }
    lat_row = lambda bi: bi
    ctx_row = lambda bi: batch

    x_lat, x_ctx = x, ctx
    s_zero = jnp.zeros((2, batch, HEAD_DIM, R_WIDTH), F32)
    for layer in range(depth):
        last = layer == depth - 1
        x_ctx, s_ctx = _layer(x_ctx, ctx_row, layer, False, not last, False, s_zero, p)
        x_lat, _ = _layer(x_lat, lat_row, layer, True, True, last, s_ctx, p)
    return x_lat
```

```python
import functools

import numpy as np
import jax
import jax.numpy as jnp
from jax import lax
from jax.experimental import pallas as pl
from jax.experimental.pallas import tpu as pltpu

F32 = jnp.float32
BF16 = jnp.bfloat16

D_MODEL = 1024
GRID_W = 64
F_WIDTH = 512
FGROUP_DIM = 128
HEAD_DIM = 64
N_RHEADS = 8
R_WIDTH = N_RHEADS * HEAD_DIM
D_LORA = 64
D_GATE_LORA = 128
RWKV_IN = 3 * R_WIDTH + 4 * D_LORA + D_GATE_LORA
GATE_W = 2 * D_MODEL
D_FF = 4 * D_MODEL
N_MOD = 6
NORM_EPS = 1e-6
GN_EPS = 64e-5
L2_EPS = 1e-12

CHUNK = 64
PREP_ROWS = 128
GROUP_LANES = 256
HEADS_PER_GROUP = GROUP_LANES // HEAD_DIM
LANE_TILE = 128
REV_BLOCK = 256
DFT_PIECE = 128
N_GROUPS = R_WIDTH // GROUP_LANES
MOD_ROWS = 16
VMEM_LIMIT = 56 * 1024 * 1024


def _bf(x):
    return x.astype(BF16)


def _dot(a, b):
    return jnp.dot(a, b, preferred_element_type=F32)


def _dot_nt(a, b):
    return lax.dot_general(a, b, (((1,), (1,)), ((), ())), preferred_element_type=F32)


def _split2(x):
    hi = _bf(x)
    lo = _bf(x - hi.astype(F32))
    return hi, lo


def _head_sum(x, ones):
    return _dot(_bf(x), ones)


def _params(*sem):
    return pltpu.CompilerParams(dimension_semantics=sem, vmem_limit_bytes=VMEM_LIMIT)


def _const_spec(shape):
    zeros = (0,) * len(shape)
    return pl.BlockSpec(shape, lambda *_: zeros)


def _layer_spec(a, l, single_buffer=False):
    tail = tuple(a.shape[1:])
    idx = (l,) + (0,) * len(tail)
    kw = {"pipeline_mode": pl.Buffered(1)} if single_buffer else {}
    return pl.BlockSpec((None,) + tail, lambda *_: idx, **kw)


def _mod_spec(mod, l, mod_row):
    return pl.BlockSpec((None, 1) + tuple(mod.shape[2:]), lambda bi, i: (l, mod_row(bi), 0, 0))


def _modulation_kernel(x_ref, w_ref, b_ref, o_ref):
    x = x_ref[...]
    x = x * jax.nn.sigmoid(x)
    o_ref[0] = _dot(_bf(x), _bf(w_ref[0])) + b_ref[0]


def _modulation(cc, w, b):
    depth, d, n = w.shape
    tn = 512
    return pl.pallas_call(
        _modulation_kernel,
        grid=(depth, n // tn),
        in_specs=[_const_spec(cc.shape),
                  pl.BlockSpec((1, d, tn), lambda l, j: (l, 0, j)),
                  pl.BlockSpec((1, 1, tn), lambda l, j: (l, 0, j))],
        out_specs=pl.BlockSpec((1, cc.shape[0], tn), lambda l, j: (l, 0, j)),
        out_shape=jax.ShapeDtypeStruct((depth, cc.shape[0], n), F32),
        compiler_params=_params("parallel", "parallel"),
        name="modulation",
    )(cc, w, b.reshape(depth, 1, n))


def _fold_kernel(w_ref, c_ref, o_ref):
    wh, wl = _split2(w_ref[...])
    ch, cl = _split2(c_ref[...])
    o_ref[...] = _bf(_dot(wh, ch) + _dot(wl, ch) + _dot(wh, cl))


def _fold_channel_dft(w_in, l, cdft):
    d = w_in.shape[1]
    return pl.pallas_call(
        _fold_kernel,
        grid=(1,),
        in_specs=[pl.BlockSpec((None, d, F_WIDTH), lambda i: (l, 0, 0)), _const_spec(cdft.shape)],
        out_specs=_const_spec((d, cdft.shape[1])),
        out_shape=jax.ShapeDtypeStruct((d, cdft.shape[1]), BF16),
        compiler_params=_params("arbitrary"),
        name="fold_channel_dft",
    )(w_in, cdft)


def _rms(x):
    return x * lax.rsqrt(jnp.mean(x * x, axis=-1, keepdims=True) + NORM_EPS)


def _inproj_kernel(x_ref, mod_ref, n1_ref, win_ref, *rest, need_out):
    if need_out:
        wz_ref, rw_ref, z_ref, g_ref = rest
    else:
        (rw_ref,) = rest
    x = x_ref[0]
    shift = mod_ref[0, 0:1, :]
    scale = mod_ref[0, 1:2, :]
    h = _bf(_rms(x) * n1_ref[...] * (1.0 + scale) + shift)
    rw_ref[0] = _dot(h, win_ref[:, F_WIDTH:F_WIDTH + RWKV_IN])
    if need_out:
        z = _dot(h, wz_ref[...])
        z_ref[0, 0] = _bf(z[:, :F_WIDTH])
        z_ref[0, 1] = _bf(z[:, F_WIDTH:])
        g_ref[0] = _bf(_dot(h, win_ref[:, F_WIDTH + RWKV_IN:]))


def _inproj(x, mod, mod_row, layer, n1, win, wz, need_out):
    b, l, d = x.shape
    tm = min(512, l)
    row_spec = lambda w: pl.BlockSpec((1, tm, w), lambda bi, i: (bi, i, 0))
    in_specs = [row_spec(d), _mod_spec(mod, layer, mod_row), _layer_spec(n1, layer),
                _layer_spec(win, layer, single_buffer=True)]
    args = [x, mod, n1, win]
    out_specs = [row_spec(RWKV_IN)]
    out_shape = [jax.ShapeDtypeStruct((b, l, RWKV_IN), F32)]
    if need_out:
        in_specs += [_const_spec(wz.shape)]
        args += [wz]
        out_specs += [pl.BlockSpec((1, 2, tm, F_WIDTH), lambda bi, i: (bi, 0, i, 0)), row_spec(GATE_W)]
        out_shape += [jax.ShapeDtypeStruct((b, 2, l, F_WIDTH), BF16),
                      jax.ShapeDtypeStruct((b, l, GATE_W), BF16)]
    return pl.pallas_call(
        functools.partial(_inproj_kernel, need_out=need_out),
        grid=(b, l // tm),
        in_specs=in_specs, out_specs=out_specs, out_shape=out_shape,
        compiler_params=_params("parallel", "parallel"),
        name="inproj",
    )(*args)


def _log_sigmoid(x):
    return jnp.minimum(x, 0.0) - jnp.log1p(jnp.exp(-jnp.abs(x)))


P_KT, P_RT, P_KH, P_BH, P_V = 0, 2, 4, 6, 8
N_PLANES = 9
N_SLOTS = 2


def _co_emit(main, filler, every, start=0):
    for k, _ in enumerate(main):
        if k >= start and (k - start) % every == 0:
            next(filler, None)
    for _ in filler:
        pass


def _scan_kernel(*refs, grid_mode, seq_len, tm, n_tiles, n_steps):
    n_in = len(refs) - 8
    in_refs = refs[:n_in]
    reff_ref, yloc_ref, m_ref, gs_ref, g_ref, bonus_ref, pp_ref, et_ref = refs[n_in:]
    s = pl.program_id(0)
    slot_prep = jnp.bitwise_and(s, 1)
    slot_alg = 1 - slot_prep
    tile = jnp.minimum(s, n_steps - 2) % n_tiles

    @pl.when(s == 0)
    def _():
        pp_ref[1] = jnp.zeros(pp_ref.shape[1:], pp_ref.dtype)
        et_ref[1] = jnp.zeros(et_ref.shape[1:], et_ref.dtype)

    prep = _prep_tile(in_refs, pp_ref, et_ref, slot_prep, g_ref, bonus_ref, tile,
                      grid_mode=grid_mode, seq_len=seq_len, tm=tm)
    alg = _chunk_algebra(pp_ref, et_ref, slot_alg, reff_ref, yloc_ref, m_ref, gs_ref, tc=tm)
    _co_emit(alg, prep, every=3)


def _prep_tile(in_refs, pp_ref, et_ref, slot, g_ref, bonus_ref, tile, *, grid_mode, seq_len, tm):
    sb = min(PREP_ROWS, tm) if grid_mode else tm
    for r0 in range(0, tm, sb):
        yield from _prep_rows(r0, sb, in_refs, pp_ref, et_ref, slot, g_ref, bonus_ref, tile,
                              grid_mode=grid_mode, seq_len=seq_len, tm=tm)


def _prep_rows(r0, sb, in_refs, pp_ref, et_ref, slot, g_ref, bonus_ref, tile, *, grid_mode, seq_len, tm):
    if grid_mode:
        (rw_ref, prev_ref, next_ref, mu_ref, w0_ref, a0_ref, wup_ref, aup_ref, gup_ref,
         kkw_ref, ka_ref, rk_ref, eh_ref) = in_refs
    else:
        (rw_ref, mu_ref, w0_ref, a0_ref, wup_ref, aup_ref, gup_ref,
         kkw_ref, ka_ref, rk_ref, eh_ref) = in_refs
    rows = slice(r0, r0 + sb)
    t_loc = lax.broadcasted_iota(jnp.int32, (sb, 128), 0) + r0
    t_glob = t_loc + tile * tm
    lane = lax.broadcasted_iota(jnp.int32, (sb, 128), 1)
    if grid_mode:
        col = jnp.bitwise_and(t_loc, GRID_W - 1)
        masks = [col != 0, col != GRID_W - 1, t_glob >= GRID_W, t_glob < seq_len - GRID_W]
        n_parts = 4
    else:
        masks = [t_glob != 0, t_glob != seq_len - 1]
        n_parts = 2
    part_w = RWKV_IN // n_parts

    def shifted(j, part):
        cols = slice(128 * j, 128 * (j + 1))
        if part == 0:
            s = pltpu.roll(rw_ref[0, rows, cols], 1, 0)
        elif part == 1:
            s = pltpu.roll(rw_ref[0, rows, cols], sb - 1, 0)
        elif part == 2:
            if r0 > 0:
                s = rw_ref[0, r0 - GRID_W:r0 + sb - GRID_W, cols]
            elif sb == GRID_W:
                s = prev_ref[0, :, cols]
            else:
                s = jnp.concatenate([prev_ref[0, :, cols], rw_ref[0, 0:sb - GRID_W, cols]], axis=0)
        else:
            if r0 + sb < tm:
                s = rw_ref[0, r0 + GRID_W:r0 + sb + GRID_W, cols]
            elif sb == GRID_W:
                s = next_ref[0, :, cols]
            else:
                s = jnp.concatenate([rw_ref[0, r0 + GRID_W:tm, cols], next_ref[0, :, cols]], axis=0)
        return jnp.where(masks[part], s, 0.0)

    blocks = []
    for j in range(RWKV_IN // 128):
        p_lo = (128 * j) // part_w
        p_hi = (128 * j + 127) // part_w
        s = shifted(j, p_lo)
        if p_hi != p_lo:
            s = jnp.where(lane + 128 * j < part_w * p_hi, s, shifted(j, p_hi))
        xj = rw_ref[0, rows, 128 * j:128 * (j + 1)]
        blocks.append(xj + mu_ref[:, 128 * j:128 * (j + 1)] * (s - xj))
        yield

    nb = R_WIDTH // 128
    r = jnp.concatenate(blocks[0:nb], axis=1)
    k = jnp.concatenate(blocks[nb:2 * nb], axis=1)
    v = jnp.concatenate(blocks[2 * nb:3 * nb], axis=1)
    wd, ad, gd = blocks[3 * nb], blocks[3 * nb + 1], blocks[3 * nb + 2]

    tw = _bf(jnp.tanh(wd))
    adb = _bf(ad)
    lw_cols, a_cols = [], []
    for c0 in range(0, 2 * R_WIDTH, GROUP_LANES):
        cols = slice(c0, c0 + GROUP_LANES)
        w_logit = w0_ref[:, cols] + _dot(tw, wup_ref[:, cols])
        lw_cols.append(-jnp.exp(_log_sigmoid(w_logit) - 0.5))
        yield
        a_cols.append(jax.nn.sigmoid(a0_ref[:, cols] + _dot(adb, aup_ref[:, cols])))
        yield
    lw = jnp.concatenate(lw_cols, axis=1)
    a = jnp.concatenate(a_cols, axis=1)
    g_ref[0, rows] = _bf(_dot(_bf(jax.nn.sigmoid(gd)), gup_ref[...]))
    yield

    eh = eh_ref[...]
    kx = k * kkw_ref[...]
    kk = kx / jnp.maximum(jnp.sqrt(_head_sum(kx * kx, eh)), L2_EPS)
    yield
    ka = ka_ref[...]
    kd = (k * (1.0 + (a[:, :R_WIDTH] - 1.0) * ka), k * (1.0 + (a[:, R_WIDTH:] - 1.0) * ka))
    bonus_ref[0, rows] = _bf(_head_sum(r * (kd[0] + kd[1]) * rk_ref[...], eh) * v)
    pp_ref[slot, P_V, rows] = _bf(v)
    yield

    t64 = lax.broadcasted_iota(jnp.int32, (CHUNK, CHUNK), 0)
    s64 = lax.broadcasted_iota(jnp.int32, (CHUNK, CHUNK), 1)
    tri = (jnp.where(s64 <= t64, 1.0, 0.0).astype(BF16), jnp.where(s64 >= t64, 1.0, 0.0).astype(BF16))
    for d in range(2):
        lw_d = lw[:, d * R_WIDTH:(d + 1) * R_WIDTH]
        b_d = kk * a[:, d * R_WIDTH:(d + 1) * R_WIDTH]
        for c0 in range(0, sb, CHUNK):
            loc = slice(c0, c0 + CHUNK)
            dst = slice(r0 + c0, r0 + c0 + CHUNK)
            lwc = lw_d[loc]
            hi, lo = _split2(lwc)
            cum = _dot(tri[d], hi) + _dot(tri[d], lo)
            e_neg = jnp.exp(-cum)
            pp_ref[slot, P_KT + d, dst] = _bf(kk[loc] * jnp.exp(cum - lwc))
            pp_ref[slot, P_RT + d, dst] = _bf(r[loc] * jnp.exp(cum))
            pp_ref[slot, P_KH + d, dst] = _bf(kd[d][loc] * e_neg)
            pp_ref[slot, P_BH + d, dst] = _bf(b_d[loc] * e_neg)
            et = jnp.exp(jnp.sum(lwc, axis=0, keepdims=True))
            et_ref[slot, d, (r0 + c0) // CHUNK] = jnp.broadcast_to(et, (8, R_WIDTH))
            yield


def _scan_pass(rw, grid_mode, layer, mu, w0c, a0c, wup, aup, gup, kkw, ka, rk, eh):
    b, l, _ = rw.shape
    tm = min(256, l) if grid_mode else l
    nt = l // tm
    n_steps = b * nt + 1
    nrow = l // GRID_W
    per = tm // GRID_W
    prep_bi = lambda s: (jnp.minimum(s, n_steps - 2) // nt, jnp.minimum(s, n_steps - 2) % nt)
    alg_bi = lambda s: (jnp.maximum(s - 1, 0) // nt, jnp.maximum(s - 1, 0) % nt)

    def rw_map(s):
        bi, i = prep_bi(s)
        return bi, i, 0

    def prev_map(s):
        bi, i = prep_bi(s)
        return bi, jnp.maximum(i * per - 1, 0), 0

    def next_map(s):
        bi, i = prep_bi(s)
        return bi, jnp.minimum((i + 1) * per, nrow - 1), 0

    def out_map(s):
        bi, i = alg_bi(s)
        return 0, bi, i, 0

    in_specs = [pl.BlockSpec((1, tm, RWKV_IN), rw_map)]
    args = [rw]
    if grid_mode:
        in_specs += [pl.BlockSpec((1, GRID_W, RWKV_IN), prev_map), pl.BlockSpec((1, GRID_W, RWKV_IN), next_map)]
        args += [rw, rw]
    stacked = [mu, w0c, a0c, wup, aup, gup, kkw, ka, rk]
    in_specs += [_layer_spec(c, layer) for c in stacked] + [_const_spec(eh.shape)]
    args += stacked + [eh]
    one = pl.BlockSpec((1, tm, R_WIDTH), rw_map)
    two = pl.BlockSpec((2, 1, tm, R_WIDTH), out_map)
    s1 = jax.ShapeDtypeStruct((b, l, R_WIDTH), BF16)
    s2 = jax.ShapeDtypeStruct((2, b, l, R_WIDTH), F32)
    s2h = jax.ShapeDtypeStruct((2, b, l, R_WIDTH), BF16)
    return pl.pallas_call(
        functools.partial(_scan_kernel, grid_mode=grid_mode, seq_len=l, tm=tm, n_tiles=nt, n_steps=n_steps),
        grid=(n_steps,),
        in_specs=in_specs,
        out_specs=[two, two, two, two, one, one],
        out_shape=[s2h, s2h, s2, s2h, s1, s1],
        scratch_shapes=[pltpu.VMEM((N_SLOTS, N_PLANES, tm, R_WIDTH), BF16),
                        pltpu.VMEM((N_SLOTS, 2, tm // CHUNK, 8, R_WIDTH), F32)],
        compiler_params=_params("arbitrary"),
        name="rwkv_scan",
    )(*args)


def _block_diag_mask():
    return lax.broadcasted_iota(jnp.int32, (CHUNK, LANE_TILE), 1) < HEAD_DIM


def _block_diag(x, first_head):
    zero = jnp.zeros((CHUNK, LANE_TILE), x.dtype)
    per_tile = LANE_TILE // HEAD_DIM
    blocks = []
    for h in range(HEADS_PER_GROUP):
        tile = h // per_tile
        piece = x[:, tile * LANE_TILE:(tile + 1) * LANE_TILE]
        piece = jnp.where(first_head, piece, zero) if h % per_tile == 0 else jnp.where(first_head, zero, piece)
        row = [zero] * (GROUP_LANES // LANE_TILE)
        row[tile] = piece
        blocks.append(jnp.concatenate(row, axis=1))
    return jnp.concatenate(blocks, axis=0)


def _chunk_algebra(pp_ref, et_ref, slot, reff_ref, yloc_ref, m_ref, gs_ref, *, tc):
    t = lax.broadcasted_iota(jnp.int32, (CHUNK, GROUP_LANES), 0)
    lane = lax.broadcasted_iota(jnp.int32, (CHUNK, GROUP_LANES), 1)
    s = jnp.bitwise_and(lane, CHUNK - 1)
    eye = jnp.where(s == t, 1.0, 0.0)
    before = (s < t, s > t)
    upto = (s <= t, s >= t)
    bdmask = _block_diag_mask()
    bdiag = lambda x: _block_diag(_bf(x), bdmask)
    units = [(d, slice(ci * CHUNK, (ci + 1) * CHUNK), slice(gi * GROUP_LANES, (gi + 1) * GROUP_LANES))
             for d in range(2) for ci in range(tc // CHUNK) for gi in range(N_GROUPS)]
    nu = range(len(units))
    dirs = [d for d, _, _ in units]

    def staged(fn):
        out = []
        for u in nu:
            out.append(fn(u))
            yield
        return out

    kt = [pp_ref[slot, P_KT + d, rows, lanes] for d, rows, lanes in units]
    rt = [pp_ref[slot, P_RT + d, rows, lanes] for d, rows, lanes in units]
    vv = [pp_ref[slot, P_V, rows, lanes] for d, rows, lanes in units]
    dec = [eye * et_ref[slot, d, rows.start // CHUNK, 0:1, lanes] for d, rows, lanes in units]

    def nt_products(plane):
        def fn(u):
            d, rows, lanes = units[u]
            dh, dl = _split2(dec[u])
            lhs = jnp.concatenate([kt[u], rt[u], dh, dl], axis=0)
            return _dot_nt(lhs, _block_diag(pp_ref[slot, plane + d, rows, lanes], bdmask))
        return fn

    sk = yield from staged(nt_products(P_KH))
    sb = yield from staged(nt_products(P_BH))
    a_k = [jnp.where(before[dirs[u]], sk[u][:CHUNK], 0.0) for u in nu]
    a_rk = [jnp.where(upto[dirs[u]], sk[u][CHUNK:2 * CHUNK], 0.0) for u in nu]
    kb_t = [sk[u][2 * CHUNK:3 * CHUNK] + sk[u][3 * CHUNK:] for u in nu]
    n = [jnp.where(before[dirs[u]], -sb[u][:CHUNK], 0.0) for u in nu]
    a_rb = [jnp.where(upto[dirs[u]], sb[u][CHUNK:2 * CHUNK], 0.0) for u in nu]
    bb_t = [sb[u][2 * CHUNK:3 * CHUNK] + sb[u][3 * CHUNK:] for u in nu]

    tinv = [eye + n[u] for u in nu]
    p = yield from staged(lambda u: _dot(_bf(n[u]), bdiag(n[u])))
    av = yield from staged(
        lambda u: _dot(_bf(jnp.concatenate([a_k[u], a_rk[u], kb_t[u]], axis=0)), _block_diag(vv[u], bdmask)))
    for _ in range(4):
        tp = yield from staged(lambda u: _dot(_bf(jnp.concatenate([tinv[u], p[u]], axis=0)), bdiag(p[u])))
        tinv = [tinv[u] + tp[u][:CHUNK] for u in nu]
        p = [tp[u][CHUNK:] for u in nu]
    tb = yield from staged(lambda u: _bf(tinv[u] + _dot(_bf(tinv[u]), bdiag(p[u]))))

    uk = yield from staged(lambda u: _dot(tb[u], _block_diag(kt[u], bdmask)))
    uv = yield from staged(lambda u: _dot(tb[u], bdiag(av[u][:CHUNK])))
    rows_b = [_bf(jnp.concatenate([a_rb[u], bb_t[u]], axis=0)) for u in nu]
    bk = yield from staged(lambda u: _dot(rows_b[u], bdiag(uk[u])))
    bv = yield from staged(lambda u: _dot(rows_b[u], bdiag(uv[u])))
    for u, (d, rows, lanes) in enumerate(units):
        reff_ref[d, 0, rows, lanes] = _bf(rt[u].astype(F32) - bk[u][:CHUNK])
        yloc_ref[d, 0, rows, lanes] = _bf(av[u][CHUNK:2 * CHUNK] - bv[u][:CHUNK])
        m_ref[d, 0, rows, lanes] = dec[u] - bk[u][CHUNK:]
        gs_ref[d, 0, rows, lanes] = _bf(av[u][2 * CHUNK:] - bv[u][CHUNK:])
        yield


def _state_steps(m0_ref, g0_ref, re0_ref, yl0_ref, m1_ref, g1_ref, re1_ref, yl1_ref, s0_ref,
                 y0_ref, y1_ref, h_ref, *, batch):
    c = pl.program_id(0)

    @pl.when(c == 0)
    def _():
        h_ref[...] = s0_ref[...]

    bdmask = _block_diag_mask()
    per_dir = ((m0_ref, g0_ref, re0_ref, yl0_ref, y0_ref), (m1_ref, g1_ref, re1_ref, yl1_ref, y1_ref))
    units = [(d, bi, slice(gi * GROUP_LANES, (gi + 1) * GROUP_LANES))
             for d in range(2) for bi in range(batch) for gi in range(N_GROUPS)]
    hs = [_split2(h_ref[d, bi, :, lanes]) for d, bi, lanes in units]
    ms = [_split2(per_dir[d][0][0, bi, :, lanes]) for d, bi, lanes in units]
    rs = [per_dir[d][2][0, bi, :, lanes] for d, bi, lanes in units]
    o1, o2 = [], []
    for u in range(len(units)):
        o1.append(_dot(jnp.concatenate([ms[u][0], ms[u][1], rs[u]], axis=0), _block_diag(hs[u][0], bdmask)))
        yield
    for u in range(len(units)):
        o2.append(_dot(jnp.concatenate([ms[u][0], rs[u]], axis=0), _block_diag(hs[u][1], bdmask)))
        yield
    for u, (d, bi, lanes) in enumerate(units):
        _, g_ref, _, yl_ref, y_ref = per_dir[d]
        mh_new = o1[u][0:CHUNK] + o1[u][CHUNK:2 * CHUNK] + o2[u][0:CHUNK]
        rh_new = o1[u][2 * CHUNK:] + o2[u][CHUNK:]
        y_ref[bi, :, lanes] = _bf(yl_ref[0, bi, :, lanes].astype(F32) + rh_new)
        h_ref[d, bi, :, lanes] = mh_new + g_ref[0, bi, :, lanes].astype(F32)
        yield


def _state_kernel(*refs, batch):
    for _ in _state_steps(*refs, batch=batch):
        pass


def _state_pass(m, g, reff, yloc, s0):
    _, b, l, _ = m.shape
    nc = l // CHUNK
    blk = (1, b, CHUNK, R_WIDTH)
    fwd = pl.BlockSpec(blk, lambda c: (0, 0, c, 0))
    bwd = pl.BlockSpec(blk, lambda c: (1, 0, nc - 1 - c, 0))
    st = pl.BlockSpec((2, b, HEAD_DIM, R_WIDTH), lambda c: (0, 0, 0, 0))
    sy = jax.ShapeDtypeStruct((b, l, R_WIDTH), BF16)
    return pl.pallas_call(
        functools.partial(_state_kernel, batch=b),
        grid=(nc,),
        in_specs=[fwd, fwd, fwd, fwd, bwd, bwd, bwd, bwd, st],
        out_specs=[pl.BlockSpec((b, CHUNK, R_WIDTH), lambda c: (0, c, 0)),
                   pl.BlockSpec((b, CHUNK, R_WIDTH), lambda c: (0, nc - 1 - c, 0)),
                   st],
        out_shape=[sy, sy, jax.ShapeDtypeStruct((2, b, HEAD_DIM, R_WIDTH), F32)],
        compiler_params=_params("arbitrary"),
        name="rwkv_state",
    )(m, g, reff, yloc, m, g, reff, yloc, s0)


def _posdft_kernel(cs_ref, z_ref, o_ref):
    l = z_ref.shape[2]
    z = z_ref[0].reshape(2 * l, F_WIDTH)
    o_ref[0] = _bf(_dot(cs_ref[...], z))


def _posdft_fold_steps(w_ref, rev_ref, z_ref, o_ref, f_ref, i, *, tm):
    l = z_ref.shape[2]
    half = l // 2
    nb = l // REV_BLOCK

    @pl.when(i == 0)
    def _():
        for part in range(2):
            for j in range(half // REV_BLOCK):
                blk = z_ref[0, part, j * REV_BLOCK:(j + 1) * REV_BLOCK, :]
                src_a = z_ref[0, part, l - (j + 1) * REV_BLOCK:l - j * REV_BLOCK, :]
                jb = (nb - j) % nb
                src_b = z_ref[0, part, jb * REV_BLOCK:(jb + 1) * REV_BLOCK, :]
                rev = _dot(rev_ref[0], src_a) + _dot(rev_ref[1], src_b)
                sgn = 1.0 if part == 0 else -1.0
                f_ref[part * half + j * REV_BLOCK:part * half + (j + 1) * REV_BLOCK, :] = _bf(
                    blk.astype(F32) + sgn * rev)

    mid = z_ref[0, 0, half:half + 1, :].astype(F32) * (1.0 / float(np.sqrt(l)))
    piece = min(DFT_PIECE, tm)
    base = 0 if w_ref.shape[0] == tm else pl.multiple_of(i * tm, piece)
    for r0 in range(0, tm, piece):
        row = lax.broadcasted_iota(jnp.int32, (piece, F_WIDTH), 0) + r0
        alt = (1 - 2 * jnp.bitwise_and(row, 1)).astype(F32)
        o_ref[0, r0:r0 + piece] = _bf(_dot(w_ref[pl.ds(base + r0, piece), :], f_ref[...]) + alt * mid)
        yield


def _posdft_fold_kernel(w_ref, rev_ref, z_ref, o_ref, f_ref, *, tm):
    for _ in _posdft_fold_steps(w_ref, rev_ref, z_ref, o_ref, f_ref, pl.program_id(1), tm=tm):
        pass


def _state_dft_kernel(*refs, batch, tm, n_tiles):
    state_refs = refs[:9] + refs[12:15]
    w_ref, rev_ref, z_ref = refs[9:12]
    o_ref, f_ref = refs[15:]
    i = pl.program_id(0) % n_tiles
    dft = _posdft_fold_steps(w_ref, rev_ref, z_ref, o_ref, f_ref, i, tm=tm)
    state_yields = 3 * 2 * batch * N_GROUPS
    dft_yields = -(-tm // DFT_PIECE)
    _co_emit(_state_steps(*state_refs, batch=batch), dft, every=max(1, state_yields // dft_yields))


def _state_dft_pass(m, g, reff, yloc, s0, z, cs):
    _, b, l, _ = m.shape
    nc = l // CHUNK
    tm = (b * l) // nc
    nt = l // tm
    w, rev = cs
    blk = (1, b, CHUNK, R_WIDTH)
    fwd = pl.BlockSpec(blk, lambda c: (0, 0, c, 0))
    bwd = pl.BlockSpec(blk, lambda c: (1, 0, nc - 1 - c, 0))
    st = pl.BlockSpec((2, b, HEAD_DIM, R_WIDTH), lambda c: (0, 0, 0, 0))
    sy = jax.ShapeDtypeStruct((b, l, R_WIDTH), BF16)
    return pl.pallas_call(
        functools.partial(_state_dft_kernel, batch=b, tm=tm, n_tiles=nt),
        grid=(nc,),
        in_specs=[fwd, fwd, fwd, fwd, bwd, bwd, bwd, bwd, st,
                  pl.BlockSpec((l, l), lambda c: (0, 0), pipeline_mode=pl.Buffered(1)),
                  _const_spec(rev.shape),
                  pl.BlockSpec((1, 2, l, F_WIDTH), lambda c: (c // nt, 0, 0, 0))],
        out_specs=[pl.BlockSpec((b, CHUNK, R_WIDTH), lambda c: (0, c, 0)),
                   pl.BlockSpec((b, CHUNK, R_WIDTH), lambda c: (0, nc - 1 - c, 0)),
                   st,
                   pl.BlockSpec((1, tm, F_WIDTH), lambda c: (c // nt, c % nt, 0))],
        out_shape=[sy, sy, jax.ShapeDtypeStruct((2, b, HEAD_DIM, R_WIDTH), F32),
                   jax.ShapeDtypeStruct((b, l, F_WIDTH), BF16)],
        scratch_shapes=[pltpu.VMEM((l, F_WIDTH), BF16)],
        compiler_params=_params("arbitrary"),
        name="rwkv_state_dft",
    )(m, g, reff, yloc, m, g, reff, yloc, s0, w, rev, z)


def _can_fuse_state_dft(b, l, cs):
    tm = b * CHUNK
    return isinstance(cs, tuple) and l % tm == 0 and tm % 16 == 0


def _pos_dft(z, cs):
    b, _, l, _ = z.shape
    tm = min(512, l)
    if isinstance(cs, tuple):
        w, rev = cs
        return pl.pallas_call(
            functools.partial(_posdft_fold_kernel, tm=tm),
            grid=(b, l // tm),
            in_specs=[pl.BlockSpec((tm, l), lambda bi, i: (i, 0)),
                      _const_spec(rev.shape),
                      pl.BlockSpec((1, 2, l, F_WIDTH), lambda bi, i: (bi, 0, 0, 0))],
            out_specs=pl.BlockSpec((1, tm, F_WIDTH), lambda bi, i: (bi, i, 0)),
            out_shape=jax.ShapeDtypeStruct((b, l, F_WIDTH), BF16),
            scratch_shapes=[pltpu.VMEM((l, F_WIDTH), BF16)],
            compiler_params=_params("parallel", "arbitrary"),
            name="pos_dft_fold",
        )(w, rev, z)
    return pl.pallas_call(
        _posdft_kernel,
        grid=(b, l // tm),
        in_specs=[pl.BlockSpec((tm, 2 * l), lambda bi, i: (i, 0)),
                  pl.BlockSpec((1, 2, l, F_WIDTH), lambda bi, i: (bi, 0, 0, 0))],
        out_specs=pl.BlockSpec((1, tm, F_WIDTH), lambda bi, i: (bi, i, 0)),
        out_shape=jax.ShapeDtypeStruct((b, l, F_WIDTH), BF16),
        compiler_params=_params("parallel", "parallel"),
        name="pos_dft",
    )(cs, z)


def _merge_mlp_kernel(f_ref, y0_ref, y1_ref, g_ref, bonus_ref, gates_ref, x_ref, mod_ref,
                      lnw_ref, lnb_ref, n2_ref, nf_ref, eh_ref, wf_ref, wr_ref, wo_ref, w1_ref, w2_ref,
                      o_ref, *, final_norm):
    y = y0_ref[0].astype(F32) + y1_ref[0].astype(F32)
    eh = eh_ref[...]
    inv_n = 1.0 / HEAD_DIM
    mu = _head_sum(y, eh) * inv_n
    dlt = y - mu
    var = _head_sum(dlt * dlt, eh) * inv_n
    yn = dlt * lax.rsqrt(var + GN_EPS) * lnw_ref[...] + lnb_ref[...]
    rwkv = _bf((yn + bonus_ref[0].astype(F32)) * g_ref[0].astype(F32))
    fo = _dot(f_ref[0], wf_ref[...])
    ro = _dot(rwkv, wr_ref[...])
    gates = gates_ref[0].astype(F32)
    merged = jax.nn.sigmoid(gates[:, :D_MODEL]) * fo + jax.nn.sigmoid(gates[:, D_MODEL:]) * ro
    x = x_ref[0] + mod_ref[0, 2:3, :] * _dot(_bf(merged), wo_ref[...])
    h = _bf(_rms(x) * n2_ref[...] * (1.0 + mod_ref[0, 4:5, :]) + mod_ref[0, 3:4, :])
    acc = jnp.zeros(x.shape, F32)
    step = 1024
    for j in range(D_FF // step):
        u = jnp.maximum(_dot(h, w1_ref[:, j * step:(j + 1) * step]), 0.0)
        acc = acc + _dot(_bf(u * u), w2_ref[j * step:(j + 1) * step, :])
    x2 = x + mod_ref[0, 5:6, :] * acc
    if final_norm:
        x2 = _rms(x2) * nf_ref[...]
    o_ref[0] = x2


def _merge_mlp(f, y0, y1, g, bonus, gates, x, mod, mod_row, layer, lnw, lnb, n2, nf, eh, wf, wr, wo, w1, w2,
               final_norm):
    b, l, d = x.shape
    tm = min(512, l)
    row = lambda w: pl.BlockSpec((1, tm, w), lambda bi, i: (bi, i, 0))
    once = lambda a: pl.BlockSpec(a.shape, lambda bi, i: (0, 0), pipeline_mode=pl.Buffered(1))
    stacked = lambda a: _layer_spec(a, layer, single_buffer=True)
    return pl.pallas_call(
        functools.partial(_merge_mlp_kernel, final_norm=final_norm),
        grid=(b, l // tm),
        in_specs=[row(F_WIDTH), row(R_WIDTH), row(R_WIDTH), row(R_WIDTH), row(R_WIDTH), row(GATE_W), row(d),
                  _mod_spec(mod, layer, mod_row),
                  stacked(lnw), stacked(lnb), stacked(n2), once(nf), once(eh),
                  stacked(wf), stacked(wr), stacked(wo), stacked(w1), stacked(w2)],
        out_specs=row(d),
        out_shape=jax.ShapeDtypeStruct((b, l, d), F32),
        compiler_params=_params("parallel", "parallel"),
        name="merge_mlp",
    )(f, y0, y1, g, bonus, gates, x, mod, lnw, lnb, n2, nf, eh, wf, wr, wo, w1, w2)


def _channel_dft():
    n = FGROUP_DIM
    jk = np.outer(np.arange(n), np.arange(n)) % n
    ang = 2.0 * np.pi * jk / n
    c = np.cos(ang) / np.sqrt(n)
    s = np.sin(ang) / np.sqrt(n)
    g = F_WIDTH // n
    out = np.zeros((F_WIDTH, 2 * F_WIDTH), np.float32)
    for i in range(g):
        out[i * n:(i + 1) * n, i * n:(i + 1) * n] = c
        out[i * n:(i + 1) * n, F_WIDTH + i * n:F_WIDTH + (i + 1) * n] = s
    return jnp.asarray(out)


def _position_dft(l):
    fold = (l // 2) % REV_BLOCK == 0
    nk = l // 2 if fold else l
    jk = np.outer(np.arange(l), np.arange(nk)) % l
    ang = 2.0 * np.pi * jk / l
    c = np.cos(ang) / np.sqrt(l)
    if fold:
        c[:, 0] *= 0.5
    cs = jnp.asarray(np.concatenate([c, -np.sin(ang) / np.sqrt(l)], axis=1).astype(np.float32)).astype(BF16)
    if not fold:
        return cs
    rev = np.zeros((2, REV_BLOCK, REV_BLOCK), np.float32)
    idx = np.arange(1, REV_BLOCK)
    rev[0, idx, REV_BLOCK - idx] = 1.0
    rev[1, 0, 0] = 1.0
    return cs, jnp.asarray(rev).astype(BF16)


def _head_ones():
    h = np.arange(R_WIDTH) // HEAD_DIM
    return jnp.asarray((h[:, None] == h[None, :]).astype(np.float32)).astype(BF16)


def _two_dir_lora(w):
    z = jnp.zeros_like(w[:, 0])
    return _bf(jnp.concatenate([jnp.concatenate([w[:, 0], z], axis=2),
                                jnp.concatenate([z, w[:, 1]], axis=2)], axis=1))


def _layer(x, mod_row, layer, grid_mode, need_out, final_norm, s0, p):
    rw_out = _inproj(x, p["mod"], mod_row, layer, p["n1"], p["win"], p["wz"][layer], need_out)
    reff, yloc, m, gs, g, bonus = _scan_pass(rw_out[0], grid_mode, layer, p["mu"], p["w0"], p["a0"], p["wup"],
                                             p["aup"], p["gup"], p["kkw"], p["ka"], p["rk"], p["eh"])
    if not need_out:
        return None, _state_pass(m, gs, reff, yloc, s0)[2]
    z, gates = rw_out[1], rw_out[2]
    cs = p["pos_dft"][x.shape[1]]
    if _can_fuse_state_dft(x.shape[0], x.shape[1], cs):
        y0, y1, s_fin, f = _state_dft_pass(m, gs, reff, yloc, s0, z, cs)
    else:
        y0, y1, s_fin = _state_pass(m, gs, reff, yloc, s0)
        f = _pos_dft(z, cs)
    x2 = _merge_mlp(f, y0, y1, g, bonus, gates, x, p["mod"], mod_row, layer, p["lnw"], p["lnb"], p["n2"],
                    p["nf"], p["eh"], p["wf"], p["wr"], p["wo"], p["w1"], p["w2"], final_norm)
    return x2, s_fin


def kernel(x, c, ctx, c_ctx, w_mod, b_mod, norm1, norm2, w_in, mu_shift, w0, w_up, a0, a_up, g_up,
           k_k, k_a, r_k, ln_x_w, ln_x_b, w_fourier_up, w_rwkv_up, w_out, mlp_w1, mlp_w2, norm_f):
    depth = w_mod.shape[0]
    batch, seq, d = x.shape
    ctx_len = ctx.shape[1]
    assert d == D_MODEL and batch + 1 <= MOD_ROWS
    assert seq % GRID_W == 0 and ctx_len % CHUNK == 0 and seq % CHUNK == 0

    cc = jnp.zeros((MOD_ROWS, d), F32).at[:batch].set(c).at[batch].set(c_ctx)
    cdft = _channel_dft()
    rows = lambda a: a.reshape(depth, 1, -1)
    p = {
        "eh": _head_ones(),
        "pos_dft": {n: _position_dft(n) for n in {seq, ctx_len}},
        "nf": norm_f.reshape(1, -1),
        "mod": _modulation(cc, w_mod, b_mod).reshape(depth, MOD_ROWS, N_MOD, d),
        "wz": [_fold_channel_dft(w_in, layer, cdft) for layer in range(depth)],
        "win": _bf(w_in),
        "n1": rows(norm1), "n2": rows(norm2), "mu": rows(mu_shift), "w0": rows(w0), "a0": rows(a0),
        "wup": _two_dir_lora(w_up), "aup": _two_dir_lora(a_up), "gup": _bf(g_up),
        "kkw": rows(k_k), "ka": rows(k_a), "rk": rows(r_k), "lnw": rows(ln_x_w), "lnb": rows(ln_x_b),
        "wf": _bf(w_fourier_up), "wr": _bf(w_rwkv_up), "wo": _bf(w_out),
        "w1": _bf(mlp_w1), "w2": _bf(mlp_w2),
    }
    lat_row = lambda bi: bi
    ctx_row = lambda bi: batch

    x_lat, x_ctx = x, ctx
    s_zero = jnp.zeros((2, batch, HEAD_DIM, R_WIDTH), F32)
    for layer in range(depth):
        last = layer == depth - 1
        x_ctx, s_ctx = _layer(x_ctx, ctx_row, layer, False, not last, False, s_zero, p)
        x_lat, _ = _layer(x_lat, lat_row, layer, True, True, last, s_ctx, p)
    return x_lat
```

```python
import functools

import numpy as np
import jax
import jax.numpy as jnp
from jax import lax
from jax.experimental import pallas as pl
from jax.experimental.pallas import tpu as pltpu

F32 = jnp.float32
BF16 = jnp.bfloat16

D_MODEL = 1024
GRID_W = 64
F_WIDTH = 512
FGROUP_DIM = 128
HEAD_DIM = 64
N_RHEADS = 8
R_WIDTH = N_RHEADS * HEAD_DIM
D_LORA = 64
D_GATE_LORA = 128
RWKV_IN = 3 * R_WIDTH + 4 * D_LORA + D_GATE_LORA
GATE_W = 2 * D_MODEL
D_FF = 4 * D_MODEL
N_MOD = 6
NORM_EPS = 1e-6
GN_EPS = 64e-5
L2_EPS = 1e-12

CHUNK = 64
PREP_ROWS = 128
GROUP_LANES = 256
HEADS_PER_GROUP = GROUP_LANES // HEAD_DIM
LANE_TILE = 128
REV_BLOCK = 256
DFT_PIECE = 128
N_GROUPS = R_WIDTH // GROUP_LANES
MOD_ROWS = 16
VMEM_LIMIT = 56 * 1024 * 1024


def _bf(x):
    return x.astype(BF16)


def _dot(a, b):
    return jnp.dot(a, b, preferred_element_type=F32)


def _dot_nt(a, b):
    return lax.dot_general(a, b, (((1,), (1,)), ((), ())), preferred_element_type=F32)


def _split2(x):
    hi = _bf(x)
    lo = _bf(x - hi.astype(F32))
    return hi, lo


def _head_sum(x, ones):
    return _dot(_bf(x), ones)


def _params(*sem):
    return pltpu.CompilerParams(dimension_semantics=sem, vmem_limit_bytes=VMEM_LIMIT)


def _const_spec(shape):
    zeros = (0,) * len(shape)
    return pl.BlockSpec(shape, lambda *_: zeros)


def _layer_spec(a, l, single_buffer=False):
    tail = tuple(a.shape[1:])
    idx = (l,) + (0,) * len(tail)
    kw = {"pipeline_mode": pl.Buffered(1)} if single_buffer else {}
    return pl.BlockSpec((None,) + tail, lambda *_: idx, **kw)


def _mod_spec(mod, l, mod_row):
    return pl.BlockSpec((None, 1) + tuple(mod.shape[2:]), lambda bi, i: (l, mod_row(bi), 0, 0))


def _modulation_kernel(x_ref, w_ref, b_ref, o_ref):
    x = x_ref[...]
    x = x * jax.nn.sigmoid(x)
    o_ref[0] = _dot(_bf(x), _bf(w_ref[0])) + b_ref[0]


def _modulation(cc, w, b):
    depth, d, n = w.shape
    tn = 512
    return pl.pallas_call(
        _modulation_kernel,
        grid=(depth, n // tn),
        in_specs=[_const_spec(cc.shape),
                  pl.BlockSpec((1, d, tn), lambda l, j: (l, 0, j)),
                  pl.BlockSpec((1, 1, tn), lambda l, j: (l, 0, j))],
        out_specs=pl.BlockSpec((1, cc.shape[0], tn), lambda l, j: (l, 0, j)),
        out_shape=jax.ShapeDtypeStruct((depth, cc.shape[0], n), F32),
        compiler_params=_params("parallel", "parallel"),
        name="modulation",
    )(cc, w, b.reshape(depth, 1, n))


def _fold_kernel(w_ref, c_ref, o_ref):
    wh, wl = _split2(w_ref[...])
    ch, cl = _split2(c_ref[...])
    o_ref[...] = _bf(_dot(wh, ch) + _dot(wl, ch) + _dot(wh, cl))


def _fold_channel_dft(w_in, l, cdft):
    d = w_in.shape[1]
    return pl.pallas_call(
        _fold_kernel,
        grid=(1,),
        in_specs=[pl.BlockSpec((None, d, F_WIDTH), lambda i: (l, 0, 0)), _const_spec(cdft.shape)],
        out_specs=_const_spec((d, cdft.shape[1])),
        out_shape=jax.ShapeDtypeStruct((d, cdft.shape[1]), BF16),
        compiler_params=_params("arbitrary"),
        name="fold_channel_dft",
    )(w_in, cdft)


def _rms(x):
    return x * lax.rsqrt(jnp.mean(x * x, axis=-1, keepdims=True) + NORM_EPS)


def _inproj_kernel(x_ref, mod_ref, n1_ref, win_ref, *rest, need_out):
    if need_out:
        wz_ref, rw_ref, z_ref, g_ref = rest
    else:
        (rw_ref,) = rest
    shift = mod_ref[0, 0:1, :]
    scale = mod_ref[0, 1:2, :]
    tm = x_ref.shape[1]
    half = tm // 2 if tm % 32 == 0 else tm
    for r0 in range(0, tm, half):
        rows = slice(r0, r0 + half)
        h = _bf(_rms(x_ref[0, rows]) * n1_ref[...] * (1.0 + scale) + shift)
        rw_ref[0, rows] = _dot(h, win_ref[:, F_WIDTH:F_WIDTH + RWKV_IN])
        if need_out:
            z = _dot(h, wz_ref[...])
            z_ref[0, 0, rows] = _bf(z[:, :F_WIDTH])
            z_ref[0, 1, rows] = _bf(z[:, F_WIDTH:])
            g_ref[0, rows] = _bf(_dot(h, win_ref[:, F_WIDTH + RWKV_IN:]))


def _inproj(x, mod, mod_row, layer, n1, win, wz, need_out):
    b, l, d = x.shape
    tm = min(512, l)
    row_spec = lambda w: pl.BlockSpec((1, tm, w), lambda bi, i: (bi, i, 0))
    in_specs = [row_spec(d), _mod_spec(mod, layer, mod_row), _layer_spec(n1, layer),
                _layer_spec(win, layer, single_buffer=True)]
    args = [x, mod, n1, win]
    out_specs = [row_spec(RWKV_IN)]
    out_shape = [jax.ShapeDtypeStruct((b, l, RWKV_IN), F32)]
    if need_out:
        in_specs += [_const_spec(wz.shape)]
        args += [wz]
        out_specs += [pl.BlockSpec((1, 2, tm, F_WIDTH), lambda bi, i: (bi, 0, i, 0)), row_spec(GATE_W)]
        out_shape += [jax.ShapeDtypeStruct((b, 2, l, F_WIDTH), BF16),
                      jax.ShapeDtypeStruct((b, l, GATE_W), BF16)]
    return pl.pallas_call(
        functools.partial(_inproj_kernel, need_out=need_out),
        grid=(b, l // tm),
        in_specs=in_specs, out_specs=out_specs, out_shape=out_shape,
        compiler_params=_params("parallel", "parallel"),
        name="inproj",
    )(*args)


def _log_sigmoid(x):
    return jnp.minimum(x, 0.0) - jnp.log1p(jnp.exp(-jnp.abs(x)))


P_KT, P_RT, P_KH, P_BH, P_V = 0, 2, 4, 6, 8
N_PLANES = 9
N_SLOTS = 2


def _co_emit(main, filler, every, start=0):
    for k, _ in enumerate(main):
        if k >= start and (k - start) % every == 0:
            next(filler, None)
    for _ in filler:
        pass


def _scan_kernel(*refs, grid_mode, seq_len, tm, n_tiles, n_steps):
    n_in = len(refs) - 8
    in_refs = refs[:n_in]
    reff_ref, yloc_ref, m_ref, gs_ref, g_ref, bonus_ref, pp_ref, et_ref = refs[n_in:]
    s = pl.program_id(0)
    slot_prep = jnp.bitwise_and(s, 1)
    slot_alg = 1 - slot_prep
    tile = jnp.minimum(s, n_steps - 2) % n_tiles

    @pl.when(s == 0)
    def _():
        pp_ref[1] = jnp.zeros(pp_ref.shape[1:], pp_ref.dtype)
        et_ref[1] = jnp.zeros(et_ref.shape[1:], et_ref.dtype)

    prep = _prep_tile(in_refs, pp_ref, et_ref, slot_prep, g_ref, bonus_ref, tile,
                      grid_mode=grid_mode, seq_len=seq_len, tm=tm)
    alg = _chunk_algebra(pp_ref, et_ref, slot_alg, reff_ref, yloc_ref, m_ref, gs_ref, tc=tm)
    _co_emit(alg, prep, every=3)


def _prep_tile(in_refs, pp_ref, et_ref, slot, g_ref, bonus_ref, tile, *, grid_mode, seq_len, tm):
    sb = min(PREP_ROWS, tm) if grid_mode else tm
    for r0 in range(0, tm, sb):
        yield from _prep_rows(r0, sb, in_refs, pp_ref, et_ref, slot, g_ref, bonus_ref, tile,
                              grid_mode=grid_mode, seq_len=seq_len, tm=tm)


def _prep_rows(r0, sb, in_refs, pp_ref, et_ref, slot, g_ref, bonus_ref, tile, *, grid_mode, seq_len, tm):
    if grid_mode:
        (rw_ref, prev_ref, next_ref, mu_ref, w0_ref, a0_ref, wup_ref, aup_ref, gup_ref,
         kkw_ref, ka_ref, rk_ref, eh_ref) = in_refs
    else:
        (rw_ref, mu_ref, w0_ref, a0_ref, wup_ref, aup_ref, gup_ref,
         kkw_ref, ka_ref, rk_ref, eh_ref) = in_refs
    rows = slice(r0, r0 + sb)
    t_loc = lax.broadcasted_iota(jnp.int32, (sb, 128), 0) + r0
    t_glob = t_loc + tile * tm
    lane = lax.broadcasted_iota(jnp.int32, (sb, 128), 1)
    if grid_mode:
        col = jnp.bitwise_and(t_loc, GRID_W - 1)
        masks = [col != 0, col != GRID_W - 1, t_glob >= GRID_W, t_glob < seq_len - GRID_W]
        n_parts = 4
    else:
        masks = [t_glob != 0, t_glob != seq_len - 1]
        n_parts = 2
    part_w = RWKV_IN // n_parts

    def shifted(j, part):
        cols = slice(128 * j, 128 * (j + 1))
        if part == 0:
            s = pltpu.roll(rw_ref[0, rows, cols], 1, 0)
        elif part == 1:
            s = pltpu.roll(rw_ref[0, rows, cols], sb - 1, 0)
        elif part == 2:
            if r0 > 0:
                s = rw_ref[0, r0 - GRID_W:r0 + sb - GRID_W, cols]
            elif sb == GRID_W:
                s = prev_ref[0, :, cols]
            else:
                s = jnp.concatenate([prev_ref[0, :, cols], rw_ref[0, 0:sb - GRID_W, cols]], axis=0)
        else:
            if r0 + sb < tm:
                s = rw_ref[0, r0 + GRID_W:r0 + sb + GRID_W, cols]
            elif sb == GRID_W:
                s = next_ref[0, :, cols]
            else:
                s = jnp.concatenate([rw_ref[0, r0 + GRID_W:tm, cols], next_ref[0, :, cols]], axis=0)
        return jnp.where(masks[part], s, 0.0)

    blocks = []
    for j in range(RWKV_IN // 128):
        p_lo = (128 * j) // part_w
        p_hi = (128 * j + 127) // part_w
        s = shifted(j, p_lo)
        if p_hi != p_lo:
            s = jnp.where(lane + 128 * j < part_w * p_hi, s, shifted(j, p_hi))
        xj = rw_ref[0, rows, 128 * j:128 * (j + 1)]
        blocks.append(xj + mu_ref[:, 128 * j:128 * (j + 1)] * (s - xj))
        yield

    nb = R_WIDTH // 128
    r = jnp.concatenate(blocks[0:nb], axis=1)
    k = jnp.concatenate(blocks[nb:2 * nb], axis=1)
    v = jnp.concatenate(blocks[2 * nb:3 * nb], axis=1)
    wd, ad, gd = blocks[3 * nb], blocks[3 * nb + 1], blocks[3 * nb + 2]

    tw = _bf(jnp.tanh(wd))
    adb = _bf(ad)
    lw_cols, a_cols = [], []
    for c0 in range(0, 2 * R_WIDTH, GROUP_LANES):
        cols = slice(c0, c0 + GROUP_LANES)
        w_logit = w0_ref[:, cols] + _dot(tw, wup_ref[:, cols])
        lw_cols.append(-jnp.exp(_log_sigmoid(w_logit) - 0.5))
        yield
        a_cols.append(jax.nn.sigmoid(a0_ref[:, cols] + _dot(adb, aup_ref[:, cols])))
        yield
    lw = jnp.concatenate(lw_cols, axis=1)
    a = jnp.concatenate(a_cols, axis=1)
    g_ref[0, rows] = _bf(_dot(_bf(jax.nn.sigmoid(gd)), gup_ref[...]))
    yield

    eh = eh_ref[...]
    kx = k * kkw_ref[...]
    kk = kx / jnp.maximum(jnp.sqrt(_head_sum(kx * kx, eh)), L2_EPS)
    yield
    ka = ka_ref[...]
    kd = (k * (1.0 + (a[:, :R_WIDTH] - 1.0) * ka), k * (1.0 + (a[:, R_WIDTH:] - 1.0) * ka))
    bonus_ref[0, rows] = _bf(_head_sum(r * (kd[0] + kd[1]) * rk_ref[...], eh) * v)
    pp_ref[slot, P_V, rows] = _bf(v)
    yield

    t64 = lax.broadcasted_iota(jnp.int32, (CHUNK, CHUNK), 0)
    s64 = lax.broadcasted_iota(jnp.int32, (CHUNK, CHUNK), 1)
    tri = (jnp.where(s64 <= t64, 1.0, 0.0).astype(BF16), jnp.where(s64 >= t64, 1.0, 0.0).astype(BF16))
    for d in range(2):
        lw_d = lw[:, d * R_WIDTH:(d + 1) * R_WIDTH]
        b_d = kk * a[:, d * R_WIDTH:(d + 1) * R_WIDTH]
        for c0 in range(0, sb, CHUNK):
            loc = slice(c0, c0 + CHUNK)
            dst = slice(r0 + c0, r0 + c0 + CHUNK)
            lwc = lw_d[loc]
            hi, lo = _split2(lwc)
            cum = _dot(tri[d], hi) + _dot(tri[d], lo)
            e_neg = jnp.exp(-cum)
            pp_ref[slot, P_KT + d, dst] = _bf(kk[loc] * jnp.exp(cum - lwc))
            pp_ref[slot, P_RT + d, dst] = _bf(r[loc] * jnp.exp(cum))
            pp_ref[slot, P_KH + d, dst] = _bf(kd[d][loc] * e_neg)
            pp_ref[slot, P_BH + d, dst] = _bf(b_d[loc] * e_neg)
            et = jnp.exp(jnp.sum(lwc, axis=0, keepdims=True))
            et_ref[slot, d, (r0 + c0) // CHUNK] = jnp.broadcast_to(et, (8, R_WIDTH))
            yield


def _scan_pass(rw, grid_mode, layer, mu, w0c, a0c, wup, aup, gup, kkw, ka, rk, eh):
    b, l, _ = rw.shape
    tm = min(256, l) if grid_mode else l
    nt = l // tm
    n_steps = b * nt + 1
    nrow = l // GRID_W
    per = tm // GRID_W
    prep_bi = lambda s: (jnp.minimum(s, n_steps - 2) // nt, jnp.minimum(s, n_steps - 2) % nt)
    alg_bi = lambda s: (jnp.maximum(s - 1, 0) // nt, jnp.maximum(s - 1, 0) % nt)

    def rw_map(s):
        bi, i = prep_bi(s)
        return bi, i, 0

    def prev_map(s):
        bi, i = prep_bi(s)
        return bi, jnp.maximum(i * per - 1, 0), 0

    def next_map(s):
        bi, i = prep_bi(s)
        return bi, jnp.minimum((i + 1) * per, nrow - 1), 0

    def out_map(s):
        bi, i = alg_bi(s)
        return 0, bi, i, 0

    in_specs = [pl.BlockSpec((1, tm, RWKV_IN), rw_map)]
    args = [rw]
    if grid_mode:
        in_specs += [pl.BlockSpec((1, GRID_W, RWKV_IN), prev_map), pl.BlockSpec((1, GRID_W, RWKV_IN), next_map)]
        args += [rw, rw]
    stacked = [mu, w0c, a0c, wup, aup, gup, kkw, ka, rk]
    in_specs += [_layer_spec(c, layer) for c in stacked] + [_const_spec(eh.shape)]
    args += stacked + [eh]
    one = pl.BlockSpec((1, tm, R_WIDTH), rw_map)
    two = pl.BlockSpec((2, 1, tm, R_WIDTH), out_map)
    s1 = jax.ShapeDtypeStruct((b, l, R_WIDTH), BF16)
    s2 = jax.ShapeDtypeStruct((2, b, l, R_WIDTH), F32)
    s2h = jax.ShapeDtypeStruct((2, b, l, R_WIDTH), BF16)
    return pl.pallas_call(
        functools.partial(_scan_kernel, grid_mode=grid_mode, seq_len=l, tm=tm, n_tiles=nt, n_steps=n_steps),
        grid=(n_steps,),
        in_specs=in_specs,
        out_specs=[two, two, two, two, one, one],
        out_shape=[s2h, s2h, s2, s2h, s1, s1],
        scratch_shapes=[pltpu.VMEM((N_SLOTS, N_PLANES, tm, R_WIDTH), BF16),
                        pltpu.VMEM((N_SLOTS, 2, tm // CHUNK, 8, R_WIDTH), F32)],
        compiler_params=_params("arbitrary"),
        name="rwkv_scan",
    )(*args)


def _block_diag_mask():
    return lax.broadcasted_iota(jnp.int32, (CHUNK, LANE_TILE), 1) < HEAD_DIM


def _block_diag(x, first_head):
    zero = jnp.zeros((CHUNK, LANE_TILE), x.dtype)
    per_tile = LANE_TILE // HEAD_DIM
    blocks = []
    for h in range(HEADS_PER_GROUP):
        tile = h // per_tile
        piece = x[:, tile * LANE_TILE:(tile + 1) * LANE_TILE]
        piece = jnp.where(first_head, piece, zero) if h % per_tile == 0 else jnp.where(first_head, zero, piece)
        row = [zero] * (GROUP_LANES // LANE_TILE)
        row[tile] = piece
        blocks.append(jnp.concatenate(row, axis=1))
    return jnp.concatenate(blocks, axis=0)


def _chunk_algebra(pp_ref, et_ref, slot, reff_ref, yloc_ref, m_ref, gs_ref, *, tc):
    t = lax.broadcasted_iota(jnp.int32, (CHUNK, GROUP_LANES), 0)
    lane = lax.broadcasted_iota(jnp.int32, (CHUNK, GROUP_LANES), 1)
    s = jnp.bitwise_and(lane, CHUNK - 1)
    eye = jnp.where(s == t, 1.0, 0.0)
    before = (s < t, s > t)
    upto = (s <= t, s >= t)
    bdmask = _block_diag_mask()
    bdiag = lambda x: _block_diag(_bf(x), bdmask)
    units = [(d, slice(ci * CHUNK, (ci + 1) * CHUNK), slice(gi * GROUP_LANES, (gi + 1) * GROUP_LANES))
             for d in range(2) for ci in range(tc // CHUNK) for gi in range(N_GROUPS)]
    nu = range(len(units))

    def staged(fn):
        out = []
        for u in nu:
            out.append(fn(u))
            yield
        return out

    kt = [pp_ref[slot, P_KT + d, rows, lanes] for d, rows, lanes in units]
    rt = [pp_ref[slot, P_RT + d, rows, lanes] for d, rows, lanes in units]
    vv = [pp_ref[slot, P_V, rows, lanes] for d, rows, lanes in units]
    dec = [eye * et_ref[slot, d, rows.start // CHUNK, 0:1, lanes] for d, rows, lanes in units]

    def nt_product(u, plane):
        d, rows, lanes = units[u]
        dh, dl = _split2(dec[u])
        lhs = jnp.concatenate([kt[u], rt[u], dh, dl], axis=0)
        s_all = _dot_nt(lhs, _block_diag(pp_ref[slot, plane + d, rows, lanes], bdmask))
        causal = jnp.where(before[d], s_all[:CHUNK], 0.0)
        incl = jnp.where(upto[d], s_all[CHUNK:2 * CHUNK], 0.0)
        probe = s_all[2 * CHUNK:3 * CHUNK] + s_all[3 * CHUNK:]
        return causal, incl, probe

    def key_stage(u):
        a_k, a_rk, kb_t = nt_product(u, P_KH)
        return _bf(jnp.concatenate([a_k, a_rk, kb_t], axis=0))

    def removal_stage(u):
        a_b, a_rb, bb_t = nt_product(u, P_BH)
        return -a_b, _bf(jnp.concatenate([a_rb, bb_t], axis=0))

    rows_v = yield from staged(key_stage)
    nb = yield from staged(removal_stage)
    n = [x[0] for x in nb]
    rows_b = [x[1] for x in nb]

    tinv = [eye + n[u] for u in nu]
    p = yield from staged(lambda u: _dot(_bf(n[u]), bdiag(n[u])))
    av = yield from staged(lambda u: _dot(rows_v[u], _block_diag(vv[u], bdmask)))

    def doubling(u):
        tp = _dot(_bf(jnp.concatenate([tinv[u], p[u]], axis=0)), bdiag(p[u]))
        return tinv[u] + tp[:CHUNK], tp[CHUNK:]

    for _ in range(4):
        both = yield from staged(doubling)
        tinv = [x[0] for x in both]
        p = [x[1] for x in both]
    tb = yield from staged(lambda u: _bf(tinv[u] + _dot(_bf(tinv[u]), bdiag(p[u]))))

    uk = yield from staged(lambda u: _bf(_dot(tb[u], _block_diag(kt[u], bdmask))))
    uv = yield from staged(lambda u: _bf(_dot(tb[u], bdiag(av[u][:CHUNK]))))
    for u, (d, rows, lanes) in enumerate(units):
        bk = _dot(rows_b[u], _block_diag(uk[u], bdmask))
        reff_ref[d, 0, rows, lanes] = _bf(rt[u].astype(F32) - bk[:CHUNK])
        m_ref[d, 0, rows, lanes] = dec[u] - bk[CHUNK:]
        yield
    for u, (d, rows, lanes) in enumerate(units):
        bv = _dot(rows_b[u], _block_diag(uv[u], bdmask))
        yloc_ref[d, 0, rows, lanes] = _bf(av[u][CHUNK:2 * CHUNK] - bv[:CHUNK])
        gs_ref[d, 0, rows, lanes] = _bf(av[u][2 * CHUNK:] - bv[CHUNK:])
        yield


def _state_steps(m0_ref, g0_ref, re0_ref, yl0_ref, m1_ref, g1_ref, re1_ref, yl1_ref, s0_ref,
                 y0_ref, y1_ref, h_ref, *, batch):
    c = pl.program_id(0)

    @pl.when(c == 0)
    def _():
        h_ref[...] = s0_ref[...]

    bdmask = _block_diag_mask()
    per_dir = ((m0_ref, g0_ref, re0_ref, yl0_ref, y0_ref), (m1_ref, g1_ref, re1_ref, yl1_ref, y1_ref))
    units = [(d, bi, slice(gi * GROUP_LANES, (gi + 1) * GROUP_LANES))
             for d in range(2) for bi in range(batch) for gi in range(N_GROUPS)]
    hs = [_split2(h_ref[d, bi, :, lanes]) for d, bi, lanes in units]
    ms = [_split2(per_dir[d][0][0, bi, :, lanes]) for d, bi, lanes in units]
    rs = [per_dir[d][2][0, bi, :, lanes] for d, bi, lanes in units]
    o1, o2 = [], []
    for u in range(len(units)):
        o1.append(_dot(jnp.concatenate([ms[u][0], ms[u][1], rs[u]], axis=0), _block_diag(hs[u][0], bdmask)))
        yield
    for u in range(len(units)):
        o2.append(_dot(jnp.concatenate([ms[u][0], rs[u]], axis=0), _block_diag(hs[u][1], bdmask)))
        yield
    for u, (d, bi, lanes) in enumerate(units):
        _, g_ref, _, yl_ref, y_ref = per_dir[d]
        mh_new = o1[u][0:CHUNK] + o1[u][CHUNK:2 * CHUNK] + o2[u][0:CHUNK]
        rh_new = o1[u][2 * CHUNK:] + o2[u][CHUNK:]
        y_ref[bi, :, lanes] = _bf(yl_ref[0, bi, :, lanes].astype(F32) + rh_new)
        h_ref[d, bi, :, lanes] = mh_new + g_ref[0, bi, :, lanes].astype(F32)
        yield


def _state_kernel(*refs, batch):
    for _ in _state_steps(*refs, batch=batch):
        pass


def _state_pass(m, g, reff, yloc, s0):
    _, b, l, _ = m.shape
    nc = l // CHUNK
    blk = (1, b, CHUNK, R_WIDTH)
    fwd = pl.BlockSpec(blk, lambda c: (0, 0, c, 0))
    bwd = pl.BlockSpec(blk, lambda c: (1, 0, nc - 1 - c, 0))
    st = pl.BlockSpec((2, b, HEAD_DIM, R_WIDTH), lambda c: (0, 0, 0, 0))
    sy = jax.ShapeDtypeStruct((b, l, R_WIDTH), BF16)
    return pl.pallas_call(
        functools.partial(_state_kernel, batch=b),
        grid=(nc,),
        in_specs=[fwd, fwd, fwd, fwd, bwd, bwd, bwd, bwd, st],
        out_specs=[pl.BlockSpec((b, CHUNK, R_WIDTH), lambda c: (0, c, 0)),
                   pl.BlockSpec((b, CHUNK, R_WIDTH), lambda c: (0, nc - 1 - c, 0)),
                   st],
        out_shape=[sy, sy, jax.ShapeDtypeStruct((2, b, HEAD_DIM, R_WIDTH), F32)],
        compiler_params=_params("arbitrary"),
        name="rwkv_state",
    )(m, g, reff, yloc, m, g, reff, yloc, s0)


def _posdft_kernel(cs_ref, z_ref, o_ref):
    l = z_ref.shape[2]
    z = z_ref[0].reshape(2 * l, F_WIDTH)
    o_ref[0] = _bf(_dot(cs_ref[...], z))


def _posdft_fold_steps(w_ref, rev_ref, z_ref, o_ref, f_ref, i, *, tm):
    l = z_ref.shape[2]
    half = l // 2
    nb = l // REV_BLOCK

    @pl.when(i == 0)
    def _():
        for part in range(2):
            for j in range(half // REV_BLOCK):
                blk = z_ref[0, part, j * REV_BLOCK:(j + 1) * REV_BLOCK, :]
                src_a = z_ref[0, part, l - (j + 1) * REV_BLOCK:l - j * REV_BLOCK, :]
                jb = (nb - j) % nb
                src_b = z_ref[0, part, jb * REV_BLOCK:(jb + 1) * REV_BLOCK, :]
                rev = _dot(rev_ref[0], src_a) + _dot(rev_ref[1], src_b)
                sgn = 1.0 if part == 0 else -1.0
                f_ref[part * half + j * REV_BLOCK:part * half + (j + 1) * REV_BLOCK, :] = _bf(
                    blk.astype(F32) + sgn * rev)

    mid = z_ref[0, 0, half:half + 1, :].astype(F32) * (1.0 / float(np.sqrt(l)))
    piece = min(DFT_PIECE, tm)
    base = 0 if w_ref.shape[0] == tm else pl.multiple_of(i * tm, piece)
    for r0 in range(0, tm, piece):
        row = lax.broadcasted_iota(jnp.int32, (piece, F_WIDTH), 0) + r0
        alt = (1 - 2 * jnp.bitwise_and(row, 1)).astype(F32)
        o_ref[0, r0:r0 + piece] = _bf(_dot(w_ref[pl.ds(base + r0, piece), :], f_ref[...]) + alt * mid)
        yield


def _posdft_fold_kernel(w_ref, rev_ref, z_ref, o_ref, f_ref, *, tm):
    for _ in _posdft_fold_steps(w_ref, rev_ref, z_ref, o_ref, f_ref, pl.program_id(1), tm=tm):
        pass


def _state_dft_kernel(*refs, batch, tm, n_tiles):
    state_refs = refs[:9] + refs[12:15]
    w_ref, rev_ref, z_ref = refs[9:12]
    o_ref, f_ref = refs[15:]
    i = pl.program_id(0) % n_tiles
    dft = _posdft_fold_steps(w_ref, rev_ref, z_ref, o_ref, f_ref, i, tm=tm)
    state_yields = 3 * 2 * batch * N_GROUPS
    dft_yields = -(-tm // DFT_PIECE)
    _co_emit(_state_steps(*state_refs, batch=batch), dft, every=max(1, state_yields // dft_yields))


def _state_dft_pass(m, g, reff, yloc, s0, z, cs):
    _, b, l, _ = m.shape
    nc = l // CHUNK
    tm = (b * l) // nc
    nt = l // tm
    w, rev = cs
    blk = (1, b, CHUNK, R_WIDTH)
    fwd = pl.BlockSpec(blk, lambda c: (0, 0, c, 0))
    bwd = pl.BlockSpec(blk, lambda c: (1, 0, nc - 1 - c, 0))
    st = pl.BlockSpec((2, b, HEAD_DIM, R_WIDTH), lambda c: (0, 0, 0, 0))
    sy = jax.ShapeDtypeStruct((b, l, R_WIDTH), BF16)
    return pl.pallas_call(
        functools.partial(_state_dft_kernel, batch=b, tm=tm, n_tiles=nt),
        grid=(nc,),
        in_specs=[fwd, fwd, fwd, fwd, bwd, bwd, bwd, bwd, st,
                  pl.BlockSpec((l, l), lambda c: (0, 0), pipeline_mode=pl.Buffered(1)),
                  _const_spec(rev.shape),
                  pl.BlockSpec((1, 2, l, F_WIDTH), lambda c: (c // nt, 0, 0, 0))],
        out_specs=[pl.BlockSpec((b, CHUNK, R_WIDTH), lambda c: (0, c, 0)),
                   pl.BlockSpec((b, CHUNK, R_WIDTH), lambda c: (0, nc - 1 - c, 0)),
                   st,
                   pl.BlockSpec((1, tm, F_WIDTH), lambda c: (c // nt, c % nt, 0))],
        out_shape=[sy, sy, jax.ShapeDtypeStruct((2, b, HEAD_DIM, R_WIDTH), F32),
                   jax.ShapeDtypeStruct((b, l, F_WIDTH), BF16)],
        scratch_shapes=[pltpu.VMEM((l, F_WIDTH), BF16)],
        compiler_params=_params("arbitrary"),
        name="rwkv_state_dft",
    )(m, g, reff, yloc, m, g, reff, yloc, s0, w, rev, z)


def _can_fuse_state_dft(b, l, cs):
    tm = b * CHUNK
    return isinstance(cs, tuple) and l % tm == 0 and tm % 16 == 0


def _pos_dft(z, cs):
    b, _, l, _ = z.shape
    tm = min(512, l)
    if isinstance(cs, tuple):
        w, rev = cs
        return pl.pallas_call(
            functools.partial(_posdft_fold_kernel, tm=tm),
            grid=(b, l // tm),
            in_specs=[pl.BlockSpec((tm, l), lambda bi, i: (i, 0)),
                      _const_spec(rev.shape),
                      pl.BlockSpec((1, 2, l, F_WIDTH), lambda bi, i: (bi, 0, 0, 0))],
            out_specs=pl.BlockSpec((1, tm, F_WIDTH), lambda bi, i: (bi, i, 0)),
            out_shape=jax.ShapeDtypeStruct((b, l, F_WIDTH), BF16),
            scratch_shapes=[pltpu.VMEM((l, F_WIDTH), BF16)],
            compiler_params=_params("parallel", "arbitrary"),
            name="pos_dft_fold",
        )(w, rev, z)
    return pl.pallas_call(
        _posdft_kernel,
        grid=(b, l // tm),
        in_specs=[pl.BlockSpec((tm, 2 * l), lambda bi, i: (i, 0)),
                  pl.BlockSpec((1, 2, l, F_WIDTH), lambda bi, i: (bi, 0, 0, 0))],
        out_specs=pl.BlockSpec((1, tm, F_WIDTH), lambda bi, i: (bi, i, 0)),
        out_shape=jax.ShapeDtypeStruct((b, l, F_WIDTH), BF16),
        compiler_params=_params("parallel", "parallel"),
        name="pos_dft",
    )(cs, z)


def _merge_mlp_kernel(f_ref, y0_ref, y1_ref, g_ref, bonus_ref, gates_ref, x_ref, mod_ref,
                      lnw_ref, lnb_ref, n2_ref, nf_ref, eh_ref, wf_ref, wr_ref, wo_ref, w1_ref, w2_ref,
                      o_ref, *, final_norm):
    y = y0_ref[0].astype(F32) + y1_ref[0].astype(F32)
    eh = eh_ref[...]
    inv_n = 1.0 / HEAD_DIM
    mu = _head_sum(y, eh) * inv_n
    dlt = y - mu
    var = _head_sum(dlt * dlt, eh) * inv_n
    yn = dlt * lax.rsqrt(var + GN_EPS) * lnw_ref[...] + lnb_ref[...]
    rwkv = _bf((yn + bonus_ref[0].astype(F32)) * g_ref[0].astype(F32))
    fo = _dot(f_ref[0], wf_ref[...])
    ro = _dot(rwkv, wr_ref[...])
    gates = gates_ref[0].astype(F32)
    merged = jax.nn.sigmoid(gates[:, :D_MODEL]) * fo + jax.nn.sigmoid(gates[:, D_MODEL:]) * ro
    x = x_ref[0] + mod_ref[0, 2:3, :] * _dot(_bf(merged), wo_ref[...])
    h = _bf(_rms(x) * n2_ref[...] * (1.0 + mod_ref[0, 4:5, :]) + mod_ref[0, 3:4, :])
    acc = jnp.zeros(x.shape, F32)
    step = 1024
    for j in range(D_FF // step):
        u = jnp.maximum(_dot(h, w1_ref[:, j * step:(j + 1) * step]), 0.0)
        acc = acc + _dot(_bf(u * u), w2_ref[j * step:(j + 1) * step, :])
    x2 = x + mod_ref[0, 5:6, :] * acc
    if final_norm:
        x2 = _rms(x2) * nf_ref[...]
    o_ref[0] = x2


def _merge_mlp(f, y0, y1, g, bonus, gates, x, mod, mod_row, layer, lnw, lnb, n2, nf, eh, wf, wr, wo, w1, w2,
               final_norm):
    b, l, d = x.shape
    tm = min(512, l)
    row = lambda w: pl.BlockSpec((1, tm, w), lambda bi, i: (bi, i, 0))
    once = lambda a: pl.BlockSpec(a.shape, lambda bi, i: (0, 0), pipeline_mode=pl.Buffered(1))
    stacked = lambda a: _layer_spec(a, layer, single_buffer=True)
    return pl.pallas_call(
        functools.partial(_merge_mlp_kernel, final_norm=final_norm),
        grid=(b, l // tm),
        in_specs=[row(F_WIDTH), row(R_WIDTH), row(R_WIDTH), row(R_WIDTH), row(R_WIDTH), row(GATE_W), row(d),
                  _mod_spec(mod, layer, mod_row),
                  stacked(lnw), stacked(lnb), stacked(n2), once(nf), once(eh),
                  stacked(wf), stacked(wr), stacked(wo), stacked(w1), stacked(w2)],
        out_specs=row(d),
        out_shape=jax.ShapeDtypeStruct((b, l, d), F32),
        compiler_params=_params("parallel", "parallel"),
        name="merge_mlp",
    )(f, y0, y1, g, bonus, gates, x, mod, lnw, lnb, n2, nf, eh, wf, wr, wo, w1, w2)


def _channel_dft():
    n = FGROUP_DIM
    jk = np.outer(np.arange(n), np.arange(n)) % n
    ang = 2.0 * np.pi * jk / n
    c = np.cos(ang) / np.sqrt(n)
    s = np.sin(ang) / np.sqrt(n)
    g = F_WIDTH // n
    out = np.zeros((F_WIDTH, 2 * F_WIDTH), np.float32)
    for i in range(g):
        out[i * n:(i + 1) * n, i * n:(i + 1) * n] = c
        out[i * n:(i + 1) * n, F_WIDTH + i * n:F_WIDTH + (i + 1) * n] = s
    return jnp.asarray(out)


def _position_dft(l):
    fold = (l // 2) % REV_BLOCK == 0
    nk = l // 2 if fold else l
    jk = np.outer(np.arange(l), np.arange(nk)) % l
    ang = 2.0 * np.pi * jk / l
    c = np.cos(ang) / np.sqrt(l)
    if fold:
        c[:, 0] *= 0.5
    cs = jnp.asarray(np.concatenate([c, -np.sin(ang) / np.sqrt(l)], axis=1).astype(np.float32)).astype(BF16)
    if not fold:
        return cs
    rev = np.zeros((2, REV_BLOCK, REV_BLOCK), np.float32)
    idx = np.arange(1, REV_BLOCK)
    rev[0, idx, REV_BLOCK - idx] = 1.0
    rev[1, 0, 0] = 1.0
    return cs, jnp.asarray(rev).astype(BF16)


def _head_ones():
    h = np.arange(R_WIDTH) // HEAD_DIM
    return jnp.asarray((h[:, None] == h[None, :]).astype(np.float32)).astype(BF16)


def _two_dir_lora(w):
    z = jnp.zeros_like(w[:, 0])
    return _bf(jnp.concatenate([jnp.concatenate([w[:, 0], z], axis=2),
                                jnp.concatenate([z, w[:, 1]], axis=2)], axis=1))


def _layer(x, mod_row, layer, grid_mode, need_out, final_norm, s0, p):
    rw_out = _inproj(x, p["mod"], mod_row, layer, p["n1"], p["win"], p["wz"][layer], need_out)
    reff, yloc, m, gs, g, bonus = _scan_pass(rw_out[0], grid_mode, layer, p["mu"], p["w0"], p["a0"], p["wup"],
                                             p["aup"], p["gup"], p["kkw"], p["ka"], p["rk"], p["eh"])
    if not need_out:
        return None, _state_pass(m, gs, reff, yloc, s0)[2]
    z, gates = rw_out[1], rw_out[2]
    cs = p["pos_dft"][x.shape[1]]
    if _can_fuse_state_dft(x.shape[0], x.shape[1], cs):
        y0, y1, s_fin, f = _state_dft_pass(m, gs, reff, yloc, s0, z, cs)
    else:
        y0, y1, s_fin = _state_pass(m, gs, reff, yloc, s0)
        f = _pos_dft(z, cs)
    x2 = _merge_mlp(f, y0, y1, g, bonus, gates, x, p["mod"], mod_row, layer, p["lnw"], p["lnb"], p["n2"],
                    p["nf"], p["eh"], p["wf"], p["wr"], p["wo"], p["w1"], p["w2"], final_norm)
    return x2, s_fin


def kernel(x, c, ctx, c_ctx, w_mod, b_mod, norm1, norm2, w_in, mu_shift, w0, w_up, a0, a_up, g_up,
           k_k, k_a, r_k, ln_x_w, ln_x_b, w_fourier_up, w_rwkv_up, w_out, mlp_w1, mlp_w2, norm_f):
    depth = w_mod.shape[0]
    batch, seq, d = x.shape
    ctx_len = ctx.shape[1]
    assert d == D_MODEL and batch + 1 <= MOD_ROWS
    assert seq % GRID_W == 0 and ctx_len % CHUNK == 0 and seq % CHUNK == 0

    cc = jnp.zeros((MOD_ROWS, d), F32).at[:batch].set(c).at[batch].set(c_ctx)
    cdft = _channel_dft()
    rows = lambda a: a.reshape(depth, 1, -1)
    p = {
        "eh": _head_ones(),
        "pos_dft": {n: _position_dft(n) for n in {seq, ctx_len}},
        "nf": norm_f.reshape(1, -1),
        "mod": _modulation(cc, w_mod, b_mod).reshape(depth, MOD_ROWS, N_MOD, d),
        "wz": [_fold_channel_dft(w_in, layer, cdft) for layer in range(depth)],
        "win": _bf(w_in),
        "n1": rows(norm1), "n2": rows(norm2), "mu": rows(mu_shift), "w0": rows(w0), "a0": rows(a0),
        "wup": _two_dir_lora(w_up), "aup": _two_dir_lora(a_up), "gup": _bf(g_up),
        "kkw": rows(k_k), "ka": rows(k_a), "rk": rows(r_k), "lnw": rows(ln_x_w), "lnb": rows(ln_x_b),
        "wf": _bf(w_fourier_up), "wr": _bf(w_rwkv_up), "wo": _bf(w_out),
        "w1": _bf(mlp_w1), "w2": _bf(mlp_w2),
    }
    lat_row = lambda bi: bi
    ctx_row = lambda bi: batch

    x_lat, x_ctx = x, ctx
    s_zero = jnp.zeros((2, batch, HEAD_DIM, R_WIDTH), F32)
    for layer in range(depth):
        last = layer == depth - 1
        x_ctx, s_ctx = _layer(x_ctx, ctx_row, layer, False, not last, False, s_zero, p)
        x_lat, _ = _layer(x_lat, lat_row, layer, True, True, last, s_ctx, p)
    return x_lat
```

```python
import functools

import numpy as np
import jax
import jax.numpy as jnp
from jax import lax
from jax.experimental import pallas as pl
from jax.experimental.pallas import tpu as pltpu

F32 = jnp.float32
BF16 = jnp.bfloat16

D_MODEL = 1024
GRID_W = 64
F_WIDTH = 512
FGROUP_DIM = 128
HEAD_DIM = 64
N_RHEADS = 8
R_WIDTH = N_RHEADS * HEAD_DIM
D_LORA = 64
D_GATE_LORA = 128
RWKV_IN = 3 * R_WIDTH + 4 * D_LORA + D_GATE_LORA
GATE_W = 2 * D_MODEL
D_FF = 4 * D_MODEL
N_MOD = 6
NORM_EPS = 1e-6
GN_EPS = 64e-5
L2_EPS = 1e-12

CHUNK = 64
PREP_ROWS = 128
GROUP_LANES = 256
HEADS_PER_GROUP = GROUP_LANES // HEAD_DIM
LANE_TILE = 128
REV_BLOCK = 256
DFT_PIECE = 128
N_GROUPS = R_WIDTH // GROUP_LANES
MOD_ROWS = 16
VMEM_LIMIT = 56 * 1024 * 1024


def _bf(x):
    return x.astype(BF16)


def _dot(a, b):
    return jnp.dot(a, b, preferred_element_type=F32)


def _dot_nt(a, b):
    return lax.dot_general(a, b, (((1,), (1,)), ((), ())), preferred_element_type=F32)


def _split2(x):
    hi = _bf(x)
    lo = _bf(x - hi.astype(F32))
    return hi, lo


def _head_sum(x, ones):
    return _dot(_bf(x), ones)


def _params(*sem):
    return pltpu.CompilerParams(dimension_semantics=sem, vmem_limit_bytes=VMEM_LIMIT)


def _const_spec(shape):
    zeros = (0,) * len(shape)
    return pl.BlockSpec(shape, lambda *_: zeros)


def _layer_spec(a, l, single_buffer=False):
    tail = tuple(a.shape[1:])
    idx = (l,) + (0,) * len(tail)
    kw = {"pipeline_mode": pl.Buffered(1)} if single_buffer else {}
    return pl.BlockSpec((None,) + tail, lambda *_: idx, **kw)


def _mod_spec(mod, l, mod_row):
    return pl.BlockSpec((None, 1) + tuple(mod.shape[2:]), lambda bi, i: (l, mod_row(bi), 0, 0))


def _modulation_kernel(x_ref, w_ref, b_ref, o_ref):
    x = x_ref[...]
    x = x * jax.nn.sigmoid(x)
    o_ref[0] = _dot(_bf(x), _bf(w_ref[0])) + b_ref[0]


def _modulation(cc, w, b):
    depth, d, n = w.shape
    tn = 2048
    return pl.pallas_call(
        _modulation_kernel,
        grid=(depth, n // tn),
        in_specs=[_const_spec(cc.shape),
                  pl.BlockSpec((1, d, tn), lambda l, j: (l, 0, j)),
                  pl.BlockSpec((1, 1, tn), lambda l, j: (l, 0, j))],
        out_specs=pl.BlockSpec((1, cc.shape[0], tn), lambda l, j: (l, 0, j)),
        out_shape=jax.ShapeDtypeStruct((depth, cc.shape[0], n), F32),
        compiler_params=_params("parallel", "parallel"),
        name="modulation",
    )(cc, w, b.reshape(depth, 1, n))


def _fold_kernel(w_ref, c_ref, o_ref):
    wh, wl = _split2(w_ref[...])
    ch, cl = _split2(c_ref[...])
    o_ref[...] = _bf(_dot(wh, ch) + _dot(wl, ch) + _dot(wh, cl))


def _fold_channel_dft(w_in, l, cdft):
    d = w_in.shape[1]
    return pl.pallas_call(
        _fold_kernel,
        grid=(1,),
        in_specs=[pl.BlockSpec((None, d, F_WIDTH), lambda i: (l, 0, 0)), _const_spec(cdft.shape)],
        out_specs=_const_spec((d, cdft.shape[1])),
        out_shape=jax.ShapeDtypeStruct((d, cdft.shape[1]), BF16),
        compiler_params=_params("arbitrary"),
        name="fold_channel_dft",
    )(w_in, cdft)


def _rms(x):
    return x * lax.rsqrt(jnp.mean(x * x, axis=-1, keepdims=True) + NORM_EPS)


def _inproj_kernel(x_ref, mod_ref, n1_all_ref, win_ref, *rest, need_out, layer):
    if need_out:
        wz_ref, rw_ref, z_ref, g_ref = rest
    else:
        (rw_ref,) = rest
    n1_ref = n1_all_ref.at[layer:layer + 1]
    shift = mod_ref[0, 0:1, :]
    scale = mod_ref[0, 1:2, :]
    tm = x_ref.shape[1]
    half = tm // 2 if tm % 32 == 0 else tm
    for r0 in range(0, tm, half):
        rows = slice(r0, r0 + half)
        h = _bf(_rms(x_ref[0, rows]) * n1_ref[...] * (1.0 + scale) + shift)
        rw_ref[0, rows] = _dot(h, win_ref[:, F_WIDTH:F_WIDTH + RWKV_IN])
        if need_out:
            z = _dot(h, wz_ref[...])
            z_ref[0, 0, rows] = _bf(z[:, :F_WIDTH])
            z_ref[0, 1, rows] = _bf(z[:, F_WIDTH:])
            g_ref[0, rows] = _bf(_dot(h, win_ref[:, F_WIDTH + RWKV_IN:]))


def _inproj(x, mod, mod_row, layer, n1, win, wz, need_out):
    b, l, d = x.shape
    tm = min(512, l)
    row_spec = lambda w: pl.BlockSpec((1, tm, w), lambda bi, i: (bi, i, 0))
    in_specs = [row_spec(d), _mod_spec(mod, layer, mod_row), _const_spec(n1.shape),
                _layer_spec(win, layer, single_buffer=True)]
    args = [x, mod, n1, win]
    out_specs = [row_spec(RWKV_IN)]
    out_shape = [jax.ShapeDtypeStruct((b, l, RWKV_IN), F32)]
    if need_out:
        in_specs += [_const_spec(wz.shape)]
        args += [wz]
        out_specs += [pl.BlockSpec((1, 2, tm, F_WIDTH), lambda bi, i: (bi, 0, i, 0)), row_spec(GATE_W)]
        out_shape += [jax.ShapeDtypeStruct((b, 2, l, F_WIDTH), BF16),
                      jax.ShapeDtypeStruct((b, l, GATE_W), BF16)]
    return pl.pallas_call(
        functools.partial(_inproj_kernel, need_out=need_out, layer=layer),
        grid=(b, l // tm),
        in_specs=in_specs, out_specs=out_specs, out_shape=out_shape,
        compiler_params=_params("parallel", "parallel"),
        name="inproj",
    )(*args)


def _log_sigmoid(x):
    return jnp.minimum(x, 0.0) - jnp.log1p(jnp.exp(-jnp.abs(x)))


P_KT, P_RT, P_KH, P_BH, P_V = 0, 2, 4, 6, 8
N_PLANES = 9
N_SLOTS = 2


def _co_emit(main, filler, every, start=0):
    for k, _ in enumerate(main):
        if k >= start and (k - start) % every == 0:
            next(filler, None)
    for _ in filler:
        pass


def _scan_kernel(*refs, grid_mode, seq_len, tm, n_tiles, n_steps, layer):
    n_in = len(refs) - 8
    in_refs = list(refs[:n_in])
    first_vec = 3 if grid_mode else 1
    for k in (first_vec, first_vec + 6, first_vec + 7):
        in_refs[k] = in_refs[k].at[layer:layer + 1]
    reff_ref, yloc_ref, m_ref, gs_ref, g_ref, bonus_ref, pp_ref, et_ref = refs[n_in:]
    s = pl.program_id(0)
    slot_prep = jnp.bitwise_and(s, 1)
    slot_alg = 1 - slot_prep
    tile = jnp.minimum(s, n_steps - 2) % n_tiles

    @pl.when(s == 0)
    def _():
        pp_ref[1] = jnp.zeros(pp_ref.shape[1:], pp_ref.dtype)
        et_ref[1] = jnp.zeros(et_ref.shape[1:], et_ref.dtype)

    prep = _prep_tile(in_refs, pp_ref, et_ref, slot_prep, g_ref, bonus_ref, tile,
                      grid_mode=grid_mode, seq_len=seq_len, tm=tm)
    alg = _chunk_algebra(pp_ref, et_ref, slot_alg, reff_ref, yloc_ref, m_ref, gs_ref, tc=tm)
    _co_emit(alg, prep, every=3)


def _prep_tile(in_refs, pp_ref, et_ref, slot, g_ref, bonus_ref, tile, *, grid_mode, seq_len, tm):
    sb = min(PREP_ROWS, tm) if grid_mode else tm
    for r0 in range(0, tm, sb):
        yield from _prep_rows(r0, sb, in_refs, pp_ref, et_ref, slot, g_ref, bonus_ref, tile,
                              grid_mode=grid_mode, seq_len=seq_len, tm=tm)


def _prep_rows(r0, sb, in_refs, pp_ref, et_ref, slot, g_ref, bonus_ref, tile, *, grid_mode, seq_len, tm):
    if grid_mode:
        (rw_ref, prev_ref, next_ref, mu_ref, w0_ref, a0_ref, wup_ref, aup_ref, gup_ref,
         kkw_ref, ka_ref, rk_ref, eh_ref) = in_refs
    else:
        (rw_ref, mu_ref, w0_ref, a0_ref, wup_ref, aup_ref, gup_ref,
         kkw_ref, ka_ref, rk_ref, eh_ref) = in_refs
    rows = slice(r0, r0 + sb)
    t_loc = lax.broadcasted_iota(jnp.int32, (sb, 128), 0) + r0
    t_glob = t_loc + tile * tm
    lane = lax.broadcasted_iota(jnp.int32, (sb, 128), 1)
    if grid_mode:
        col = jnp.bitwise_and(t_loc, GRID_W - 1)
        masks = [col != 0, col != GRID_W - 1, t_glob >= GRID_W, t_glob < seq_len - GRID_W]
        n_parts = 4
    else:
        masks = [t_glob != 0, t_glob != seq_len - 1]
        n_parts = 2
    part_w = RWKV_IN // n_parts

    def shifted(j, part):
        cols = slice(128 * j, 128 * (j + 1))
        if part == 0:
            s = pltpu.roll(rw_ref[0, rows, cols], 1, 0)
        elif part == 1:
            s = pltpu.roll(rw_ref[0, rows, cols], sb - 1, 0)
        elif part == 2:
            if r0 > 0:
                s = rw_ref[0, r0 - GRID_W:r0 + sb - GRID_W, cols]
            elif sb == GRID_W:
                s = prev_ref[0, :, cols]
            else:
                s = jnp.concatenate([prev_ref[0, :, cols], rw_ref[0, 0:sb - GRID_W, cols]], axis=0)
        else:
            if r0 + sb < tm:
                s = rw_ref[0, r0 + GRID_W:r0 + sb + GRID_W, cols]
            elif sb == GRID_W:
                s = next_ref[0, :, cols]
            else:
                s = jnp.concatenate([rw_ref[0, r0 + GRID_W:tm, cols], next_ref[0, :, cols]], axis=0)
        return jnp.where(masks[part], s, 0.0)

    blocks = []
    for j in range(RWKV_IN // 128):
        p_lo = (128 * j) // part_w
        p_hi = (128 * j + 127) // part_w
        s = shifted(j, p_lo)
        if p_hi != p_lo:
            s = jnp.where(lane + 128 * j < part_w * p_hi, s, shifted(j, p_hi))
        xj = rw_ref[0, rows, 128 * j:128 * (j + 1)]
        blocks.append(xj + mu_ref[:, 128 * j:128 * (j + 1)] * (s - xj))
        yield

    nb = R_WIDTH // 128
    r = jnp.concatenate(blocks[0:nb], axis=1)
    k = jnp.concatenate(blocks[nb:2 * nb], axis=1)
    v = jnp.concatenate(blocks[2 * nb:3 * nb], axis=1)
    wd, ad, gd = blocks[3 * nb], blocks[3 * nb + 1], blocks[3 * nb + 2]

    tw = _bf(jnp.tanh(wd))
    adb = _bf(ad)
    lw_cols, a_cols = [], []
    for c0 in range(0, 2 * R_WIDTH, GROUP_LANES):
        cols = slice(c0, c0 + GROUP_LANES)
        w_logit = w0_ref[:, cols] + _dot(tw, wup_ref[:, cols])
        lw_cols.append(-jnp.exp(_log_sigmoid(w_logit) - 0.5))
        yield
        a_cols.append(jax.nn.sigmoid(a0_ref[:, cols] + _dot(adb, aup_ref[:, cols])))
        yield
    lw = jnp.concatenate(lw_cols, axis=1)
    a = jnp.concatenate(a_cols, axis=1)
    g_ref[0, rows] = _bf(_dot(_bf(jax.nn.sigmoid(gd)), gup_ref[...]))
    yield

    eh = eh_ref[...]
    kx = k * kkw_ref[...]
    kk = kx / jnp.maximum(jnp.sqrt(_head_sum(kx * kx, eh)), L2_EPS)
    yield
    ka = ka_ref[...]
    kd = (k * (1.0 + (a[:, :R_WIDTH] - 1.0) * ka), k * (1.0 + (a[:, R_WIDTH:] - 1.0) * ka))
    bonus_ref[0, rows] = _bf(_head_sum(r * (kd[0] + kd[1]) * rk_ref[...], eh) * v)
    pp_ref[slot, P_V, rows] = _bf(v)
    yield

    t64 = lax.broadcasted_iota(jnp.int32, (CHUNK, CHUNK), 0)
    s64 = lax.broadcasted_iota(jnp.int32, (CHUNK, CHUNK), 1)
    tri = (jnp.where(s64 <= t64, 1.0, 0.0).astype(BF16), jnp.where(s64 >= t64, 1.0, 0.0).astype(BF16))
    for d in range(2):
        lw_d = lw[:, d * R_WIDTH:(d + 1) * R_WIDTH]
        b_d = kk * a[:, d * R_WIDTH:(d + 1) * R_WIDTH]
        for c0 in range(0, sb, CHUNK):
            loc = slice(c0, c0 + CHUNK)
            dst = slice(r0 + c0, r0 + c0 + CHUNK)
            lwc = lw_d[loc]
            hi, lo = _split2(lwc)
            cum = _dot(tri[d], hi) + _dot(tri[d], lo)
            e_neg = jnp.exp(-cum)
            pp_ref[slot, P_KT + d, dst] = _bf(kk[loc] * jnp.exp(cum - lwc))
            pp_ref[slot, P_RT + d, dst] = _bf(r[loc] * jnp.exp(cum))
            pp_ref[slot, P_KH + d, dst] = _bf(kd[d][loc] * e_neg)
            pp_ref[slot, P_BH + d, dst] = _bf(b_d[loc] * e_neg)
            et = jnp.exp(jnp.sum(lwc, axis=0, keepdims=True))
            et_ref[slot, d, (r0 + c0) // CHUNK] = jnp.broadcast_to(et, (8, R_WIDTH))
            yield


def _scan_pass(rw, grid_mode, layer, mu, w0c, a0c, wup, aup, gup, kkw, ka, rk, eh):
    b, l, _ = rw.shape
    tm = min(256, l) if grid_mode else l
    nt = l // tm
    n_steps = b * nt + 1
    nrow = l // GRID_W
    per = tm // GRID_W
    prep_bi = lambda s: (jnp.minimum(s, n_steps - 2) // nt, jnp.minimum(s, n_steps - 2) % nt)
    alg_bi = lambda s: (jnp.maximum(s - 1, 0) // nt, jnp.maximum(s - 1, 0) % nt)

    def rw_map(s):
        bi, i = prep_bi(s)
        return bi, i, 0

    def prev_map(s):
        bi, i = prep_bi(s)
        return bi, jnp.maximum(i * per - 1, 0), 0

    def next_map(s):
        bi, i = prep_bi(s)
        return bi, jnp.minimum((i + 1) * per, nrow - 1), 0

    def out_map(s):
        bi, i = alg_bi(s)
        return 0, bi, i, 0

    in_specs = [pl.BlockSpec((1, tm, RWKV_IN), rw_map)]
    args = [rw]
    if grid_mode:
        in_specs += [pl.BlockSpec((1, GRID_W, RWKV_IN), prev_map), pl.BlockSpec((1, GRID_W, RWKV_IN), next_map)]
        args += [rw, rw]
    stacked = [mu, w0c, a0c, wup, aup, gup, kkw, ka, rk]
    in_specs += [_const_spec(c.shape) if c.ndim == 2 else _layer_spec(c, layer) for c in stacked]
    in_specs += [_const_spec(eh.shape)]
    args += stacked + [eh]
    one = pl.BlockSpec((1, tm, R_WIDTH), rw_map)
    two = pl.BlockSpec((2, 1, tm, R_WIDTH), out_map)
    s1 = jax.ShapeDtypeStruct((b, l, R_WIDTH), BF16)
    s2 = jax.ShapeDtypeStruct((2, b, l, R_WIDTH), F32)
    s2h = jax.ShapeDtypeStruct((2, b, l, R_WIDTH), BF16)
    return pl.pallas_call(
        functools.partial(_scan_kernel, grid_mode=grid_mode, seq_len=l, tm=tm, n_tiles=nt, n_steps=n_steps,
                          layer=layer),
        grid=(n_steps,),
        in_specs=in_specs,
        out_specs=[two, two, two, two, one, one],
        out_shape=[s2h, s2h, s2, s2h, s1, s1],
        scratch_shapes=[pltpu.VMEM((N_SLOTS, N_PLANES, tm, R_WIDTH), BF16),
                        pltpu.VMEM((N_SLOTS, 2, tm // CHUNK, 8, R_WIDTH), F32)],
        compiler_params=_params("arbitrary"),
        name="rwkv_scan",
    )(*args)


def _block_diag_mask():
    return lax.broadcasted_iota(jnp.int32, (CHUNK, LANE_TILE), 1) < HEAD_DIM


def _block_diag(x, first_head):
    zero = jnp.zeros((CHUNK, LANE_TILE), x.dtype)
    per_tile = LANE_TILE // HEAD_DIM
    blocks = []
    for h in range(HEADS_PER_GROUP):
        tile = h // per_tile
        piece = x[:, tile * LANE_TILE:(tile + 1) * LANE_TILE]
        piece = jnp.where(first_head, piece, zero) if h % per_tile == 0 else jnp.where(first_head, zero, piece)
        row = [zero] * (GROUP_LANES // LANE_TILE)
        row[tile] = piece
        blocks.append(jnp.concatenate(row, axis=1))
    return jnp.concatenate(blocks, axis=0)


def _chunk_algebra(pp_ref, et_ref, slot, reff_ref, yloc_ref, m_ref, gs_ref, *, tc):
    t = lax.broadcasted_iota(jnp.int32, (CHUNK, GROUP_LANES), 0)
    lane = lax.broadcasted_iota(jnp.int32, (CHUNK, GROUP_LANES), 1)
    s = jnp.bitwise_and(lane, CHUNK - 1)
    eye = jnp.where(s == t, 1.0, 0.0)
    before = (s < t, s > t)
    upto = (s <= t, s >= t)
    bdmask = _block_diag_mask()
    bdiag = lambda x: _block_diag(_bf(x), bdmask)
    units = [(d, slice(ci * CHUNK, (ci + 1) * CHUNK), slice(gi * GROUP_LANES, (gi + 1) * GROUP_LANES))
             for d in range(2) for ci in range(tc // CHUNK) for gi in range(N_GROUPS)]
    nu = range(len(units))

    def staged(fn):
        out = []
        for u in nu:
            out.append(fn(u))
            yield
        return out

    kt = [pp_ref[slot, P_KT + d, rows, lanes] for d, rows, lanes in units]
    rt = [pp_ref[slot, P_RT + d, rows, lanes] for d, rows, lanes in units]
    vv = [pp_ref[slot, P_V, rows, lanes] for d, rows, lanes in units]
    dec = [eye * et_ref[slot, d, rows.start // CHUNK, 0:1, lanes] for d, rows, lanes in units]

    def nt_product(u, plane):
        d, rows, lanes = units[u]
        dh, dl = _split2(dec[u])
        lhs = jnp.concatenate([kt[u], rt[u], dh, dl], axis=0)
        s_all = _dot_nt(lhs, _block_diag(pp_ref[slot, plane + d, rows, lanes], bdmask))
        causal = jnp.where(before[d], s_all[:CHUNK], 0.0)
        incl = jnp.where(upto[d], s_all[CHUNK:2 * CHUNK], 0.0)
        probe = s_all[2 * CHUNK:3 * CHUNK] + s_all[3 * CHUNK:]
        return causal, incl, probe

    def key_stage(u):
        a_k, a_rk, kb_t = nt_product(u, P_KH)
        return _bf(jnp.concatenate([a_k, a_rk, kb_t], axis=0))

    def removal_stage(u):
        a_b, a_rb, bb_t = nt_product(u, P_BH)
        return -a_b, _bf(jnp.concatenate([a_rb, bb_t], axis=0))

    rows_v = yield from staged(key_stage)
    nb = yield from staged(removal_stage)
    n = [x[0] for x in nb]
    rows_b = [x[1] for x in nb]

    tinv = [eye + n[u] for u in nu]
    p = yield from staged(lambda u: _dot(_bf(n[u]), bdiag(n[u])))
    av = yield from staged(lambda u: _dot(rows_v[u], _block_diag(vv[u], bdmask)))

    def doubling(u):
        tp = _dot(_bf(jnp.concatenate([tinv[u], p[u]], axis=0)), bdiag(p[u]))
        return tinv[u] + tp[:CHUNK], tp[CHUNK:]

    for _ in range(4):
        both = yield from staged(doubling)
        tinv = [x[0] for x in both]
        p = [x[1] for x in both]
    tb = yield from staged(lambda u: _bf(tinv[u] + _dot(_bf(tinv[u]), bdiag(p[u]))))

    uk = yield from staged(lambda u: _bf(_dot(tb[u], _block_diag(kt[u], bdmask))))
    uv = yield from staged(lambda u: _bf(_dot(tb[u], bdiag(av[u][:CHUNK]))))
    for u, (d, rows, lanes) in enumerate(units):
        bk = _dot(rows_b[u], _block_diag(uk[u], bdmask))
        reff_ref[d, 0, rows, lanes] = _bf(rt[u].astype(F32) - bk[:CHUNK])
        m_ref[d, 0, rows, lanes] = dec[u] - bk[CHUNK:]
        yield
    for u, (d, rows, lanes) in enumerate(units):
        bv = _dot(rows_b[u], _block_diag(uv[u], bdmask))
        yloc_ref[d, 0, rows, lanes] = _bf(av[u][CHUNK:2 * CHUNK] - bv[:CHUNK])
        gs_ref[d, 0, rows, lanes] = _bf(av[u][2 * CHUNK:] - bv[CHUNK:])
        yield


def _state_steps(m0_ref, g0_ref, re0_ref, yl0_ref, m1_ref, g1_ref, re1_ref, yl1_ref, s0_ref,
                 y0_ref, y1_ref, h_ref, *, batch):
    c = pl.program_id(0)

    @pl.when(c == 0)
    def _():
        h_ref[...] = s0_ref[...]

    bdmask = _block_diag_mask()
    per_dir = ((m0_ref, g0_ref, re0_ref, yl0_ref, y0_ref), (m1_ref, g1_ref, re1_ref, yl1_ref, y1_ref))
    units = [(d, bi, slice(gi * GROUP_LANES, (gi + 1) * GROUP_LANES))
             for d in range(2) for bi in range(batch) for gi in range(N_GROUPS)]
    hs = [_split2(h_ref[d, bi, :, lanes]) for d, bi, lanes in units]
    ms = [_split2(per_dir[d][0][0, bi, :, lanes]) for d, bi, lanes in units]
    rs = [per_dir[d][2][0, bi, :, lanes] for d, bi, lanes in units]
    o1, o2 = [], []
    for u in range(len(units)):
        o1.append(_dot(jnp.concatenate([ms[u][0], ms[u][1], rs[u]], axis=0), _block_diag(hs[u][0], bdmask)))
        yield
    for u in range(len(units)):
        o2.append(_dot(jnp.concatenate([ms[u][0], rs[u]], axis=0), _block_diag(hs[u][1], bdmask)))
        yield
    for u, (d, bi, lanes) in enumerate(units):
        _, g_ref, _, yl_ref, y_ref = per_dir[d]
        mh_new = o1[u][0:CHUNK] + o1[u][CHUNK:2 * CHUNK] + o2[u][0:CHUNK]
        rh_new = o1[u][2 * CHUNK:] + o2[u][CHUNK:]
        y_ref[bi, :, lanes] = _bf(yl_ref[0, bi, :, lanes].astype(F32) + rh_new)
        h_ref[d, bi, :, lanes] = mh_new + g_ref[0, bi, :, lanes].astype(F32)
        yield


def _state_kernel(*refs, batch):
    for _ in _state_steps(*refs, batch=batch):
        pass


def _state_pass(m, g, reff, yloc, s0):
    _, b, l, _ = m.shape
    nc = l // CHUNK
    blk = (1, b, CHUNK, R_WIDTH)
    fwd = pl.BlockSpec(blk, lambda c: (0, 0, c, 0))
    bwd = pl.BlockSpec(blk, lambda c: (1, 0, nc - 1 - c, 0))
    st = pl.BlockSpec((2, b, HEAD_DIM, R_WIDTH), lambda c: (0, 0, 0, 0))
    sy = jax.ShapeDtypeStruct((b, l, R_WIDTH), BF16)
    return pl.pallas_call(
        functools.partial(_state_kernel, batch=b),
        grid=(nc,),
        in_specs=[fwd, fwd, fwd, fwd, bwd, bwd, bwd, bwd, st],
        out_specs=[pl.BlockSpec((b, CHUNK, R_WIDTH), lambda c: (0, c, 0)),
                   pl.BlockSpec((b, CHUNK, R_WIDTH), lambda c: (0, nc - 1 - c, 0)),
                   st],
        out_shape=[sy, sy, jax.ShapeDtypeStruct((2, b, HEAD_DIM, R_WIDTH), F32)],
        compiler_params=_params("arbitrary"),
        name="rwkv_state",
    )(m, g, reff, yloc, m, g, reff, yloc, s0)


def _posdft_kernel(cs_ref, z_ref, o_ref):
    l = z_ref.shape[2]
    z = z_ref[0].reshape(2 * l, F_WIDTH)
    o_ref[0] = _bf(_dot(cs_ref[...], z))


def _posdft_fold_steps(w_ref, rev_ref, z_ref, o_ref, f_ref, i, *, tm):
    l = z_ref.shape[2]
    half = l // 2
    nb = l // REV_BLOCK

    @pl.when(i == 0)
    def _():
        for part in range(2):
            for j in range(half // REV_BLOCK):
                blk = z_ref[0, part, j * REV_BLOCK:(j + 1) * REV_BLOCK, :]
                src_a = z_ref[0, part, l - (j + 1) * REV_BLOCK:l - j * REV_BLOCK, :]
                jb = (nb - j) % nb
                src_b = z_ref[0, part, jb * REV_BLOCK:(jb + 1) * REV_BLOCK, :]
                rev = _dot(rev_ref[0], src_a) + _dot(rev_ref[1], src_b)
                sgn = 1.0 if part == 0 else -1.0
                f_ref[part * half + j * REV_BLOCK:part * half + (j + 1) * REV_BLOCK, :] = _bf(
                    blk.astype(F32) + sgn * rev)

    mid = z_ref[0, 0, half:half + 1, :].astype(F32) * (1.0 / float(np.sqrt(l)))
    piece = min(DFT_PIECE, tm)
    base = 0 if w_ref.shape[0] == tm else pl.multiple_of(i * tm, piece)
    for r0 in range(0, tm, piece):
        row = lax.broadcasted_iota(jnp.int32, (piece, F_WIDTH), 0) + r0
        alt = (1 - 2 * jnp.bitwise_and(row, 1)).astype(F32)
        o_ref[0, r0:r0 + piece] = _bf(_dot(w_ref[pl.ds(base + r0, piece), :], f_ref[...]) + alt * mid)
        yield


def _posdft_fold_kernel(w_ref, rev_ref, z_ref, o_ref, f_ref, *, tm):
    for _ in _posdft_fold_steps(w_ref, rev_ref, z_ref, o_ref, f_ref, pl.program_id(1), tm=tm):
        pass


def _state_dft_kernel(*refs, batch, tm, n_tiles):
    state_refs = refs[:9] + refs[12:15]
    w_ref, rev_ref, z_ref = refs[9:12]
    o_ref, f_ref = refs[15:]
    i = pl.program_id(0) % n_tiles
    dft = _posdft_fold_steps(w_ref, rev_ref, z_ref, o_ref, f_ref, i, tm=tm)
    state_yields = 3 * 2 * batch * N_GROUPS
    dft_yields = -(-tm // DFT_PIECE)
    _co_emit(_state_steps(*state_refs, batch=batch), dft, every=max(1, state_yields // dft_yields))


def _state_dft_pass(m, g, reff, yloc, s0, z, cs):
    _, b, l, _ = m.shape
    nc = l // CHUNK
    tm = (b * l) // nc
    nt = l // tm
    w, rev = cs
    blk = (1, b, CHUNK, R_WIDTH)
    fwd = pl.BlockSpec(blk, lambda c: (0, 0, c, 0))
    bwd = pl.BlockSpec(blk, lambda c: (1, 0, nc - 1 - c, 0))
    st = pl.BlockSpec((2, b, HEAD_DIM, R_WIDTH), lambda c: (0, 0, 0, 0))
    sy = jax.ShapeDtypeStruct((b, l, R_WIDTH), BF16)
    return pl.pallas_call(
        functools.partial(_state_dft_kernel, batch=b, tm=tm, n_tiles=nt),
        grid=(nc,),
        in_specs=[fwd, fwd, fwd, fwd, bwd, bwd, bwd, bwd, st,
                  pl.BlockSpec((l, l), lambda c: (0, 0), pipeline_mode=pl.Buffered(1)),
                  _const_spec(rev.shape),
                  pl.BlockSpec((1, 2, l, F_WIDTH), lambda c: (c // nt, 0, 0, 0))],
        out_specs=[pl.BlockSpec((b, CHUNK, R_WIDTH), lambda c: (0, c, 0)),
                   pl.BlockSpec((b, CHUNK, R_WIDTH), lambda c: (0, nc - 1 - c, 0)),
                   st,
                   pl.BlockSpec((1, tm, F_WIDTH), lambda c: (c // nt, c % nt, 0))],
        out_shape=[sy, sy, jax.ShapeDtypeStruct((2, b, HEAD_DIM, R_WIDTH), F32),
                   jax.ShapeDtypeStruct((b, l, F_WIDTH), BF16)],
        scratch_shapes=[pltpu.VMEM((l, F_WIDTH), BF16)],
        compiler_params=_params("arbitrary"),
        name="rwkv_state_dft",
    )(m, g, reff, yloc, m, g, reff, yloc, s0, w, rev, z)


def _can_fuse_state_dft(b, l, cs):
    tm = b * CHUNK
    return isinstance(cs, tuple) and l % tm == 0 and tm % 16 == 0


def _pos_dft(z, cs):
    b, _, l, _ = z.shape
    tm = min(512, l)
    if isinstance(cs, tuple):
        w, rev = cs
        return pl.pallas_call(
            functools.partial(_posdft_fold_kernel, tm=tm),
            grid=(b, l // tm),
            in_specs=[pl.BlockSpec((tm, l), lambda bi, i: (i, 0)),
                      _const_spec(rev.shape),
                      pl.BlockSpec((1, 2, l, F_WIDTH), lambda bi, i: (bi, 0, 0, 0))],
            out_specs=pl.BlockSpec((1, tm, F_WIDTH), lambda bi, i: (bi, i, 0)),
            out_shape=jax.ShapeDtypeStruct((b, l, F_WIDTH), BF16),
            scratch_shapes=[pltpu.VMEM((l, F_WIDTH), BF16)],
            compiler_params=_params("parallel", "arbitrary"),
            name="pos_dft_fold",
        )(w, rev, z)
    return pl.pallas_call(
        _posdft_kernel,
        grid=(b, l // tm),
        in_specs=[pl.BlockSpec((tm, 2 * l), lambda bi, i: (i, 0)),
                  pl.BlockSpec((1, 2, l, F_WIDTH), lambda bi, i: (bi, 0, 0, 0))],
        out_specs=pl.BlockSpec((1, tm, F_WIDTH), lambda bi, i: (bi, i, 0)),
        out_shape=jax.ShapeDtypeStruct((b, l, F_WIDTH), BF16),
        compiler_params=_params("parallel", "parallel"),
        name="pos_dft",
    )(cs, z)


def _merge_mlp_kernel(f_ref, y0_ref, y1_ref, g_ref, bonus_ref, gates_ref, x_ref, mod_ref,
                      lnw_all_ref, lnb_all_ref, n2_all_ref, nf_ref, eh_ref, wf_ref, wr_ref, wo_ref, w1_ref, w2_ref,
                      o_ref, *, final_norm, layer):
    lnw_ref, lnb_ref, n2_ref = (r.at[layer:layer + 1] for r in (lnw_all_ref, lnb_all_ref, n2_all_ref))
    y = y0_ref[0].astype(F32) + y1_ref[0].astype(F32)
    eh = eh_ref[...]
    inv_n = 1.0 / HEAD_DIM
    mu = _head_sum(y, eh) * inv_n
    dlt = y - mu
    var = _head_sum(dlt * dlt, eh) * inv_n
    yn = dlt * lax.rsqrt(var + GN_EPS) * lnw_ref[...] + lnb_ref[...]
    rwkv = _bf((yn + bonus_ref[0].astype(F32)) * g_ref[0].astype(F32))
    fo = _dot(f_ref[0], wf_ref[...])
    ro = _dot(rwkv, wr_ref[...])
    gates = gates_ref[0].astype(F32)
    merged = jax.nn.sigmoid(gates[:, :D_MODEL]) * fo + jax.nn.sigmoid(gates[:, D_MODEL:]) * ro
    x = x_ref[0] + mod_ref[0, 2:3, :] * _dot(_bf(merged), wo_ref[...])
    h = _bf(_rms(x) * n2_ref[...] * (1.0 + mod_ref[0, 4:5, :]) + mod_ref[0, 3:4, :])
    acc = jnp.zeros(x.shape, F32)
    step = 1024
    for j in range(D_FF // step):
        u = jnp.maximum(_dot(h, w1_ref[:, j * step:(j + 1) * step]), 0.0)
        acc = acc + _dot(_bf(u * u), w2_ref[j * step:(j + 1) * step, :])
    x2 = x + mod_ref[0, 5:6, :] * acc
    if final_norm:
        x2 = _rms(x2) * nf_ref[...]
    o_ref[0] = x2


def _merge_mlp(f, y0, y1, g, bonus, gates, x, mod, mod_row, layer, lnw, lnb, n2, nf, eh, wf, wr, wo, w1, w2,
               final_norm):
    b, l, d = x.shape
    tm = min(512, l)
    row = lambda w: pl.BlockSpec((1, tm, w), lambda bi, i: (bi, i, 0))
    once = lambda a: pl.BlockSpec(a.shape, lambda bi, i: (0, 0), pipeline_mode=pl.Buffered(1))
    stacked = lambda a: _layer_spec(a, layer, single_buffer=True)
    return pl.pallas_call(
        functools.partial(_merge_mlp_kernel, final_norm=final_norm, layer=layer),
        grid=(b, l // tm),
        in_specs=[row(F_WIDTH), row(R_WIDTH), row(R_WIDTH), row(R_WIDTH), row(R_WIDTH), row(GATE_W), row(d),
                  _mod_spec(mod, layer, mod_row),
                  once(lnw), once(lnb), once(n2), once(nf), once(eh),
                  stacked(wf), stacked(wr), stacked(wo), stacked(w1), stacked(w2)],
        out_specs=row(d),
        out_shape=jax.ShapeDtypeStruct((b, l, d), F32),
        compiler_params=_params("parallel", "parallel"),
        name="merge_mlp",
    )(f, y0, y1, g, bonus, gates, x, mod, lnw, lnb, n2, nf, eh, wf, wr, wo, w1, w2)


def _channel_dft():
    n = FGROUP_DIM
    jk = np.outer(np.arange(n), np.arange(n)) % n
    ang = 2.0 * np.pi * jk / n
    c = np.cos(ang) / np.sqrt(n)
    s = np.sin(ang) / np.sqrt(n)
    g = F_WIDTH // n
    out = np.zeros((F_WIDTH, 2 * F_WIDTH), np.float32)
    for i in range(g):
        out[i * n:(i + 1) * n, i * n:(i + 1) * n] = c
        out[i * n:(i + 1) * n, F_WIDTH + i * n:F_WIDTH + (i + 1) * n] = s
    return jnp.asarray(out)


def _position_dft(l):
    fold = (l // 2) % REV_BLOCK == 0
    nk = l // 2 if fold else l
    jk = np.outer(np.arange(l), np.arange(nk)) % l
    ang = 2.0 * np.pi * jk / l
    c = np.cos(ang) / np.sqrt(l)
    if fold:
        c[:, 0] *= 0.5
    cs = jnp.asarray(np.concatenate([c, -np.sin(ang) / np.sqrt(l)], axis=1).astype(np.float32)).astype(BF16)
    if not fold:
        return cs
    rev = np.zeros((2, REV_BLOCK, REV_BLOCK), np.float32)
    idx = np.arange(1, REV_BLOCK)
    rev[0, idx, REV_BLOCK - idx] = 1.0
    rev[1, 0, 0] = 1.0
    return cs, jnp.asarray(rev).astype(BF16)


def _head_ones():
    h = np.arange(R_WIDTH) // HEAD_DIM
    return jnp.asarray((h[:, None] == h[None, :]).astype(np.float32)).astype(BF16)


def _two_dir_lora(w):
    z = jnp.zeros_like(w[:, 0])
    return _bf(jnp.concatenate([jnp.concatenate([w[:, 0], z], axis=2),
                                jnp.concatenate([z, w[:, 1]], axis=2)], axis=1))


def _layer(x, mod_row, layer, grid_mode, need_out, final_norm, s0, p):
    rw_out = _inproj(x, p["mod"], mod_row, layer, p["n1"], p["win"], p["wz"][layer], need_out)
    reff, yloc, m, gs, g, bonus = _scan_pass(rw_out[0], grid_mode, layer, p["mu"], p["w0"], p["a0"], p["wup"],
                                             p["aup"], p["gup"], p["kkw"], p["ka"], p["rk"], p["eh"])
    if not need_out:
        return None, _state_pass(m, gs, reff, yloc, s0)[2]
    z, gates = rw_out[1], rw_out[2]
    cs = p["pos_dft"][x.shape[1]]
    if _can_fuse_state_dft(x.shape[0], x.shape[1], cs):
        y0, y1, s_fin, f = _state_dft_pass(m, gs, reff, yloc, s0, z, cs)
    else:
        y0, y1, s_fin = _state_pass(m, gs, reff, yloc, s0)
        f = _pos_dft(z, cs)
    x2 = _merge_mlp(f, y0, y1, g, bonus, gates, x, p["mod"], mod_row, layer, p["lnw"], p["lnb"], p["n2"],
                    p["nf"], p["eh"], p["wf"], p["wr"], p["wo"], p["w1"], p["w2"], final_norm)
    return x2, s_fin


def kernel(x, c, ctx, c_ctx, w_mod, b_mod, norm1, norm2, w_in, mu_shift, w0, w_up, a0, a_up, g_up,
           k_k, k_a, r_k, ln_x_w, ln_x_b, w_fourier_up, w_rwkv_up, w_out, mlp_w1, mlp_w2, norm_f):
    depth = w_mod.shape[0]
    batch, seq, d = x.shape
    ctx_len = ctx.shape[1]
    assert d == D_MODEL and batch + 1 <= MOD_ROWS
    assert seq % GRID_W == 0 and ctx_len % CHUNK == 0 and seq % CHUNK == 0

    cc = jnp.zeros((MOD_ROWS, d), F32).at[:batch].set(c).at[batch].set(c_ctx)
    cdft = _channel_dft()
    rows = lambda a: a.reshape(depth, 1, -1)
    p = {
        "eh": _head_ones(),
        "pos_dft": {n: _position_dft(n) for n in {seq, ctx_len}},
        "nf": norm_f.reshape(1, -1),
        "mod": _modulation(cc, w_mod, b_mod).reshape(depth, MOD_ROWS, N_MOD, d),
        "wz": [_fold_channel_dft(w_in, layer, cdft) for layer in range(depth)],
        "win": _bf(w_in),
        "n1": norm1, "n2": norm2, "mu": mu_shift, "w0": rows(w0), "a0": rows(a0),
        "wup": _two_dir_lora(w_up), "aup": _two_dir_lora(a_up), "gup": _bf(g_up),
        "kkw": k_k, "ka": k_a, "rk": rows(r_k), "lnw": ln_x_w, "lnb": ln_x_b,
        "wf": _bf(w_fourier_up), "wr": _bf(w_rwkv_up), "wo": _bf(w_out),
        "w1": _bf(mlp_w1), "w2": _bf(mlp_w2),
    }
    lat_row = lambda bi: bi
    ctx_row = lambda bi: batch

    x_lat, x_ctx = x, ctx
    s_zero = jnp.zeros((2, batch, HEAD_DIM, R_WIDTH), F32)
    for layer in range(depth):
        last = layer == depth - 1
        x_ctx, s_ctx = _layer(x_ctx, ctx_row, layer, False, not last, False, s_zero, p)
        x_lat, _ = _layer(x_lat, lat_row, layer, True, True, last, s_ctx, p)
    return x_lat
```

```python
import functools

import numpy as np
import jax
import jax.numpy as jnp
from jax import lax
from jax.experimental import pallas as pl
from jax.experimental.pallas import tpu as pltpu

F32 = jnp.float32
BF16 = jnp.bfloat16

D_MODEL = 1024
GRID_W = 64
F_WIDTH = 512
FGROUP_DIM = 128
HEAD_DIM = 64
N_RHEADS = 8
R_WIDTH = N_RHEADS * HEAD_DIM
D_LORA = 64
D_GATE_LORA = 128
RWKV_IN = 3 * R_WIDTH + 4 * D_LORA + D_GATE_LORA
GATE_W = 2 * D_MODEL
D_FF = 4 * D_MODEL
N_MOD = 6
NORM_EPS = 1e-6
GN_EPS = 64e-5
L2_EPS = 1e-12

CHUNK = 64
PREP_ROWS = 128
GROUP_LANES = 256
HEADS_PER_GROUP = GROUP_LANES // HEAD_DIM
LANE_TILE = 128
REV_BLOCK = 256
DFT_PIECE = 128
N_GROUPS = R_WIDTH // GROUP_LANES
MOD_ROWS = 16
VMEM_LIMIT = 56 * 1024 * 1024


def _bf(x):
    return x.astype(BF16)


def _dot(a, b):
    return jnp.dot(a, b, preferred_element_type=F32)


def _dot_nt(a, b):
    return lax.dot_general(a, b, (((1,), (1,)), ((), ())), preferred_element_type=F32)


def _split2(x):
    hi = _bf(x)
    lo = _bf(x - hi.astype(F32))
    return hi, lo


def _head_sum(x, ones):
    return _dot(_bf(x), ones)


def _params(*sem):
    return pltpu.CompilerParams(dimension_semantics=sem, vmem_limit_bytes=VMEM_LIMIT)


def _const_spec(shape):
    zeros = (0,) * len(shape)
    return pl.BlockSpec(shape, lambda *_: zeros)


def _layer_spec(a, l, single_buffer=False):
    tail = tuple(a.shape[1:])
    idx = (l,) + (0,) * len(tail)
    kw = {"pipeline_mode": pl.Buffered(1)} if single_buffer else {}
    return pl.BlockSpec((None,) + tail, lambda *_: idx, **kw)


def _mod_spec(mod, l, mod_row):
    return pl.BlockSpec((None, 1) + tuple(mod.shape[2:]), lambda bi, i: (l, mod_row(bi), 0, 0))


def _modulation_kernel(x_ref, w_ref, b_ref, o_ref):
    x = x_ref[...]
    x = x * jax.nn.sigmoid(x)
    o_ref[0] = _dot(_bf(x), _bf(w_ref[0])) + b_ref[0]


def _modulation(cc, w, b):
    depth, d, n = w.shape
    tn = 2048
    return pl.pallas_call(
        _modulation_kernel,
        grid=(depth, n // tn),
        in_specs=[_const_spec(cc.shape),
                  pl.BlockSpec((1, d, tn), lambda l, j: (l, 0, j)),
                  pl.BlockSpec((1, 1, tn), lambda l, j: (l, 0, j))],
        out_specs=pl.BlockSpec((1, cc.shape[0], tn), lambda l, j: (l, 0, j)),
        out_shape=jax.ShapeDtypeStruct((depth, cc.shape[0], n), F32),
        compiler_params=_params("parallel", "parallel"),
        name="modulation",
    )(cc, w, b.reshape(depth, 1, n))


def _fold_kernel(w_ref, c_ref, o_ref):
    wh, wl = _split2(w_ref[...])
    ch, cl = _split2(c_ref[...])
    o_ref[...] = _bf(_dot(wh, ch) + _dot(wl, ch) + _dot(wh, cl))


def _fold_channel_dft(w_in, cdft):
    depth, d, _ = w_in.shape
    n = cdft.shape[1]
    return pl.pallas_call(
        _fold_kernel,
        grid=(depth,),
        in_specs=[pl.BlockSpec((None, d, F_WIDTH), lambda l: (l, 0, 0)), _const_spec(cdft.shape)],
        out_specs=pl.BlockSpec((None, d, n), lambda l: (l, 0, 0)),
        out_shape=jax.ShapeDtypeStruct((depth, d, n), BF16),
        compiler_params=_params("parallel"),
        name="fold_channel_dft",
    )(w_in, cdft)


def _rms(x):
    return x * lax.rsqrt(jnp.mean(x * x, axis=-1, keepdims=True) + NORM_EPS)


def _inproj_kernel(x_ref, mod_ref, n1_all_ref, win_ref, *rest, need_out, layer):
    if need_out:
        wz_ref, rw_ref, z_ref, g_ref = rest
    else:
        (rw_ref,) = rest
    n1_ref = n1_all_ref.at[layer:layer + 1]
    shift = mod_ref[0, 0:1, :]
    scale = mod_ref[0, 1:2, :]
    tm = x_ref.shape[1]
    half = tm // 2 if tm % 32 == 0 else tm
    for r0 in range(0, tm, half):
        rows = slice(r0, r0 + half)
        h = _bf(_rms(x_ref[0, rows]) * n1_ref[...] * (1.0 + scale) + shift)
        rw_ref[0, rows] = _dot(h, win_ref[:, F_WIDTH:F_WIDTH + RWKV_IN])
        if need_out:
            z = _dot(h, wz_ref[...])
            z_ref[0, 0, rows] = _bf(z[:, :F_WIDTH])
            z_ref[0, 1, rows] = _bf(z[:, F_WIDTH:])
            g_ref[0, rows] = _bf(_dot(h, win_ref[:, F_WIDTH + RWKV_IN:]))


def _inproj(x, mod, mod_row, layer, n1, win, wz, need_out):
    b, l, d = x.shape
    tm = min(512, l)
    row_spec = lambda w: pl.BlockSpec((1, tm, w), lambda bi, i: (bi, i, 0))
    in_specs = [row_spec(d), _mod_spec(mod, layer, mod_row), _const_spec(n1.shape),
                _layer_spec(win, layer, single_buffer=True)]
    args = [x, mod, n1, win]
    out_specs = [row_spec(RWKV_IN)]
    out_shape = [jax.ShapeDtypeStruct((b, l, RWKV_IN), F32)]
    if need_out:
        in_specs += [_layer_spec(wz, layer)]
        args += [wz]
        out_specs += [pl.BlockSpec((1, 2, tm, F_WIDTH), lambda bi, i: (bi, 0, i, 0)), row_spec(GATE_W)]
        out_shape += [jax.ShapeDtypeStruct((b, 2, l, F_WIDTH), BF16),
                      jax.ShapeDtypeStruct((b, l, GATE_W), BF16)]
    return pl.pallas_call(
        functools.partial(_inproj_kernel, need_out=need_out, layer=layer),
        grid=(b, l // tm),
        in_specs=in_specs, out_specs=out_specs, out_shape=out_shape,
        compiler_params=_params("parallel", "parallel"),
        name="inproj",
    )(*args)


def _log_sigmoid(x):
    return jnp.minimum(x, 0.0) - jnp.log1p(jnp.exp(-jnp.abs(x)))


P_KT, P_RT, P_KH, P_BH, P_V = 0, 2, 4, 6, 8
N_PLANES = 9
N_SLOTS = 2


def _co_emit(main, filler, every, start=0):
    for k, _ in enumerate(main):
        if k >= start and (k - start) % every == 0:
            next(filler, None)
    for _ in filler:
        pass


def _scan_kernel(*refs, grid_mode, seq_len, tm, n_tiles, n_steps, layer):
    n_in = len(refs) - 8
    in_refs = list(refs[:n_in])
    first_vec = 3 if grid_mode else 1
    for k in (first_vec, first_vec + 6, first_vec + 7):
        in_refs[k] = in_refs[k].at[layer:layer + 1]
    reff_ref, yloc_ref, m_ref, gs_ref, g_ref, bonus_ref, pp_ref, et_ref = refs[n_in:]
    s = pl.program_id(0)
    slot_prep = jnp.bitwise_and(s, 1)
    slot_alg = 1 - slot_prep
    tile = jnp.minimum(s, n_steps - 2) % n_tiles

    @pl.when(s == 0)
    def _():
        pp_ref[1] = jnp.zeros(pp_ref.shape[1:], pp_ref.dtype)
        et_ref[1] = jnp.zeros(et_ref.shape[1:], et_ref.dtype)

    prep = _prep_tile(in_refs, pp_ref, et_ref, slot_prep, g_ref, bonus_ref, tile,
                      grid_mode=grid_mode, seq_len=seq_len, tm=tm)
    alg = _chunk_algebra(pp_ref, et_ref, slot_alg, reff_ref, yloc_ref, m_ref, gs_ref, tc=tm)
    _co_emit(alg, prep, every=3)


def _prep_tile(in_refs, pp_ref, et_ref, slot, g_ref, bonus_ref, tile, *, grid_mode, seq_len, tm):
    sb = min(PREP_ROWS, tm) if grid_mode else tm
    for r0 in range(0, tm, sb):
        yield from _prep_rows(r0, sb, in_refs, pp_ref, et_ref, slot, g_ref, bonus_ref, tile,
                              grid_mode=grid_mode, seq_len=seq_len, tm=tm)


def _prep_rows(r0, sb, in_refs, pp_ref, et_ref, slot, g_ref, bonus_ref, tile, *, grid_mode, seq_len, tm):
    if grid_mode:
        (rw_ref, prev_ref, next_ref, mu_ref, w0_ref, a0_ref, wup_ref, aup_ref, gup_ref,
         kkw_ref, ka_ref, rk_ref, eh_ref) = in_refs
    else:
        (rw_ref, mu_ref, w0_ref, a0_ref, wup_ref, aup_ref, gup_ref,
         kkw_ref, ka_ref, rk_ref, eh_ref) = in_refs
    rows = slice(r0, r0 + sb)
    t_loc = lax.broadcasted_iota(jnp.int32, (sb, 128), 0) + r0
    t_glob = t_loc + tile * tm
    lane = lax.broadcasted_iota(jnp.int32, (sb, 128), 1)
    if grid_mode:
        col = jnp.bitwise_and(t_loc, GRID_W - 1)
        masks = [col != 0, col != GRID_W - 1, t_glob >= GRID_W, t_glob < seq_len - GRID_W]
        n_parts = 4
    else:
        masks = [t_glob != 0, t_glob != seq_len - 1]
        n_parts = 2
    part_w = RWKV_IN // n_parts

    def shifted(j, part):
        cols = slice(128 * j, 128 * (j + 1))
        if part == 0:
            s = pltpu.roll(rw_ref[0, rows, cols], 1, 0)
        elif part == 1:
            s = pltpu.roll(rw_ref[0, rows, cols], sb - 1, 0)
        elif part == 2:
            if r0 > 0:
                s = rw_ref[0, r0 - GRID_W:r0 + sb - GRID_W, cols]
            elif sb == GRID_W:
                s = prev_ref[0, :, cols]
            else:
                s = jnp.concatenate([prev_ref[0, :, cols], rw_ref[0, 0:sb - GRID_W, cols]], axis=0)
        else:
            if r0 + sb < tm:
                s = rw_ref[0, r0 + GRID_W:r0 + sb + GRID_W, cols]
            elif sb == GRID_W:
                s = next_ref[0, :, cols]
            else:
                s = jnp.concatenate([rw_ref[0, r0 + GRID_W:tm, cols], next_ref[0, :, cols]], axis=0)
        return jnp.where(masks[part], s, 0.0)

    blocks = []
    for j in range(RWKV_IN // 128):
        p_lo = (128 * j) // part_w
        p_hi = (128 * j + 127) // part_w
        s = shifted(j, p_lo)
        if p_hi != p_lo:
            s = jnp.where(lane + 128 * j < part_w * p_hi, s, shifted(j, p_hi))
        xj = rw_ref[0, rows, 128 * j:128 * (j + 1)]
        blocks.append(xj + mu_ref[:, 128 * j:128 * (j + 1)] * (s - xj))
        yield

    nb = R_WIDTH // 128
    r = jnp.concatenate(blocks[0:nb], axis=1)
    k = jnp.concatenate(blocks[nb:2 * nb], axis=1)
    v = jnp.concatenate(blocks[2 * nb:3 * nb], axis=1)
    wd, ad, gd = blocks[3 * nb], blocks[3 * nb + 1], blocks[3 * nb + 2]

    tw = _bf(jnp.tanh(wd))
    adb = _bf(ad)
    lw_cols, a_cols = [], []
    for c0 in range(0, 2 * R_WIDTH, GROUP_LANES):
        cols = slice(c0, c0 + GROUP_LANES)
        w_logit = w0_ref[:, cols] + _dot(tw, wup_ref[:, cols])
        lw_cols.append(-jnp.exp(_log_sigmoid(w_logit) - 0.5))
        yield
        a_cols.append(jax.nn.sigmoid(a0_ref[:, cols] + _dot(adb, aup_ref[:, cols])))
        yield
    lw = jnp.concatenate(lw_cols, axis=1)
    a = jnp.concatenate(a_cols, axis=1)
    g_ref[0, rows] = _bf(_dot(_bf(jax.nn.sigmoid(gd)), gup_ref[...]))
    yield

    eh = eh_ref[...]
    kx = k * kkw_ref[...]
    kk = kx / jnp.maximum(jnp.sqrt(_head_sum(kx * kx, eh)), L2_EPS)
    yield
    ka = ka_ref[...]
    kd = (k * (1.0 + (a[:, :R_WIDTH] - 1.0) * ka), k * (1.0 + (a[:, R_WIDTH:] - 1.0) * ka))
    bonus_ref[0, rows] = _bf(_head_sum(r * (kd[0] + kd[1]) * rk_ref[...], eh) * v)
    pp_ref[slot, P_V, rows] = _bf(v)
    yield

    t64 = lax.broadcasted_iota(jnp.int32, (CHUNK, CHUNK), 0)
    s64 = lax.broadcasted_iota(jnp.int32, (CHUNK, CHUNK), 1)
    tri = (jnp.where(s64 <= t64, 1.0, 0.0).astype(BF16), jnp.where(s64 >= t64, 1.0, 0.0).astype(BF16))
    for d in range(2):
        lw_d = lw[:, d * R_WIDTH:(d + 1) * R_WIDTH]
        b_d = kk * a[:, d * R_WIDTH:(d + 1) * R_WIDTH]
        for c0 in range(0, sb, CHUNK):
            loc = slice(c0, c0 + CHUNK)
            dst = slice(r0 + c0, r0 + c0 + CHUNK)
            lwc = lw_d[loc]
            hi, lo = _split2(lwc)
            cum = _dot(tri[d], hi) + _dot(tri[d], lo)
            e_neg = jnp.exp(-cum)
            pp_ref[slot, P_KT + d, dst] = _bf(kk[loc] * jnp.exp(cum - lwc))
            pp_ref[slot, P_RT + d, dst] = _bf(r[loc] * jnp.exp(cum))
            pp_ref[slot, P_KH + d, dst] = _bf(kd[d][loc] * e_neg)
            pp_ref[slot, P_BH + d, dst] = _bf(b_d[loc] * e_neg)
            et = jnp.exp(jnp.sum(lwc, axis=0, keepdims=True))
            et_ref[slot, d, (r0 + c0) // CHUNK] = jnp.broadcast_to(et, (8, R_WIDTH))
            yield


def _scan_pass(rw, grid_mode, layer, mu, w0c, a0c, wup, aup, gup, kkw, ka, rk, eh):
    b, l, _ = rw.shape
    tm = min(256, l) if grid_mode else l
    nt = l // tm
    n_steps = b * nt + 1
    nrow = l // GRID_W
    per = tm // GRID_W
    prep_bi = lambda s: (jnp.minimum(s, n_steps - 2) // nt, jnp.minimum(s, n_steps - 2) % nt)
    alg_bi = lambda s: (jnp.maximum(s - 1, 0) // nt, jnp.maximum(s - 1, 0) % nt)

    def rw_map(s):
        bi, i = prep_bi(s)
        return bi, i, 0

    def prev_map(s):
        bi, i = prep_bi(s)
        return bi, jnp.maximum(i * per - 1, 0), 0

    def next_map(s):
        bi, i = prep_bi(s)
        return bi, jnp.minimum((i + 1) * per, nrow - 1), 0

    def out_map(s):
        bi, i = alg_bi(s)
        return 0, bi, i, 0

    in_specs = [pl.BlockSpec((1, tm, RWKV_IN), rw_map)]
    args = [rw]
    if grid_mode:
        in_specs += [pl.BlockSpec((1, GRID_W, RWKV_IN), prev_map), pl.BlockSpec((1, GRID_W, RWKV_IN), next_map)]
        args += [rw, rw]
    stacked = [mu, w0c, a0c, wup, aup, gup, kkw, ka, rk]
    in_specs += [_const_spec(c.shape) if c.ndim == 2 else _layer_spec(c, layer) for c in stacked]
    in_specs += [_const_spec(eh.shape)]
    args += stacked + [eh]
    one = pl.BlockSpec((1, tm, R_WIDTH), rw_map)
    two = pl.BlockSpec((2, 1, tm, R_WIDTH), out_map)
    s1 = jax.ShapeDtypeStruct((b, l, R_WIDTH), BF16)
    s2 = jax.ShapeDtypeStruct((2, b, l, R_WIDTH), F32)
    s2h = jax.ShapeDtypeStruct((2, b, l, R_WIDTH), BF16)
    return pl.pallas_call(
        functools.partial(_scan_kernel, grid_mode=grid_mode, seq_len=l, tm=tm, n_tiles=nt, n_steps=n_steps,
                          layer=layer),
        grid=(n_steps,),
        in_specs=in_specs,
        out_specs=[two, two, two, two, one, one],
        out_shape=[s2h, s2h, s2, s2h, s1, s1],
        scratch_shapes=[pltpu.VMEM((N_SLOTS, N_PLANES, tm, R_WIDTH), BF16),
                        pltpu.VMEM((N_SLOTS, 2, tm // CHUNK, 8, R_WIDTH), F32)],
        compiler_params=_params("arbitrary"),
        name="rwkv_scan",
    )(*args)


def _block_diag_mask():
    return lax.broadcasted_iota(jnp.int32, (CHUNK, LANE_TILE), 1) < HEAD_DIM


def _block_diag(x, first_head):
    zero = jnp.zeros((CHUNK, LANE_TILE), x.dtype)
    per_tile = LANE_TILE // HEAD_DIM
    blocks = []
    for h in range(HEADS_PER_GROUP):
        tile = h // per_tile
        piece = x[:, tile * LANE_TILE:(tile + 1) * LANE_TILE]
        piece = jnp.where(first_head, piece, zero) if h % per_tile == 0 else jnp.where(first_head, zero, piece)
        row = [zero] * (GROUP_LANES // LANE_TILE)
        row[tile] = piece
        blocks.append(jnp.concatenate(row, axis=1))
    return jnp.concatenate(blocks, axis=0)


def _chunk_algebra(pp_ref, et_ref, slot, reff_ref, yloc_ref, m_ref, gs_ref, *, tc):
    t = lax.broadcasted_iota(jnp.int32, (CHUNK, GROUP_LANES), 0)
    lane = lax.broadcasted_iota(jnp.int32, (CHUNK, GROUP_LANES), 1)
    s = jnp.bitwise_and(lane, CHUNK - 1)
    eye = jnp.where(s == t, 1.0, 0.0)
    before = (s < t, s > t)
    upto = (s <= t, s >= t)
    bdmask = _block_diag_mask()
    bdiag = lambda x: _block_diag(_bf(x), bdmask)
    units = [(d, slice(ci * CHUNK, (ci + 1) * CHUNK), slice(gi * GROUP_LANES, (gi + 1) * GROUP_LANES))
             for d in range(2) for ci in range(tc // CHUNK) for gi in range(N_GROUPS)]
    nu = range(len(units))

    def staged(fn):
        out = []
        for u in nu:
            out.append(fn(u))
            yield
        return out

    kt = [pp_ref[slot, P_KT + d, rows, lanes] for d, rows, lanes in units]
    rt = [pp_ref[slot, P_RT + d, rows, lanes] for d, rows, lanes in units]
    vv = [pp_ref[slot, P_V, rows, lanes] for d, rows, lanes in units]
    dec = [eye * et_ref[slot, d, rows.start // CHUNK, 0:1, lanes] for d, rows, lanes in units]

    def nt_product(u, plane):
        d, rows, lanes = units[u]
        dh, dl = _split2(dec[u])
        lhs = jnp.concatenate([kt[u], rt[u], dh, dl], axis=0)
        s_all = _dot_nt(lhs, _block_diag(pp_ref[slot, plane + d, rows, lanes], bdmask))
        causal = jnp.where(before[d], s_all[:CHUNK], 0.0)
        incl = jnp.where(upto[d], s_all[CHUNK:2 * CHUNK], 0.0)
        probe = s_all[2 * CHUNK:3 * CHUNK] + s_all[3 * CHUNK:]
        return causal, incl, probe

    def key_stage(u):
        a_k, a_rk, kb_t = nt_product(u, P_KH)
        return _bf(jnp.concatenate([a_k, a_rk, kb_t], axis=0))

    def removal_stage(u):
        a_b, a_rb, bb_t = nt_product(u, P_BH)
        return -a_b, _bf(jnp.concatenate([a_rb, bb_t], axis=0))

    rows_v = yield from staged(key_stage)
    nb = yield from staged(removal_stage)
    n = [x[0] for x in nb]
    rows_b = [x[1] for x in nb]

    tinv = [eye + n[u] for u in nu]
    p = yield from staged(lambda u: _dot(_bf(n[u]), bdiag(n[u])))
    av = yield from staged(lambda u: _dot(rows_v[u], _block_diag(vv[u], bdmask)))

    def doubling(u):
        tp = _dot(_bf(jnp.concatenate([tinv[u], p[u]], axis=0)), bdiag(p[u]))
        return tinv[u] + tp[:CHUNK], tp[CHUNK:]

    for _ in range(4):
        both = yield from staged(doubling)
        tinv = [x[0] for x in both]
        p = [x[1] for x in both]
    tb = yield from staged(lambda u: _bf(tinv[u] + _dot(_bf(tinv[u]), bdiag(p[u]))))

    uk = yield from staged(lambda u: _bf(_dot(tb[u], _block_diag(kt[u], bdmask))))
    uv = yield from staged(lambda u: _bf(_dot(tb[u], bdiag(av[u][:CHUNK]))))
    for u, (d, rows, lanes) in enumerate(units):
        bk = _dot(rows_b[u], _block_diag(uk[u], bdmask))
        reff_ref[d, 0, rows, lanes] = _bf(rt[u].astype(F32) - bk[:CHUNK])
        m_ref[d, 0, rows, lanes] = dec[u] - bk[CHUNK:]
        yield
    for u, (d, rows, lanes) in enumerate(units):
        bv = _dot(rows_b[u], _block_diag(uv[u], bdmask))
        yloc_ref[d, 0, rows, lanes] = _bf(av[u][CHUNK:2 * CHUNK] - bv[:CHUNK])
        gs_ref[d, 0, rows, lanes] = _bf(av[u][2 * CHUNK:] - bv[CHUNK:])
        yield


def _state_steps(m0_ref, g0_ref, re0_ref, yl0_ref, m1_ref, g1_ref, re1_ref, yl1_ref, s0_ref,
                 y0_ref, y1_ref, h_ref, *, batch):
    c = pl.program_id(0)

    @pl.when(c == 0)
    def _():
        h_ref[...] = s0_ref[...]

    bdmask = _block_diag_mask()
    per_dir = ((m0_ref, g0_ref, re0_ref, yl0_ref, y0_ref), (m1_ref, g1_ref, re1_ref, yl1_ref, y1_ref))
    units = [(d, bi, slice(gi * GROUP_LANES, (gi + 1) * GROUP_LANES))
             for d in range(2) for bi in range(batch) for gi in range(N_GROUPS)]
    hs = [_split2(h_ref[d, bi, :, lanes]) for d, bi, lanes in units]
    ms = [_split2(per_dir[d][0][0, bi, :, lanes]) for d, bi, lanes in units]
    rs = [per_dir[d][2][0, bi, :, lanes] for d, bi, lanes in units]
    o1, o2 = [], []
    for u in range(len(units)):
        o1.append(_dot(jnp.concatenate([ms[u][0], ms[u][1], rs[u]], axis=0), _block_diag(hs[u][0], bdmask)))
        yield
    for u in range(len(units)):
        o2.append(_dot(jnp.concatenate([ms[u][0], rs[u]], axis=0), _block_diag(hs[u][1], bdmask)))
        yield
    for u, (d, bi, lanes) in enumerate(units):
        _, g_ref, _, yl_ref, y_ref = per_dir[d]
        mh_new = o1[u][0:CHUNK] + o1[u][CHUNK:2 * CHUNK] + o2[u][0:CHUNK]
        rh_new = o1[u][2 * CHUNK:] + o2[u][CHUNK:]
        y_ref[bi, :, lanes] = _bf(yl_ref[0, bi, :, lanes].astype(F32) + rh_new)
        h_ref[d, bi, :, lanes] = mh_new + g_ref[0, bi, :, lanes].astype(F32)
        yield


def _state_kernel(*refs, batch):
    for _ in _state_steps(*refs, batch=batch):
        pass


def _state_pass(m, g, reff, yloc, s0):
    _, b, l, _ = m.shape
    nc = l // CHUNK
    blk = (1, b, CHUNK, R_WIDTH)
    fwd = pl.BlockSpec(blk, lambda c: (0, 0, c, 0))
    bwd = pl.BlockSpec(blk, lambda c: (1, 0, nc - 1 - c, 0))
    st = pl.BlockSpec((2, b, HEAD_DIM, R_WIDTH), lambda c: (0, 0, 0, 0))
    sy = jax.ShapeDtypeStruct((b, l, R_WIDTH), BF16)
    return pl.pallas_call(
        functools.partial(_state_kernel, batch=b),
        grid=(nc,),
        in_specs=[fwd, fwd, fwd, fwd, bwd, bwd, bwd, bwd, st],
        out_specs=[pl.BlockSpec((b, CHUNK, R_WIDTH), lambda c: (0, c, 0)),
                   pl.BlockSpec((b, CHUNK, R_WIDTH), lambda c: (0, nc - 1 - c, 0)),
                   st],
        out_shape=[sy, sy, jax.ShapeDtypeStruct((2, b, HEAD_DIM, R_WIDTH), F32)],
        compiler_params=_params("arbitrary"),
        name="rwkv_state",
    )(m, g, reff, yloc, m, g, reff, yloc, s0)


def _posdft_kernel(cs_ref, z_ref, o_ref):
    l = z_ref.shape[2]
    z = z_ref[0].reshape(2 * l, F_WIDTH)
    o_ref[0] = _bf(_dot(cs_ref[...], z))


def _posdft_fold_steps(w_ref, rev_ref, z_ref, o_ref, f_ref, i, *, tm):
    l = z_ref.shape[2]
    half = l // 2
    nb = l // REV_BLOCK

    @pl.when(i == 0)
    def _():
        for part in range(2):
            for j in range(half // REV_BLOCK):
                blk = z_ref[0, part, j * REV_BLOCK:(j + 1) * REV_BLOCK, :]
                src_a = z_ref[0, part, l - (j + 1) * REV_BLOCK:l - j * REV_BLOCK, :]
                jb = (nb - j) % nb
                src_b = z_ref[0, part, jb * REV_BLOCK:(jb + 1) * REV_BLOCK, :]
                rev = _dot(rev_ref[0], src_a) + _dot(rev_ref[1], src_b)
                sgn = 1.0 if part == 0 else -1.0
                f_ref[part * half + j * REV_BLOCK:part * half + (j + 1) * REV_BLOCK, :] = _bf(
                    blk.astype(F32) + sgn * rev)

    mid = z_ref[0, 0, half:half + 1, :].astype(F32) * (1.0 / float(np.sqrt(l)))
    piece = min(DFT_PIECE, tm)
    base = 0 if w_ref.shape[0] == tm else pl.multiple_of(i * tm, piece)
    for r0 in range(0, tm, piece):
        row = lax.broadcasted_iota(jnp.int32, (piece, F_WIDTH), 0) + r0
        alt = (1 - 2 * jnp.bitwise_and(row, 1)).astype(F32)
        o_ref[0, r0:r0 + piece] = _bf(_dot(w_ref[pl.ds(base + r0, piece), :], f_ref[...]) + alt * mid)
        yield


def _posdft_fold_kernel(w_ref, rev_ref, z_ref, o_ref, f_ref, *, tm):
    for _ in _posdft_fold_steps(w_ref, rev_ref, z_ref, o_ref, f_ref, pl.program_id(1), tm=tm):
        pass


def _state_dft_kernel(*refs, batch, tm, n_tiles):
    state_refs = refs[:9] + refs[12:15]
    w_ref, rev_ref, z_ref = refs[9:12]
    o_ref, f_ref = refs[15:]
    i = pl.program_id(0) % n_tiles
    dft = _posdft_fold_steps(w_ref, rev_ref, z_ref, o_ref, f_ref, i, tm=tm)
    state_yields = 3 * 2 * batch * N_GROUPS
    dft_yields = -(-tm // DFT_PIECE)
    _co_emit(_state_steps(*state_refs, batch=batch), dft, every=max(1, state_yields // dft_yields))


def _state_dft_pass(m, g, reff, yloc, s0, z, cs):
    _, b, l, _ = m.shape
    nc = l // CHUNK
    tm = (b * l) // nc
    nt = l // tm
    w, rev = cs
    blk = (1, b, CHUNK, R_WIDTH)
    fwd = pl.BlockSpec(blk, lambda c: (0, 0, c, 0))
    bwd = pl.BlockSpec(blk, lambda c: (1, 0, nc - 1 - c, 0))
    st = pl.BlockSpec((2, b, HEAD_DIM, R_WIDTH), lambda c: (0, 0, 0, 0))
    sy = jax.ShapeDtypeStruct((b, l, R_WIDTH), BF16)
    return pl.pallas_call(
        functools.partial(_state_dft_kernel, batch=b, tm=tm, n_tiles=nt),
        grid=(nc,),
        in_specs=[fwd, fwd, fwd, fwd, bwd, bwd, bwd, bwd, st,
                  pl.BlockSpec((l, l), lambda c: (0, 0), pipeline_mode=pl.Buffered(1)),
                  _const_spec(rev.shape),
                  pl.BlockSpec((1, 2, l, F_WIDTH), lambda c: (c // nt, 0, 0, 0))],
        out_specs=[pl.BlockSpec((b, CHUNK, R_WIDTH), lambda c: (0, c, 0)),
                   pl.BlockSpec((b, CHUNK, R_WIDTH), lambda c: (0, nc - 1 - c, 0)),
                   st,
                   pl.BlockSpec((1, tm, F_WIDTH), lambda c: (c // nt, c % nt, 0))],
        out_shape=[sy, sy, jax.ShapeDtypeStruct((2, b, HEAD_DIM, R_WIDTH), F32),
                   jax.ShapeDtypeStruct((b, l, F_WIDTH), BF16)],
        scratch_shapes=[pltpu.VMEM((l, F_WIDTH), BF16)],
        compiler_params=_params("arbitrary"),
        name="rwkv_state_dft",
    )(m, g, reff, yloc, m, g, reff, yloc, s0, w, rev, z)


def _can_fuse_state_dft(b, l, cs):
    tm = b * CHUNK
    return isinstance(cs, tuple) and l % tm == 0 and tm % 16 == 0


def _pos_dft(z, cs):
    b, _, l, _ = z.shape
    tm = min(512, l)
    if isinstance(cs, tuple):
        w, rev = cs
        return pl.pallas_call(
            functools.partial(_posdft_fold_kernel, tm=tm),
            grid=(b, l // tm),
            in_specs=[pl.BlockSpec((tm, l), lambda bi, i: (i, 0)),
                      _const_spec(rev.shape),
                      pl.BlockSpec((1, 2, l, F_WIDTH), lambda bi, i: (bi, 0, 0, 0))],
            out_specs=pl.BlockSpec((1, tm, F_WIDTH), lambda bi, i: (bi, i, 0)),
            out_shape=jax.ShapeDtypeStruct((b, l, F_WIDTH), BF16),
            scratch_shapes=[pltpu.VMEM((l, F_WIDTH), BF16)],
            compiler_params=_params("parallel", "arbitrary"),
            name="pos_dft_fold",
        )(w, rev, z)
    return pl.pallas_call(
        _posdft_kernel,
        grid=(b, l // tm),
        in_specs=[pl.BlockSpec((tm, 2 * l), lambda bi, i: (i, 0)),
                  pl.BlockSpec((1, 2, l, F_WIDTH), lambda bi, i: (bi, 0, 0, 0))],
        out_specs=pl.BlockSpec((1, tm, F_WIDTH), lambda bi, i: (bi, i, 0)),
        out_shape=jax.ShapeDtypeStruct((b, l, F_WIDTH), BF16),
        compiler_params=_params("parallel", "parallel"),
        name="pos_dft",
    )(cs, z)


def _merge_mlp_kernel(f_ref, y0_ref, y1_ref, g_ref, bonus_ref, gates_ref, x_ref, mod_ref,
                      lnw_all_ref, lnb_all_ref, n2_all_ref, nf_ref, eh_ref, wf_ref, wr_ref, wo_ref, w1_ref, w2_ref,
                      o_ref, *, final_norm, layer):
    lnw_ref, lnb_ref, n2_ref = (r.at[layer:layer + 1] for r in (lnw_all_ref, lnb_all_ref, n2_all_ref))
    y = y0_ref[0].astype(F32) + y1_ref[0].astype(F32)
    eh = eh_ref[...]
    inv_n = 1.0 / HEAD_DIM
    mu = _head_sum(y, eh) * inv_n
    dlt = y - mu
    var = _head_sum(dlt * dlt, eh) * inv_n
    yn = dlt * lax.rsqrt(var + GN_EPS) * lnw_ref[...] + lnb_ref[...]
    rwkv = _bf((yn + bonus_ref[0].astype(F32)) * g_ref[0].astype(F32))
    fo = _dot(f_ref[0], wf_ref[...])
    ro = _dot(rwkv, wr_ref[...])
    gates = gates_ref[0].astype(F32)
    merged = jax.nn.sigmoid(gates[:, :D_MODEL]) * fo + jax.nn.sigmoid(gates[:, D_MODEL:]) * ro
    x = x_ref[0] + mod_ref[0, 2:3, :] * _dot(_bf(merged), wo_ref[...])
    h = _bf(_rms(x) * n2_ref[...] * (1.0 + mod_ref[0, 4:5, :]) + mod_ref[0, 3:4, :])
    acc = jnp.zeros(x.shape, F32)
    step = 1024
    for j in range(D_FF // step):
        u = jnp.maximum(_dot(h, w1_ref[:, j * step:(j + 1) * step]), 0.0)
        acc = acc + _dot(_bf(u * u), w2_ref[j * step:(j + 1) * step, :])
    x2 = x + mod_ref[0, 5:6, :] * acc
    if final_norm:
        x2 = _rms(x2) * nf_ref[...]
    o_ref[0] = x2


def _merge_mlp(f, y0, y1, g, bonus, gates, x, mod, mod_row, layer, lnw, lnb, n2, nf, eh, wf, wr, wo, w1, w2,
               final_norm):
    b, l, d = x.shape
    tm = min(512, l)
    row = lambda w: pl.BlockSpec((1, tm, w), lambda bi, i: (bi, i, 0))
    once = lambda a: pl.BlockSpec(a.shape, lambda bi, i: (0, 0), pipeline_mode=pl.Buffered(1))
    stacked = lambda a: _layer_spec(a, layer, single_buffer=True)
    return pl.pallas_call(
        functools.partial(_merge_mlp_kernel, final_norm=final_norm, layer=layer),
        grid=(b, l // tm),
        in_specs=[row(F_WIDTH), row(R_WIDTH), row(R_WIDTH), row(R_WIDTH), row(R_WIDTH), row(GATE_W), row(d),
                  _mod_spec(mod, layer, mod_row),
                  once(lnw), once(lnb), once(n2), once(nf), once(eh),
                  stacked(wf), stacked(wr), stacked(wo), stacked(w1), stacked(w2)],
        out_specs=row(d),
        out_shape=jax.ShapeDtypeStruct((b, l, d), F32),
        compiler_params=_params("parallel", "parallel"),
        name="merge_mlp",
    )(f, y0, y1, g, bonus, gates, x, mod, lnw, lnb, n2, nf, eh, wf, wr, wo, w1, w2)


def _channel_dft():
    n = FGROUP_DIM
    jk = np.outer(np.arange(n), np.arange(n)) % n
    ang = 2.0 * np.pi * jk / n
    c = np.cos(ang) / np.sqrt(n)
    s = np.sin(ang) / np.sqrt(n)
    g = F_WIDTH // n
    out = np.zeros((F_WIDTH, 2 * F_WIDTH), np.float32)
    for i in range(g):
        out[i * n:(i + 1) * n, i * n:(i + 1) * n] = c
        out[i * n:(i + 1) * n, F_WIDTH + i * n:F_WIDTH + (i + 1) * n] = s
    return jnp.asarray(out)


def _position_dft(l):
    fold = (l // 2) % REV_BLOCK == 0
    nk = l // 2 if fold else l
    jk = np.outer(np.arange(l), np.arange(nk)) % l
    ang = 2.0 * np.pi * jk / l
    c = np.cos(ang) / np.sqrt(l)
    if fold:
        c[:, 0] *= 0.5
    cs = jnp.asarray(np.concatenate([c, -np.sin(ang) / np.sqrt(l)], axis=1).astype(np.float32)).astype(BF16)
    if not fold:
        return cs
    rev = np.zeros((2, REV_BLOCK, REV_BLOCK), np.float32)
    idx = np.arange(1, REV_BLOCK)
    rev[0, idx, REV_BLOCK - idx] = 1.0
    rev[1, 0, 0] = 1.0
    return cs, jnp.asarray(rev).astype(BF16)


def _head_ones():
    h = np.arange(R_WIDTH) // HEAD_DIM
    return jnp.asarray((h[:, None] == h[None, :]).astype(np.float32)).astype(BF16)


def _two_dir_lora(w):
    z = jnp.zeros_like(w[:, 0])
    return _bf(jnp.concatenate([jnp.concatenate([w[:, 0], z], axis=2),
                                jnp.concatenate([z, w[:, 1]], axis=2)], axis=1))


def _layer(x, mod_row, layer, grid_mode, need_out, final_norm, s0, p):
    rw_out = _inproj(x, p["mod"], mod_row, layer, p["n1"], p["win"], p["wz"], need_out)
    reff, yloc, m, gs, g, bonus = _scan_pass(rw_out[0], grid_mode, layer, p["mu"], p["w0"], p["a0"], p["wup"],
                                             p["aup"], p["gup"], p["kkw"], p["ka"], p["rk"], p["eh"])
    if not need_out:
        return None, _state_pass(m, gs, reff, yloc, s0)[2]
    z, gates = rw_out[1], rw_out[2]
    cs = p["pos_dft"][x.shape[1]]
    if _can_fuse_state_dft(x.shape[0], x.shape[1], cs):
        y0, y1, s_fin, f = _state_dft_pass(m, gs, reff, yloc, s0, z, cs)
    else:
        y0, y1, s_fin = _state_pass(m, gs, reff, yloc, s0)
        f = _pos_dft(z, cs)
    x2 = _merge_mlp(f, y0, y1, g, bonus, gates, x, p["mod"], mod_row, layer, p["lnw"], p["lnb"], p["n2"],
                    p["nf"], p["eh"], p["wf"], p["wr"], p["wo"], p["w1"], p["w2"], final_norm)
    return x2, s_fin


def kernel(x, c, ctx, c_ctx, w_mod, b_mod, norm1, norm2, w_in, mu_shift, w0, w_up, a0, a_up, g_up,
           k_k, k_a, r_k, ln_x_w, ln_x_b, w_fourier_up, w_rwkv_up, w_out, mlp_w1, mlp_w2, norm_f):
    depth = w_mod.shape[0]
    batch, seq, d = x.shape
    ctx_len = ctx.shape[1]
    assert d == D_MODEL and batch + 1 <= MOD_ROWS
    assert seq % GRID_W == 0 and ctx_len % CHUNK == 0 and seq % CHUNK == 0

    cc = jnp.zeros((MOD_ROWS, d), F32).at[:batch].set(c).at[batch].set(c_ctx)
    cdft = _channel_dft()
    rows = lambda a: a.reshape(depth, 1, -1)
    p = {
        "eh": _head_ones(),
        "pos_dft": {n: _position_dft(n) for n in {seq, ctx_len}},
        "nf": norm_f.reshape(1, -1),
        "mod": _modulation(cc, w_mod, b_mod).reshape(depth, MOD_ROWS, N_MOD, d),
        "wz": _fold_channel_dft(w_in, cdft),
        "win": _bf(w_in),
        "n1": norm1, "n2": norm2, "mu": mu_shift, "w0": rows(w0), "a0": rows(a0),
        "wup": _two_dir_lora(w_up), "aup": _two_dir_lora(a_up), "gup": _bf(g_up),
        "kkw": k_k, "ka": k_a, "rk": rows(r_k), "lnw": ln_x_w, "lnb": ln_x_b,
        "wf": _bf(w_fourier_up), "wr": _bf(w_rwkv_up), "wo": _bf(w_out),
        "w1": _bf(mlp_w1), "w2": _bf(mlp_w2),
    }
    lat_row = lambda bi: bi
    ctx_row = lambda bi: batch

    x_lat, x_ctx = x, ctx
    s_zero = jnp.zeros((2, batch, HEAD_DIM, R_WIDTH), F32)
    for layer in range(depth):
        last = layer == depth - 1
        x_ctx, s_ctx = _layer(x_ctx, ctx_row, layer, False, not last, False, s_zero, p)
        x_lat, _ = _layer(x_lat, lat_row, layer, True, True, last, s_ctx, p)
    return x_lat
```

```python
import functools

import numpy as np
import jax
import jax.numpy as jnp
from jax import lax
from jax.experimental import pallas as pl
from jax.experimental.pallas import tpu as pltpu

F32 = jnp.float32
BF16 = jnp.bfloat16

D_MODEL = 1024
GRID_W = 64
F_WIDTH = 512
FGROUP_DIM = 128
HEAD_DIM = 64
N_RHEADS = 8
R_WIDTH = N_RHEADS * HEAD_DIM
D_LORA = 64
D_GATE_LORA = 128
RWKV_IN = 3 * R_WIDTH + 4 * D_LORA + D_GATE_LORA
GATE_W = 2 * D_MODEL
D_FF = 4 * D_MODEL
N_MOD = 6
NORM_EPS = 1e-6
GN_EPS = 64e-5
L2_EPS = 1e-12

CHUNK = 64
PREP_ROWS = 128
GROUP_LANES = 256
HEADS_PER_GROUP = GROUP_LANES // HEAD_DIM
LANE_TILE = 128
REV_BLOCK = 256
DFT_PIECE = 128
N_GROUPS = R_WIDTH // GROUP_LANES
MOD_ROWS = 16
VMEM_LIMIT = 56 * 1024 * 1024


def _bf(x):
    return x.astype(BF16)


def _dot(a, b):
    return jnp.dot(a, b, preferred_element_type=F32)


def _dot_nt(a, b):
    return lax.dot_general(a, b, (((1,), (1,)), ((), ())), preferred_element_type=F32)


def _split2(x):
    hi = _bf(x)
    lo = _bf(x - hi.astype(F32))
    return hi, lo


def _head_sum(x, ones):
    return _dot(_bf(x), ones)


def _params(*sem):
    return pltpu.CompilerParams(dimension_semantics=sem, vmem_limit_bytes=VMEM_LIMIT)


def _const_spec(shape):
    zeros = (0,) * len(shape)
    return pl.BlockSpec(shape, lambda *_: zeros)


def _layer_spec(a, l, single_buffer=False):
    tail = tuple(a.shape[1:])
    idx = (l,) + (0,) * len(tail)
    kw = {"pipeline_mode": pl.Buffered(1)} if single_buffer else {}
    return pl.BlockSpec((None,) + tail, lambda *_: idx, **kw)


def _mod_spec(mod, l, mod_row):
    return pl.BlockSpec((None, 1) + tuple(mod.shape[2:]), lambda bi, i: (l, mod_row(bi), 0, 0))


def _modulation_kernel(x_ref, w_ref, b_ref, o_ref):
    x = x_ref[...]
    x = x * jax.nn.sigmoid(x)
    o_ref[0] = _dot(_bf(x), _bf(w_ref[0])) + b_ref[0]


def _modulation(cc, w, b):
    depth, d, n = w.shape
    tn = 2048
    return pl.pallas_call(
        _modulation_kernel,
        grid=(depth, n // tn),
        in_specs=[_const_spec(cc.shape),
                  pl.BlockSpec((1, d, tn), lambda l, j: (l, 0, j)),
                  pl.BlockSpec((1, 1, tn), lambda l, j: (l, 0, j))],
        out_specs=pl.BlockSpec((1, cc.shape[0], tn), lambda l, j: (l, 0, j)),
        out_shape=jax.ShapeDtypeStruct((depth, cc.shape[0], n), F32),
        compiler_params=_params("parallel", "parallel"),
        name="modulation",
    )(cc, w, b.reshape(depth, 1, n))


def _fold_kernel(w_ref, c_ref, o_ref):
    wh, wl = _split2(w_ref[...])
    ch, cl = _split2(c_ref[...])
    o_ref[...] = _bf(_dot(wh, ch) + _dot(wl, ch) + _dot(wh, cl))


def _fold_channel_dft(w_in, cdft):
    depth, d, _ = w_in.shape
    n = cdft.shape[1]
    return pl.pallas_call(
        _fold_kernel,
        grid=(depth,),
        in_specs=[pl.BlockSpec((None, d, F_WIDTH), lambda l: (l, 0, 0)), _const_spec(cdft.shape)],
        out_specs=pl.BlockSpec((None, d, n), lambda l: (l, 0, 0)),
        out_shape=jax.ShapeDtypeStruct((depth, d, n), BF16),
        compiler_params=_params("parallel"),
        name="fold_channel_dft",
    )(w_in, cdft)


def _rms(x):
    return x * lax.rsqrt(jnp.mean(x * x, axis=-1, keepdims=True) + NORM_EPS)


def _inproj_kernel(x_ref, mod_ref, n1_all_ref, win_ref, *rest, need_out, layer):
    if need_out:
        wz_ref, rw_ref, z_ref, g_ref = rest
    else:
        (rw_ref,) = rest
    n1_ref = n1_all_ref.at[layer:layer + 1]
    shift = mod_ref[0, 0:1, :]
    scale = mod_ref[0, 1:2, :]
    tm = x_ref.shape[1]
    half = tm // 4 if tm % 64 == 0 else tm
    for r0 in range(0, tm, half):
        rows = slice(r0, r0 + half)
        h = _bf(_rms(x_ref[0, rows]) * n1_ref[...] * (1.0 + scale) + shift)
        rw_ref[0, rows] = _dot(h, win_ref[:, F_WIDTH:F_WIDTH + RWKV_IN])
        if need_out:
            z = _dot(h, wz_ref[...])
            z_ref[0, 0, rows] = _bf(z[:, :F_WIDTH])
            z_ref[0, 1, rows] = _bf(z[:, F_WIDTH:])
            g_ref[0, rows] = _bf(_dot(h, win_ref[:, F_WIDTH + RWKV_IN:]))


def _inproj(x, mod, mod_row, layer, n1, win, wz, need_out):
    b, l, d = x.shape
    tm = min(512, l)
    row_spec = lambda w: pl.BlockSpec((1, tm, w), lambda bi, i: (bi, i, 0))
    in_specs = [row_spec(d), _mod_spec(mod, layer, mod_row), _const_spec(n1.shape),
                _layer_spec(win, layer, single_buffer=True)]
    args = [x, mod, n1, win]
    out_specs = [row_spec(RWKV_IN)]
    out_shape = [jax.ShapeDtypeStruct((b, l, RWKV_IN), F32)]
    if need_out:
        in_specs += [_layer_spec(wz, layer)]
        args += [wz]
        out_specs += [pl.BlockSpec((1, 2, tm, F_WIDTH), lambda bi, i: (bi, 0, i, 0)), row_spec(GATE_W)]
        out_shape += [jax.ShapeDtypeStruct((b, 2, l, F_WIDTH), BF16),
                      jax.ShapeDtypeStruct((b, l, GATE_W), BF16)]
    return pl.pallas_call(
        functools.partial(_inproj_kernel, need_out=need_out, layer=layer),
        grid=(b, l // tm),
        in_specs=in_specs, out_specs=out_specs, out_shape=out_shape,
        compiler_params=_params("parallel", "parallel"),
        name="inproj",
    )(*args)


def _log_sigmoid(x):
    return jnp.minimum(x, 0.0) - jnp.log1p(jnp.exp(-jnp.abs(x)))


P_KT, P_RT, P_KH, P_BH, P_V = 0, 2, 4, 6, 8
N_PLANES = 9
N_SLOTS = 2


def _co_emit(main, filler, every, start=0):
    for k, _ in enumerate(main):
        if k >= start and (k - start) % every == 0:
            next(filler, None)
    for _ in filler:
        pass


def _scan_kernel(*refs, grid_mode, seq_len, tm, n_tiles, n_steps, layer):
    n_in = len(refs) - 8
    in_refs = list(refs[:n_in])
    first_vec = 3 if grid_mode else 1
    for k in (first_vec, first_vec + 6, first_vec + 7):
        in_refs[k] = in_refs[k].at[layer:layer + 1]
    reff_ref, yloc_ref, m_ref, gs_ref, g_ref, bonus_ref, pp_ref, et_ref = refs[n_in:]
    s = pl.program_id(0)
    slot_prep = jnp.bitwise_and(s, 1)
    slot_alg = 1 - slot_prep
    tile = jnp.minimum(s, n_steps - 2) % n_tiles

    @pl.when(s == 0)
    def _():
        pp_ref[1] = jnp.zeros(pp_ref.shape[1:], pp_ref.dtype)
        et_ref[1] = jnp.zeros(et_ref.shape[1:], et_ref.dtype)

    prep = _prep_tile(in_refs, pp_ref, et_ref, slot_prep, g_ref, bonus_ref, tile,
                      grid_mode=grid_mode, seq_len=seq_len, tm=tm)
    alg = _chunk_algebra(pp_ref, et_ref, slot_alg, reff_ref, yloc_ref, m_ref, gs_ref, tc=tm)
    _co_emit(alg, prep, every=3)


def _prep_tile(in_refs, pp_ref, et_ref, slot, g_ref, bonus_ref, tile, *, grid_mode, seq_len, tm):
    sb = min(PREP_ROWS, tm) if grid_mode else tm
    for r0 in range(0, tm, sb):
        yield from _prep_rows(r0, sb, in_refs, pp_ref, et_ref, slot, g_ref, bonus_ref, tile,
                              grid_mode=grid_mode, seq_len=seq_len, tm=tm)


def _prep_rows(r0, sb, in_refs, pp_ref, et_ref, slot, g_ref, bonus_ref, tile, *, grid_mode, seq_len, tm):
    if grid_mode:
        (rw_ref, prev_ref, next_ref, mu_ref, w0_ref, a0_ref, wup_ref, aup_ref, gup_ref,
         kkw_ref, ka_ref, rk_ref, eh_ref) = in_refs
    else:
        (rw_ref, mu_ref, w0_ref, a0_ref, wup_ref, aup_ref, gup_ref,
         kkw_ref, ka_ref, rk_ref, eh_ref) = in_refs
    rows = slice(r0, r0 + sb)
    t_loc = lax.broadcasted_iota(jnp.int32, (sb, 128), 0) + r0
    t_glob = t_loc + tile * tm
    lane = lax.broadcasted_iota(jnp.int32, (sb, 128), 1)
    if grid_mode:
        col = jnp.bitwise_and(t_loc, GRID_W - 1)
        masks = [col != 0, col != GRID_W - 1, t_glob >= GRID_W, t_glob < seq_len - GRID_W]
        n_parts = 4
    else:
        masks = [t_glob != 0, t_glob != seq_len - 1]
        n_parts = 2
    part_w = RWKV_IN // n_parts

    def shifted(j, part):
        cols = slice(128 * j, 128 * (j + 1))
        if part == 0:
            s = pltpu.roll(rw_ref[0, rows, cols], 1, 0)
        elif part == 1:
            s = pltpu.roll(rw_ref[0, rows, cols], sb - 1, 0)
        elif part == 2:
            if r0 > 0:
                s = rw_ref[0, r0 - GRID_W:r0 + sb - GRID_W, cols]
            elif sb == GRID_W:
                s = prev_ref[0, :, cols]
            else:
                s = jnp.concatenate([prev_ref[0, :, cols], rw_ref[0, 0:sb - GRID_W, cols]], axis=0)
        else:
            if r0 + sb < tm:
                s = rw_ref[0, r0 + GRID_W:r0 + sb + GRID_W, cols]
            elif sb == GRID_W:
                s = next_ref[0, :, cols]
            else:
                s = jnp.concatenate([rw_ref[0, r0 + GRID_W:tm, cols], next_ref[0, :, cols]], axis=0)
        return jnp.where(masks[part], s, 0.0)

    blocks = []
    for j in range(RWKV_IN // 128):
        p_lo = (128 * j) // part_w
        p_hi = (128 * j + 127) // part_w
        s = shifted(j, p_lo)
        if p_hi != p_lo:
            s = jnp.where(lane + 128 * j < part_w * p_hi, s, shifted(j, p_hi))
        xj = rw_ref[0, rows, 128 * j:128 * (j + 1)]
        blocks.append(xj + mu_ref[:, 128 * j:128 * (j + 1)] * (s - xj))
        yield

    nb = R_WIDTH // 128
    r = jnp.concatenate(blocks[0:nb], axis=1)
    k = jnp.concatenate(blocks[nb:2 * nb], axis=1)
    v = jnp.concatenate(blocks[2 * nb:3 * nb], axis=1)
    wd, ad, gd = blocks[3 * nb], blocks[3 * nb + 1], blocks[3 * nb + 2]

    tw = _bf(jnp.tanh(wd))
    adb = _bf(ad)
    lw_cols, a_cols = [], []
    for c0 in range(0, 2 * R_WIDTH, GROUP_LANES):
        cols = slice(c0, c0 + GROUP_LANES)
        w_logit = w0_ref[:, cols] + _dot(tw, wup_ref[:, cols])
        lw_cols.append(-jnp.exp(_log_sigmoid(w_logit) - 0.5))
        yield
        a_cols.append(jax.nn.sigmoid(a0_ref[:, cols] + _dot(adb, aup_ref[:, cols])))
        yield
    lw = jnp.concatenate(lw_cols, axis=1)
    a = jnp.concatenate(a_cols, axis=1)
    g_ref[0, rows] = _bf(_dot(_bf(jax.nn.sigmoid(gd)), gup_ref[...]))
    yield

    eh = eh_ref[...]
    kx = k * kkw_ref[...]
    kk = kx / jnp.maximum(jnp.sqrt(_head_sum(kx * kx, eh)), L2_EPS)
    yield
    ka = ka_ref[...]
    kd = (k * (1.0 + (a[:, :R_WIDTH] - 1.0) * ka), k * (1.0 + (a[:, R_WIDTH:] - 1.0) * ka))
    bonus_ref[0, rows] = _bf(_head_sum(r * (kd[0] + kd[1]) * rk_ref[...], eh) * v)
    pp_ref[slot, P_V, rows] = _bf(v)
    yield

    t64 = lax.broadcasted_iota(jnp.int32, (CHUNK, CHUNK), 0)
    s64 = lax.broadcasted_iota(jnp.int32, (CHUNK, CHUNK), 1)
    tri = (jnp.where(s64 <= t64, 1.0, 0.0).astype(BF16), jnp.where(s64 >= t64, 1.0, 0.0).astype(BF16))
    for d in range(2):
        lw_d = lw[:, d * R_WIDTH:(d + 1) * R_WIDTH]
        b_d = kk * a[:, d * R_WIDTH:(d + 1) * R_WIDTH]
        for c0 in range(0, sb, CHUNK):
            loc = slice(c0, c0 + CHUNK)
            dst = slice(r0 + c0, r0 + c0 + CHUNK)
            lwc = lw_d[loc]
            hi, lo = _split2(lwc)
            cum = _dot(tri[d], hi) + _dot(tri[d], lo)
            e_neg = jnp.exp(-cum)
            pp_ref[slot, P_KT + d, dst] = _bf(kk[loc] * jnp.exp(cum - lwc))
            pp_ref[slot, P_RT + d, dst] = _bf(r[loc] * jnp.exp(cum))
            pp_ref[slot, P_KH + d, dst] = _bf(kd[d][loc] * e_neg)
            pp_ref[slot, P_BH + d, dst] = _bf(b_d[loc] * e_neg)
            et = jnp.exp(jnp.sum(lwc, axis=0, keepdims=True))
            et_ref[slot, d, (r0 + c0) // CHUNK] = jnp.broadcast_to(et, (8, R_WIDTH))
            yield


def _scan_pass(rw, grid_mode, layer, mu, w0c, a0c, wup, aup, gup, kkw, ka, rk, eh):
    b, l, _ = rw.shape
    tm = min(256, l) if grid_mode else l
    nt = l // tm
    n_steps = b * nt + 1
    nrow = l // GRID_W
    per = tm // GRID_W
    prep_bi = lambda s: (jnp.minimum(s, n_steps - 2) // nt, jnp.minimum(s, n_steps - 2) % nt)
    alg_bi = lambda s: (jnp.maximum(s - 1, 0) // nt, jnp.maximum(s - 1, 0) % nt)

    def rw_map(s):
        bi, i = prep_bi(s)
        return bi, i, 0

    def prev_map(s):
        bi, i = prep_bi(s)
        return bi, jnp.maximum(i * per - 1, 0), 0

    def next_map(s):
        bi, i = prep_bi(s)
        return bi, jnp.minimum((i + 1) * per, nrow - 1), 0

    def out_map(s):
        bi, i = alg_bi(s)
        return 0, bi, i, 0

    in_specs = [pl.BlockSpec((1, tm, RWKV_IN), rw_map)]
    args = [rw]
    if grid_mode:
        in_specs += [pl.BlockSpec((1, GRID_W, RWKV_IN), prev_map), pl.BlockSpec((1, GRID_W, RWKV_IN), next_map)]
        args += [rw, rw]
    stacked = [mu, w0c, a0c, wup, aup, gup, kkw, ka, rk]
    in_specs += [_const_spec(c.shape) if c.ndim == 2 else _layer_spec(c, layer) for c in stacked]
    in_specs += [_const_spec(eh.shape)]
    args += stacked + [eh]
    one = pl.BlockSpec((1, tm, R_WIDTH), rw_map)
    two = pl.BlockSpec((2, 1, tm, R_WIDTH), out_map)
    s1 = jax.ShapeDtypeStruct((b, l, R_WIDTH), BF16)
    s2 = jax.ShapeDtypeStruct((2, b, l, R_WIDTH), F32)
    s2h = jax.ShapeDtypeStruct((2, b, l, R_WIDTH), BF16)
    return pl.pallas_call(
        functools.partial(_scan_kernel, grid_mode=grid_mode, seq_len=l, tm=tm, n_tiles=nt, n_steps=n_steps,
                          layer=layer),
        grid=(n_steps,),
        in_specs=in_specs,
        out_specs=[two, two, two, two, one, one],
        out_shape=[s2h, s2h, s2, s2h, s1, s1],
        scratch_shapes=[pltpu.VMEM((N_SLOTS, N_PLANES, tm, R_WIDTH), BF16),
                        pltpu.VMEM((N_SLOTS, 2, tm // CHUNK, 8, R_WIDTH), F32)],
        compiler_params=_params("arbitrary"),
        name="rwkv_scan",
    )(*args)


def _block_diag_mask():
    return lax.broadcasted_iota(jnp.int32, (CHUNK, LANE_TILE), 1) < HEAD_DIM


def _block_diag(x, first_head):
    zero = jnp.zeros((CHUNK, LANE_TILE), x.dtype)
    per_tile = LANE_TILE // HEAD_DIM
    blocks = []
    for h in range(HEADS_PER_GROUP):
        tile = h // per_tile
        piece = x[:, tile * LANE_TILE:(tile + 1) * LANE_TILE]
        piece = jnp.where(first_head, piece, zero) if h % per_tile == 0 else jnp.where(first_head, zero, piece)
        row = [zero] * (GROUP_LANES // LANE_TILE)
        row[tile] = piece
        blocks.append(jnp.concatenate(row, axis=1))
    return jnp.concatenate(blocks, axis=0)


def _chunk_algebra(pp_ref, et_ref, slot, reff_ref, yloc_ref, m_ref, gs_ref, *, tc):
    t = lax.broadcasted_iota(jnp.int32, (CHUNK, GROUP_LANES), 0)
    lane = lax.broadcasted_iota(jnp.int32, (CHUNK, GROUP_LANES), 1)
    s = jnp.bitwise_and(lane, CHUNK - 1)
    eye = jnp.where(s == t, 1.0, 0.0)
    before = (s < t, s > t)
    upto = (s <= t, s >= t)
    bdmask = _block_diag_mask()
    bdiag = lambda x: _block_diag(_bf(x), bdmask)
    units = [(d, slice(ci * CHUNK, (ci + 1) * CHUNK), slice(gi * GROUP_LANES, (gi + 1) * GROUP_LANES))
             for d in range(2) for ci in range(tc // CHUNK) for gi in range(N_GROUPS)]
    nu = range(len(units))

    def staged(fn):
        out = []
        for u in nu:
            out.append(fn(u))
            yield
        return out

    kt = [pp_ref[slot, P_KT + d, rows, lanes] for d, rows, lanes in units]
    rt = [pp_ref[slot, P_RT + d, rows, lanes] for d, rows, lanes in units]
    vv = [pp_ref[slot, P_V, rows, lanes] for d, rows, lanes in units]
    dec = [eye * et_ref[slot, d, rows.start // CHUNK, 0:1, lanes] for d, rows, lanes in units]

    def nt_product(u, plane):
        d, rows, lanes = units[u]
        dh, dl = _split2(dec[u])
        lhs = jnp.concatenate([kt[u], rt[u], dh, dl], axis=0)
        s_all = _dot_nt(lhs, _block_diag(pp_ref[slot, plane + d, rows, lanes], bdmask))
        causal = jnp.where(before[d], s_all[:CHUNK], 0.0)
        incl = jnp.where(upto[d], s_all[CHUNK:2 * CHUNK], 0.0)
        probe = s_all[2 * CHUNK:3 * CHUNK] + s_all[3 * CHUNK:]
        return causal, incl, probe

    def key_stage(u):
        a_k, a_rk, kb_t = nt_product(u, P_KH)
        return _bf(jnp.concatenate([a_k, a_rk, kb_t], axis=0))

    def removal_stage(u):
        a_b, a_rb, bb_t = nt_product(u, P_BH)
        return -a_b, _bf(jnp.concatenate([a_rb, bb_t], axis=0))

    rows_v = yield from staged(key_stage)
    nb = yield from staged(removal_stage)
    n = [x[0] for x in nb]
    rows_b = [x[1] for x in nb]

    tinv = [eye + n[u] for u in nu]
    p = yield from staged(lambda u: _dot(_bf(n[u]), bdiag(n[u])))
    av = yield from staged(lambda u: _dot(rows_v[u], _block_diag(vv[u], bdmask)))

    def doubling(u):
        tp = _dot(_bf(jnp.concatenate([tinv[u], p[u]], axis=0)), bdiag(p[u]))
        return tinv[u] + tp[:CHUNK], tp[CHUNK:]

    for _ in range(4):
        both = yield from staged(doubling)
        tinv = [x[0] for x in both]
        p = [x[1] for x in both]
    tb = yield from staged(lambda u: _bf(tinv[u] + _dot(_bf(tinv[u]), bdiag(p[u]))))

    uk = yield from staged(lambda u: _bf(_dot(tb[u], _block_diag(kt[u], bdmask))))
    uv = yield from staged(lambda u: _bf(_dot(tb[u], bdiag(av[u][:CHUNK]))))
    for u, (d, rows, lanes) in enumerate(units):
        bk = _dot(rows_b[u], _block_diag(uk[u], bdmask))
        reff_ref[d, 0, rows, lanes] = _bf(rt[u].astype(F32) - bk[:CHUNK])
        m_ref[d, 0, rows, lanes] = dec[u] - bk[CHUNK:]
        yield
    for u, (d, rows, lanes) in enumerate(units):
        bv = _dot(rows_b[u], _block_diag(uv[u], bdmask))
        yloc_ref[d, 0, rows, lanes] = _bf(av[u][CHUNK:2 * CHUNK] - bv[:CHUNK])
        gs_ref[d, 0, rows, lanes] = _bf(av[u][2 * CHUNK:] - bv[CHUNK:])
        yield


def _state_steps(m0_ref, g0_ref, re0_ref, yl0_ref, m1_ref, g1_ref, re1_ref, yl1_ref, s0_ref,
                 y0_ref, y1_ref, h_ref, *, batch):
    c = pl.program_id(0)

    @pl.when(c == 0)
    def _():
        h_ref[...] = s0_ref[...]

    bdmask = _block_diag_mask()
    per_dir = ((m0_ref, g0_ref, re0_ref, yl0_ref, y0_ref), (m1_ref, g1_ref, re1_ref, yl1_ref, y1_ref))
    units = [(d, bi, slice(gi * GROUP_LANES, (gi + 1) * GROUP_LANES))
             for d in range(2) for bi in range(batch) for gi in range(N_GROUPS)]
    hs = [_split2(h_ref[d, bi, :, lanes]) for d, bi, lanes in units]
    ms = [_split2(per_dir[d][0][0, bi, :, lanes]) for d, bi, lanes in units]
    rs = [per_dir[d][2][0, bi, :, lanes] for d, bi, lanes in units]
    o1, o2 = [], []
    for u in range(len(units)):
        o1.append(_dot(jnp.concatenate([ms[u][0], ms[u][1], rs[u]], axis=0), _block_diag(hs[u][0], bdmask)))
        yield
    for u in range(len(units)):
        o2.append(_dot(jnp.concatenate([ms[u][0], rs[u]], axis=0), _block_diag(hs[u][1], bdmask)))
        yield
    for u, (d, bi, lanes) in enumerate(units):
        _, g_ref, _, yl_ref, y_ref = per_dir[d]
        mh_new = o1[u][0:CHUNK] + o1[u][CHUNK:2 * CHUNK] + o2[u][0:CHUNK]
        rh_new = o1[u][2 * CHUNK:] + o2[u][CHUNK:]
        y_ref[bi, :, lanes] = _bf(yl_ref[0, bi, :, lanes].astype(F32) + rh_new)
        h_ref[d, bi, :, lanes] = mh_new + g_ref[0, bi, :, lanes].astype(F32)
        yield


def _state_kernel(*refs, batch):
    for _ in _state_steps(*refs, batch=batch):
        pass


def _state_pass(m, g, reff, yloc, s0):
    _, b, l, _ = m.shape
    nc = l // CHUNK
    blk = (1, b, CHUNK, R_WIDTH)
    fwd = pl.BlockSpec(blk, lambda c: (0, 0, c, 0))
    bwd = pl.BlockSpec(blk, lambda c: (1, 0, nc - 1 - c, 0))
    st = pl.BlockSpec((2, b, HEAD_DIM, R_WIDTH), lambda c: (0, 0, 0, 0))
    sy = jax.ShapeDtypeStruct((b, l, R_WIDTH), BF16)
    return pl.pallas_call(
        functools.partial(_state_kernel, batch=b),
        grid=(nc,),
        in_specs=[fwd, fwd, fwd, fwd, bwd, bwd, bwd, bwd, st],
        out_specs=[pl.BlockSpec((b, CHUNK, R_WIDTH), lambda c: (0, c, 0)),
                   pl.BlockSpec((b, CHUNK, R_WIDTH), lambda c: (0, nc - 1 - c, 0)),
                   st],
        out_shape=[sy, sy, jax.ShapeDtypeStruct((2, b, HEAD_DIM, R_WIDTH), F32)],
        compiler_params=_params("arbitrary"),
        name="rwkv_state",
    )(m, g, reff, yloc, m, g, reff, yloc, s0)


def _posdft_kernel(cs_ref, z_ref, o_ref):
    l = z_ref.shape[2]
    z = z_ref[0].reshape(2 * l, F_WIDTH)
    o_ref[0] = _bf(_dot(cs_ref[...], z))


def _posdft_fold_steps(w_ref, rev_ref, z_ref, o_ref, f_ref, i, *, tm):
    l = z_ref.shape[2]
    half = l // 2
    nb = l // REV_BLOCK

    @pl.when(i == 0)
    def _():
        for part in range(2):
            for j in range(half // REV_BLOCK):
                blk = z_ref[0, part, j * REV_BLOCK:(j + 1) * REV_BLOCK, :]
                src_a = z_ref[0, part, l - (j + 1) * REV_BLOCK:l - j * REV_BLOCK, :]
                jb = (nb - j) % nb
                src_b = z_ref[0, part, jb * REV_BLOCK:(jb + 1) * REV_BLOCK, :]
                rev = _dot(rev_ref[0], src_a) + _dot(rev_ref[1], src_b)
                sgn = 1.0 if part == 0 else -1.0
                f_ref[part * half + j * REV_BLOCK:part * half + (j + 1) * REV_BLOCK, :] = _bf(
                    blk.astype(F32) + sgn * rev)

    mid = z_ref[0, 0, half:half + 1, :].astype(F32) * (1.0 / float(np.sqrt(l)))
    piece = min(DFT_PIECE, tm)
    base = 0 if w_ref.shape[0] == tm else pl.multiple_of(i * tm, piece)
    for r0 in range(0, tm, piece):
        row = lax.broadcasted_iota(jnp.int32, (piece, F_WIDTH), 0) + r0
        alt = (1 - 2 * jnp.bitwise_and(row, 1)).astype(F32)
        o_ref[0, r0:r0 + piece] = _bf(_dot(w_ref[pl.ds(base + r0, piece), :], f_ref[...]) + alt * mid)
        yield


def _posdft_fold_kernel(w_ref, rev_ref, z_ref, o_ref, f_ref, *, tm):
    for _ in _posdft_fold_steps(w_ref, rev_ref, z_ref, o_ref, f_ref, pl.program_id(1), tm=tm):
        pass


def _state_dft_kernel(*refs, batch, tm, n_tiles):
    state_refs = refs[:9] + refs[12:15]
    w_ref, rev_ref, z_ref = refs[9:12]
    o_ref, f_ref = refs[15:]
    i = pl.program_id(0) % n_tiles
    dft = _posdft_fold_steps(w_ref, rev_ref, z_ref, o_ref, f_ref, i, tm=tm)
    state_yields = 3 * 2 * batch * N_GROUPS
    dft_yields = -(-tm // DFT_PIECE)
    _co_emit(_state_steps(*state_refs, batch=batch), dft, every=max(1, state_yields // dft_yields))


def _state_dft_pass(m, g, reff, yloc, s0, z, cs):
    _, b, l, _ = m.shape
    nc = l // CHUNK
    tm = (b * l) // nc
    nt = l // tm
    w, rev = cs
    blk = (1, b, CHUNK, R_WIDTH)
    fwd = pl.BlockSpec(blk, lambda c: (0, 0, c, 0))
    bwd = pl.BlockSpec(blk, lambda c: (1, 0, nc - 1 - c, 0))
    st = pl.BlockSpec((2, b, HEAD_DIM, R_WIDTH), lambda c: (0, 0, 0, 0))
    sy = jax.ShapeDtypeStruct((b, l, R_WIDTH), BF16)
    return pl.pallas_call(
        functools.partial(_state_dft_kernel, batch=b, tm=tm, n_tiles=nt),
        grid=(nc,),
        in_specs=[fwd, fwd, fwd, fwd, bwd, bwd, bwd, bwd, st,
                  pl.BlockSpec((l, l), lambda c: (0, 0), pipeline_mode=pl.Buffered(1)),
                  _const_spec(rev.shape),
                  pl.BlockSpec((1, 2, l, F_WIDTH), lambda c: (c // nt, 0, 0, 0))],
        out_specs=[pl.BlockSpec((b, CHUNK, R_WIDTH), lambda c: (0, c, 0)),
                   pl.BlockSpec((b, CHUNK, R_WIDTH), lambda c: (0, nc - 1 - c, 0)),
                   st,
                   pl.BlockSpec((1, tm, F_WIDTH), lambda c: (c // nt, c % nt, 0))],
        out_shape=[sy, sy, jax.ShapeDtypeStruct((2, b, HEAD_DIM, R_WIDTH), F32),
                   jax.ShapeDtypeStruct((b, l, F_WIDTH), BF16)],
        scratch_shapes=[pltpu.VMEM((l, F_WIDTH), BF16)],
        compiler_params=_params("arbitrary"),
        name="rwkv_state_dft",
    )(m, g, reff, yloc, m, g, reff, yloc, s0, w, rev, z)


def _can_fuse_state_dft(b, l, cs):
    tm = b * CHUNK
    return isinstance(cs, tuple) and l % tm == 0 and tm % 16 == 0


def _pos_dft(z, cs):
    b, _, l, _ = z.shape
    tm = min(512, l)
    if isinstance(cs, tuple):
        w, rev = cs
        return pl.pallas_call(
            functools.partial(_posdft_fold_kernel, tm=tm),
            grid=(b, l // tm),
            in_specs=[pl.BlockSpec((tm, l), lambda bi, i: (i, 0)),
                      _const_spec(rev.shape),
                      pl.BlockSpec((1, 2, l, F_WIDTH), lambda bi, i: (bi, 0, 0, 0))],
            out_specs=pl.BlockSpec((1, tm, F_WIDTH), lambda bi, i: (bi, i, 0)),
            out_shape=jax.ShapeDtypeStruct((b, l, F_WIDTH), BF16),
            scratch_shapes=[pltpu.VMEM((l, F_WIDTH), BF16)],
            compiler_params=_params("parallel", "arbitrary"),
            name="pos_dft_fold",
        )(w, rev, z)
    return pl.pallas_call(
        _posdft_kernel,
        grid=(b, l // tm),
        in_specs=[pl.BlockSpec((tm, 2 * l), lambda bi, i: (i, 0)),
                  pl.BlockSpec((1, 2, l, F_WIDTH), lambda bi, i: (bi, 0, 0, 0))],
        out_specs=pl.BlockSpec((1, tm, F_WIDTH), lambda bi, i: (bi, i, 0)),
        out_shape=jax.ShapeDtypeStruct((b, l, F_WIDTH), BF16),
        compiler_params=_params("parallel", "parallel"),
        name="pos_dft",
    )(cs, z)


def _merge_mlp_kernel(f_ref, y0_ref, y1_ref, g_ref, bonus_ref, gates_ref, x_ref, mod_ref,
                      lnw_all_ref, lnb_all_ref, n2_all_ref, nf_ref, eh_ref, wf_ref, wr_ref, wo_ref, w1_ref, w2_ref,
                      o_ref, *, final_norm, layer):
    lnw_ref, lnb_ref, n2_ref = (r.at[layer:layer + 1] for r in (lnw_all_ref, lnb_all_ref, n2_all_ref))
    y = y0_ref[0].astype(F32) + y1_ref[0].astype(F32)
    eh = eh_ref[...]
    inv_n = 1.0 / HEAD_DIM
    mu = _head_sum(y, eh) * inv_n
    dlt = y - mu
    var = _head_sum(dlt * dlt, eh) * inv_n
    yn = dlt * lax.rsqrt(var + GN_EPS) * lnw_ref[...] + lnb_ref[...]
    rwkv = _bf((yn + bonus_ref[0].astype(F32)) * g_ref[0].astype(F32))
    fo = _dot(f_ref[0], wf_ref[...])
    ro = _dot(rwkv, wr_ref[...])
    gates = gates_ref[0].astype(F32)
    merged = jax.nn.sigmoid(gates[:, :D_MODEL]) * fo + jax.nn.sigmoid(gates[:, D_MODEL:]) * ro
    x = x_ref[0] + mod_ref[0, 2:3, :] * _dot(_bf(merged), wo_ref[...])
    h = _bf(_rms(x) * n2_ref[...] * (1.0 + mod_ref[0, 4:5, :]) + mod_ref[0, 3:4, :])
    acc = jnp.zeros(x.shape, F32)
    step = 1024
    for j in range(D_FF // step):
        u = jnp.maximum(_dot(h, w1_ref[:, j * step:(j + 1) * step]), 0.0)
        acc = acc + _dot(_bf(u * u), w2_ref[j * step:(j + 1) * step, :])
    x2 = x + mod_ref[0, 5:6, :] * acc
    if final_norm:
        x2 = _rms(x2) * nf_ref[...]
    o_ref[0] = x2


def _merge_mlp(f, y0, y1, g, bonus, gates, x, mod, mod_row, layer, lnw, lnb, n2, nf, eh, wf, wr, wo, w1, w2,
               final_norm):
    b, l, d = x.shape
    tm = min(512, l)
    row = lambda w: pl.BlockSpec((1, tm, w), lambda bi, i: (bi, i, 0))
    once = lambda a: pl.BlockSpec(a.shape, lambda bi, i: (0, 0), pipeline_mode=pl.Buffered(1))
    stacked = lambda a: _layer_spec(a, layer, single_buffer=True)
    return pl.pallas_call(
        functools.partial(_merge_mlp_kernel, final_norm=final_norm, layer=layer),
        grid=(b, l // tm),
        in_specs=[row(F_WIDTH), row(R_WIDTH), row(R_WIDTH), row(R_WIDTH), row(R_WIDTH), row(GATE_W), row(d),
                  _mod_spec(mod, layer, mod_row),
                  once(lnw), once(lnb), once(n2), once(nf), once(eh),
                  stacked(wf), stacked(wr), stacked(wo), stacked(w1), stacked(w2)],
        out_specs=row(d),
        out_shape=jax.ShapeDtypeStruct((b, l, d), F32),
        compiler_params=_params("parallel", "parallel"),
        name="merge_mlp",
    )(f, y0, y1, g, bonus, gates, x, mod, lnw, lnb, n2, nf, eh, wf, wr, wo, w1, w2)


def _channel_dft():
    n = FGROUP_DIM
    jk = np.outer(np.arange(n), np.arange(n)) % n
    ang = 2.0 * np.pi * jk / n
    c = np.cos(ang) / np.sqrt(n)
    s = np.sin(ang) / np.sqrt(n)
    g = F_WIDTH // n
    out = np.zeros((F_WIDTH, 2 * F_WIDTH), np.float32)
    for i in range(g):
        out[i * n:(i + 1) * n, i * n:(i + 1) * n] = c
        out[i * n:(i + 1) * n, F_WIDTH + i * n:F_WIDTH + (i + 1) * n] = s
    return jnp.asarray(out)


def _position_dft(l):
    fold = (l // 2) % REV_BLOCK == 0
    nk = l // 2 if fold else l
    jk = np.outer(np.arange(l), np.arange(nk)) % l
    ang = 2.0 * np.pi * jk / l
    c = np.cos(ang) / np.sqrt(l)
    if fold:
        c[:, 0] *= 0.5
    cs = jnp.asarray(np.concatenate([c, -np.sin(ang) / np.sqrt(l)], axis=1).astype(np.float32)).astype(BF16)
    if not fold:
        return cs
    rev = np.zeros((2, REV_BLOCK, REV_BLOCK), np.float32)
    idx = np.arange(1, REV_BLOCK)
    rev[0, idx, REV_BLOCK - idx] = 1.0
    rev[1, 0, 0] = 1.0
    return cs, jnp.asarray(rev).astype(BF16)


def _head_ones():
    h = np.arange(R_WIDTH) // HEAD_DIM
    return jnp.asarray((h[:, None] == h[None, :]).astype(np.float32)).astype(BF16)


def _two_dir_lora(w):
    z = jnp.zeros_like(w[:, 0])
    return _bf(jnp.concatenate([jnp.concatenate([w[:, 0], z], axis=2),
                                jnp.concatenate([z, w[:, 1]], axis=2)], axis=1))


def _layer(x, mod_row, layer, grid_mode, need_out, final_norm, s0, p):
    rw_out = _inproj(x, p["mod"], mod_row, layer, p["n1"], p["win"], p["wz"], need_out)
    reff, yloc, m, gs, g, bonus = _scan_pass(rw_out[0], grid_mode, layer, p["mu"], p["w0"], p["a0"], p["wup"],
                                             p["aup"], p["gup"], p["kkw"], p["ka"], p["rk"], p["eh"])
    if not need_out:
        return None, _state_pass(m, gs, reff, yloc, s0)[2]
    z, gates = rw_out[1], rw_out[2]
    cs = p["pos_dft"][x.shape[1]]
    if _can_fuse_state_dft(x.shape[0], x.shape[1], cs):
        y0, y1, s_fin, f = _state_dft_pass(m, gs, reff, yloc, s0, z, cs)
    else:
        y0, y1, s_fin = _state_pass(m, gs, reff, yloc, s0)
        f = _pos_dft(z, cs)
    x2 = _merge_mlp(f, y0, y1, g, bonus, gates, x, p["mod"], mod_row, layer, p["lnw"], p["lnb"], p["n2"],
                    p["nf"], p["eh"], p["wf"], p["wr"], p["wo"], p["w1"], p["w2"], final_norm)
    return x2, s_fin


def kernel(x, c, ctx, c_ctx, w_mod, b_mod, norm1, norm2, w_in, mu_shift, w0, w_up, a0, a_up, g_up,
           k_k, k_a, r_k, ln_x_w, ln_x_b, w_fourier_up, w_rwkv_up, w_out, mlp_w1, mlp_w2, norm_f):
    depth = w_mod.shape[0]
    batch, seq, d = x.shape
    ctx_len = ctx.shape[1]
    assert d == D_MODEL and batch + 1 <= MOD_ROWS
    assert seq % GRID_W == 0 and ctx_len % CHUNK == 0 and seq % CHUNK == 0

    cc = jnp.zeros((MOD_ROWS, d), F32).at[:batch].set(c).at[batch].set(c_ctx)
    cdft = _channel_dft()
    rows = lambda a: a.reshape(depth, 1, -1)
    p = {
        "eh": _head_ones(),
        "pos_dft": {n: _position_dft(n) for n in {seq, ctx_len}},
        "nf": norm_f.reshape(1, -1),
        "mod": _modulation(cc, w_mod, b_mod).reshape(depth, MOD_ROWS, N_MOD, d),
        "wz": _fold_channel_dft(w_in, cdft),
        "win": _bf(w_in),
        "n1": norm1, "n2": norm2, "mu": mu_shift, "w0": rows(w0), "a0": rows(a0),
        "wup": _two_dir_lora(w_up), "aup": _two_dir_lora(a_up), "gup": _bf(g_up),
        "kkw": k_k, "ka": k_a, "rk": rows(r_k), "lnw": ln_x_w, "lnb": ln_x_b,
        "wf": _bf(w_fourier_up), "wr": _bf(w_rwkv_up), "wo": _bf(w_out),
        "w1": _bf(mlp_w1), "w2": _bf(mlp_w2),
    }
    lat_row = lambda bi: bi
    ctx_row = lambda bi: batch

    x_lat, x_ctx = x, ctx
    s_zero = jnp.zeros((2, batch, HEAD_DIM, R_WIDTH), F32)
    for layer in range(depth):
        last = layer == depth - 1
        x_ctx, s_ctx = _layer(x_ctx, ctx_row, layer, False, not last, False, s_zero, p)
        x_lat, _ = _layer(x_lat, lat_row, layer, True, True, last, s_ctx, p)
    return x_lat
```
